```python
import jax, jax.numpy as jnp
from jax import lax
import numpy as np

D_MODEL = 1024
BATCH = 4
SEQ = 8192
DEPTH = 1

HEAD_DIM = 64
D_MIX = D_MODEL
D_RWKV = D_MIX // 2
D_FOX = D_MIX - D_RWKV
H_RWKV = D_RWKV // HEAD_DIM
H_FOX = D_FOX // HEAD_DIM
DECAY_LORA = 64
AAA_LORA = 64
Q_BLOCK = 128
NORM_EPS = 1e-6
GN_EPS = 64e-5

RW_COLS = 3 * D_RWKV + DECAY_LORA + AAA_LORA
FOX_COLS = 3 * D_FOX + H_FOX
GATE_COLS = D_MIX
D_IN_PROJ = RW_COLS + FOX_COLS + GATE_COLS

kernel_name = "hymba_rwkv7_fox_hybrid"


def rmsnorm(x, g, eps=NORM_EPS):
    xf = x.astype(jnp.float32)
    y = xf * lax.rsqrt(jnp.mean(xf * xf, axis=-1, keepdims=True) + eps)
    return (y * g.astype(jnp.float32)).astype(x.dtype)


def rwkv7_branch(p, mu, w_up, w0, a_up, a0, k_k, k_a, r_k, gn_g, gn_b):
    B, S, _ = p.shape
    dt = p.dtype
    prev = jnp.pad(p, ((0, 0), (1, 0), (0, 0)))[:, :-1]
    p = p + (prev - p) * mu
    r = p[..., 0:D_RWKV]
    k = p[..., D_RWKV:2 * D_RWKV]
    v = p[..., 2 * D_RWKV:3 * D_RWKV]
    wl = p[..., 3 * D_RWKV:3 * D_RWKV + DECAY_LORA]
    al = p[..., 3 * D_RWKV + DECAY_LORA:]
    w = w0 + jnp.tanh(wl) @ w_up
    w = (-jax.nn.softplus(-w) - 0.5).astype(jnp.float32)
    decay = jnp.exp(-jnp.exp(w))
    a = jax.nn.sigmoid(a0 + al @ a_up)

    hs = lambda t: t.reshape(B, S, H_RWKV, HEAD_DIM)
    kk = hs(k * k_k).astype(jnp.float32)
    kk = kk / jnp.maximum(jnp.sqrt(jnp.sum(kk * kk, axis=-1, keepdims=True)), 1e-12)
    k = k * (1.0 + (a - 1.0) * k_a)
    r_h, k_h, v_h, a_h, w_h = hs(r), hs(k), hs(v), hs(a), decay.reshape(B, S, H_RWKV, HEAD_DIM)

    to_time = lambda t: jnp.swapaxes(t.astype(jnp.float32), 0, 1)
    xs = (to_time(r_h), to_time(w_h), to_time(k_h), to_time(v_h), to_time(kk), to_time(a_h))

    def step(state, inp):
        rt, wt, kt, vt, kkt, at = inp
        sa = jnp.einsum('bhvk,bhk->bhv', state, -kkt)
        state = (state * wt[:, :, None, :]
                 + sa[..., :, None] * (kkt * at)[..., None, :]
                 + vt[..., :, None] * kt[..., None, :])
        yt = jnp.einsum('bhvk,bhk->bhv', state, rt)
        return state, yt

    s0 = jnp.zeros((B, H_RWKV, HEAD_DIM, HEAD_DIM), jnp.float32)
    _, y = lax.scan(step, s0, xs)
    y = jnp.swapaxes(y, 0, 1)
    mean = jnp.mean(y, axis=-1, keepdims=True)
    var = jnp.mean(jnp.square(y - mean), axis=-1, keepdims=True)
    y = ((y - mean) * lax.rsqrt(var + GN_EPS)).reshape(B, S, D_RWKV)
    y = y * gn_g.astype(jnp.float32) + gn_b.astype(jnp.float32)
    bonus = jnp.sum((r_h * k_h * r_k).astype(jnp.float32), axis=-1, keepdims=True) * v_h.astype(jnp.float32)
    y = y + bonus.reshape(B, S, D_RWKV)
    return y.astype(dt)


def fox_branch(p, b_f, q_g, k_g):
    B, S, _ = p.shape
    dt = p.dtype
    heads = lambda t: t.reshape(B, S, H_FOX, HEAD_DIM).transpose(0, 2, 1, 3)
    q = rmsnorm(heads(p[..., 0:D_FOX]), q_g)
    k = rmsnorm(heads(p[..., D_FOX:2 * D_FOX]), k_g)
    v = heads(p[..., 2 * D_FOX:3 * D_FOX])
    fl = p[..., 3 * D_FOX:]
    log_f = jax.nn.log_sigmoid((fl + b_f).astype(jnp.float32))
    c = jnp.cumsum(log_f, axis=1).transpose(0, 2, 1)
    scale = HEAD_DIM ** -0.5
    nb = S // Q_BLOCK
    q_blk = q.reshape(B, H_FOX, nb, Q_BLOCK, HEAD_DIM).transpose(2, 0, 1, 3, 4)
    c_blk = c.reshape(B, H_FOX, nb, Q_BLOCK).transpose(2, 0, 1, 3)
    pos = jnp.arange(S, dtype=jnp.int32)
    pos_blk = pos.reshape(nb, Q_BLOCK)
    kf = k.astype(jnp.float32)
    vf = v.astype(jnp.float32)

    def attend(args):
        qb, cb, pb = args
        s = jnp.einsum('bhqd,bhkd->bhqk', qb.astype(jnp.float32), kf) * scale
        s = s + cb[..., :, None] - c[:, :, None, :]
        mask = pos[None, :] <= pb[:, None]
        s = jnp.where(mask[None, None], s, -jnp.inf)
        prob = jax.nn.softmax(s, axis=-1)
        return jnp.einsum('bhqk,bhkd->bhqd', prob, vf).astype(dt)

    o = lax.map(attend, (q_blk, c_blk, pos_blk))
    return o.transpose(1, 0, 3, 2, 4).reshape(B, S, D_FOX)


def setup_inputs(seed: int = 0) -> dict:
    key = jax.random.key(seed)
    ks = jax.random.split(key, 20)
    f32 = jnp.float32
    n = lambda k, shape, s: (jax.random.normal(k, shape, f32) * s)
    L = DEPTH
    return {
        "x": jax.random.normal(ks[0], (BATCH, SEQ, D_MODEL), f32),
        "norm_g": 1.0 + n(ks[1], (L, D_MODEL), 0.02),
        "w_in": n(ks[2], (L, D_MODEL, D_IN_PROJ), D_MODEL ** -0.5),
        "rw_mu": jax.random.uniform(ks[3], (L, RW_COLS), f32, 0.0, 1.0),
        "rw_w_up": n(ks[4], (L, DECAY_LORA, D_RWKV), 0.5 * DECAY_LORA ** -0.5),
        "rw_w0": jax.random.uniform(ks[5], (L, D_RWKV), f32, -5.0, -0.5),
        "rw_a_up": n(ks[6], (L, AAA_LORA, D_RWKV), 0.5 * AAA_LORA ** -0.5),
        "rw_a0": n(ks[7], (L, D_RWKV), 0.1),
        "rw_k_k": 0.85 + n(ks[8], (L, D_RWKV), 0.05),
        "rw_k_a": 1.0 + n(ks[9], (L, D_RWKV), 0.05),
        "rw_r_k": n(ks[10], (L, H_RWKV, HEAD_DIM), 0.1),
        "rw_gn_g": 1.0 + n(ks[11], (L, D_RWKV), 0.02),
        "rw_gn_b": n(ks[12], (L, D_RWKV), 0.01),
        "fox_q_g": 1.0 + n(ks[13], (L, HEAD_DIM), 0.02),
        "fox_k_g": 1.0 + n(ks[14], (L, HEAD_DIM), 0.02),
        "fox_b_f": 3.0 + n(ks[15], (L, H_FOX), 0.5),
        "w_out": n(ks[16], (L, D_MIX, D_MODEL), D_MIX ** -0.5),
        "final_g": 1.0 + n(ks[17], (D_MODEL,), 0.02),
    }


def reference(x, norm_g, w_in, rw_mu, rw_w_up, rw_w0, rw_a_up, rw_a0, rw_k_k, rw_k_a, rw_r_k,
              rw_gn_g, rw_gn_b, fox_q_g, fox_k_g, fox_b_f, w_out, final_g):
    for l in range(DEPTH):
        h = rmsnorm(x, norm_g[l])
        p = h @ w_in[l]
        p_rw = p[..., :RW_COLS]
        p_fox = p[..., RW_COLS:RW_COLS + FOX_COLS]
        gate = p[..., RW_COLS + FOX_COLS:]
        y_rw = rwkv7_branch(p_rw, rw_mu[l], rw_w_up[l], rw_w0[l], rw_a_up[l], rw_a0[l],
                            rw_k_k[l], rw_k_a[l], rw_r_k[l], rw_gn_g[l], rw_gn_b[l])
        y_fox = fox_branch(p_fox, fox_b_f[l], fox_q_g[l], fox_k_g[l])
        y = jnp.concatenate([y_rw, y_fox], axis=-1) * jax.nn.silu(gate)
        x = x + y @ w_out[l]
    return rmsnorm(x, final_g)
```

```python
import functools
import math

import jax
import jax.numpy as jnp
from jax import lax
from jax.experimental import pallas as pl
from jax.experimental.pallas import tpu as pltpu

F32 = jnp.float32
BF16 = jnp.bfloat16
HI = lax.Precision.HIGHEST

HEAD_DIM = 64
NORM_EPS = 1e-6
GN_EPS = 64e-5
CHUNK = 64
LANES = 128
NEG_BIG = -1e30
VMEM_LIMIT = 56 * 1024 * 1024


def _dot(a, b, precision=None):
    return jnp.dot(a, b, preferred_element_type=F32, precision=precision)


def _dot_nt(a, b, precision=None):
    return lax.dot_general(a, b, (((1,), (1,)), ((), ())), preferred_element_type=F32, precision=precision)


def _dot_tn(a, b, precision=None):
    return lax.dot_general(a, b, (((0,), (0,)), ((), ())), preferred_element_type=F32, precision=precision)


def _sigmoid(x):
    return 1.0 / (1.0 + jnp.exp(-x))


def _inproj_body(x_ref, g_ref, w_ref, *out_refs, widths):
    x = x_ref[...]
    y = x * lax.rsqrt(jnp.mean(x * x, axis=-1, keepdims=True) + NORM_EPS)
    h = (y * g_ref[...]).astype(BF16)
    off = 0
    for ref, width in zip(out_refs, widths):
        ref[...] = _dot(h, w_ref[:, off:off + width])
        off += width


def _inproj(x2, g, w_cat, widths, tm):
    n, d = x2.shape
    return pl.pallas_call(
        functools.partial(_inproj_body, widths=widths),
        grid=(n // tm,),
        in_specs=[
            pl.BlockSpec((tm, d), lambda i: (i, 0)),
            pl.BlockSpec((1, d), lambda i: (0, 0)),
            pl.BlockSpec(w_cat.shape, lambda i: (0, 0)),
        ],
        out_specs=[pl.BlockSpec((tm, w), lambda i: (i, 0)) for w in widths],
        out_shape=[jax.ShapeDtypeStruct((n, w), F32) for w in widths],
        compiler_params=pltpu.CompilerParams(
            dimension_semantics=("parallel",), vmem_limit_bytes=VMEM_LIMIT),
        name="inproj",
    )(x2, g, w_cat)


def _unit_lower_inverse(n_strict, row, col):
    eye = (row == col).astype(F32)
    same = lambda s: (row >> s) == (col >> s)
    n0 = jnp.where(same(3), n_strict, 0.0)
    n2 = _dot(n0, n0, HI)
    n4 = _dot(n2, n2, HI)
    x = eye + n0
    x = x + _dot(x, n2, HI)
    x = x + _dot(x, n4, HI)
    for s in (3, 4, 5):
        n_off = jnp.where(same(s + 1) & jnp.logical_not(same(s)), n_strict, 0.0)
        x = x + _dot(_dot(x, n_off, HI), x, HI)
    return x


def _rwkv_body(p_ref, mu_ref, wup_ref, w0_ref, aup_ref, a0_ref, kk_ref, ka_ref, rk_ref,
               gng_ref, gnb_ref, gsum_ref, y_ref, state_ref, prev_ref, *, n_heads, lora):
    c = pl.program_id(1)

    @pl.when(c == 0)
    def _():
        state_ref[...] = jnp.zeros_like(state_ref)
        prev_ref[...] = jnp.zeros_like(prev_ref)

    p = p_ref[0]
    L = p.shape[0]
    d = n_heads * HEAD_DIM
    gsum = gsum_ref[...]

    row_p = lax.broadcasted_iota(jnp.int32, p.shape, 0)
    prev = jnp.where(row_p == 0, prev_ref[...], pltpu.roll(p, 1, axis=0))
    prev_ref[...] = p[L - 1:L, :]
    xs = p + (prev - p) * mu_ref[...]

    r = xs[:, 0:d]
    k = xs[:, d:2 * d]
    v = xs[:, 2 * d:3 * d]
    wl = xs[:, 3 * d:3 * d + lora]
    al = xs[:, 3 * d + lora:3 * d + 2 * lora]

    w_raw = w0_ref[...] + _dot(jnp.tanh(wl), wup_ref[...], HI)
    logw = (-math.exp(-0.5)) * _sigmoid(w_raw)
    a = _sigmoid(a0_ref[...] + _dot(al, aup_ref[...], HI))

    kkr = k * kk_ref[...]
    kk = kkr / jnp.maximum(jnp.sqrt(_dot(kkr * kkr, gsum, HI)), 1e-12)
    k2 = k * (1.0 + (a - 1.0) * ka_ref[...])
    bonus = _dot(r * k2 * rk_ref[...], gsum, HI) * v

    row = lax.broadcasted_iota(jnp.int32, (L, L), 0)
    col = lax.broadcasted_iota(jnp.int32, (L, L), 1)
    incl = row >= col
    strict = row > col
    eye = row == col

    cum = _dot(incl.astype(F32), logw, HI)
    cum_end = cum[L - 1:L, :]
    e_neg = jnp.exp(-cum)
    e_end = jnp.exp(cum_end - cum)
    r_t = r * jnp.exp(cum)
    a_t = -kk * jnp.exp(cum - logw)
    kka = kk * a
    b_t = kka * e_neg
    k_t = k2 * e_neg
    b_e = kka * e_end
    k_e = k2 * e_end
    g_end = jnp.exp(cum_end)

    zeros = jnp.zeros((L, HEAD_DIM), F32)
    ys = []
    for h in range(n_heads):
        sl = slice(h * HEAD_DIM, (h + 1) * HEAD_DIM)
        a_h, r_h, v_h = a_t[:, sl], r_t[:, sl], v[:, sl]
        amat = _dot_nt(jnp.concatenate([a_h, r_h], axis=0),
                       jnp.concatenate([b_t[:, sl], k_t[:, sl]], axis=0), HI)
        a_ab = jnp.where(strict, amat[:L, :L], 0.0)
        a_ak = jnp.where(strict, amat[:L, L:], 0.0)
        a_rb = jnp.where(incl, amat[L:, :L], 0.0)
        a_rk = jnp.where(incl, amat[L:, L:], 0.0)
        t_inv = _unit_lower_inverse(a_ab, row, col)
        wu = _dot(t_inv, jnp.concatenate([a_h, _dot(a_ak, v_h, HI)], axis=1), HI)
        rhs2 = jnp.concatenate([wu, jnp.concatenate([zeros, v_h], axis=1)], axis=0)
        top = _dot(jnp.concatenate([a_rb, a_rk], axis=1), rhs2, HI)
        bot = _dot_tn(jnp.concatenate([b_e[:, sl], k_e[:, sl]], axis=0), rhs2, HI)
        rw = r_h + top[:, :HEAD_DIM]
        m_mat = jnp.where(eye, g_end[:, sl], 0.0) + bot[:, :HEAD_DIM]
        s0 = state_ref[h]
        ys.append(_dot(rw, s0, HI) + top[:, HEAD_DIM:])
        state_ref[h] = _dot(m_mat, s0, HI) + bot[:, HEAD_DIM:]

    y = jnp.concatenate(ys, axis=1)
    inv_n = 1.0 / HEAD_DIM
    mean = _dot(y, gsum, HI) * inv_n
    yc = y - mean
    var = _dot(yc * yc, gsum, HI) * inv_n
    y_ref[0] = (yc * lax.rsqrt(var + GN_EPS)) * gng_ref[...] + gnb_ref[...] + bonus


def _rwkv(p_rw, mu, w_up, w0, a_up, a0, k_k, k_a, r_k, gn_g, gn_b, gsum):
    b, s, cols = p_rw.shape
    lora, d = w_up.shape
    n_heads = d // HEAD_DIM
    row = lambda a: a.reshape(1, -1)
    const = lambda shape: pl.BlockSpec(shape, lambda i, j: (0,) * len(shape))
    return pl.pallas_call(
        functools.partial(_rwkv_body, n_heads=n_heads, lora=lora),
        grid=(b, s // CHUNK),
        in_specs=[
            pl.BlockSpec((1, CHUNK, cols), lambda i, j: (i, j, 0)),
            const((1, cols)), const((lora, d)), const((1, d)), const((lora, d)), const((1, d)),
            const((1, d)), const((1, d)), const((1, d)), const((1, d)), const((1, d)), const((d, d)),
        ],
        out_specs=pl.BlockSpec((1, CHUNK, d), lambda i, j: (i, j, 0)),
        out_shape=jax.ShapeDtypeStruct((b, s, d), F32),
        scratch_shapes=[pltpu.VMEM((n_heads, HEAD_DIM, HEAD_DIM), F32), pltpu.VMEM((1, cols), F32)],
        compiler_params=pltpu.CompilerParams(
            dimension_semantics=("arbitrary", "arbitrary"), vmem_limit_bytes=VMEM_LIMIT),
        name="rwkv7",
    )(p_rw, row(mu), w_up, row(w0), a_up, row(a0), row(k_k), row(k_a), row(r_k), row(gn_g), row(gn_b), gsum)


def _split3(x):
    hi = x.astype(BF16)
    r1 = x - hi.astype(F32)
    mid = r1.astype(BF16)
    lo = (r1 - mid.astype(F32)).astype(BF16)
    return hi.astype(F32), mid.astype(F32), lo.astype(F32)


def _foxprep_body(qkv_ref, fl_ref, qg_ref, kg_ref, bf_ref, gsum_ref, qa_ref, ka_ref, va_ref, carry_ref,
                  *, n_heads):
    @pl.when(pl.program_id(1) == 0)
    def _():
        carry_ref[...] = jnp.zeros_like(carry_ref)

    t = qkv_ref[0]
    tm = t.shape[0]
    d = n_heads * HEAD_DIM
    gsum = gsum_ref[...]
    inv_n = 1.0 / HEAD_DIM

    def headnorm(z, g):
        ms = _dot(z * z, gsum, HI) * inv_n
        return (z * lax.rsqrt(ms + NORM_EPS)) * g

    q = headnorm(t[:, 0:d], qg_ref[...]) * (HEAD_DIM ** -0.5)
    k = headnorm(t[:, d:2 * d], kg_ref[...])
    v = t[:, 2 * d:3 * d]

    z = fl_ref[0] + bf_ref[...]
    log_f = jnp.minimum(z, 0.0) - jnp.log1p(jnp.exp(-jnp.abs(z)))
    row = lax.broadcasted_iota(jnp.int32, (tm, tm), 0)
    col = lax.broadcasted_iota(jnp.int32, (tm, tm), 1)
    cs = _dot((row >= col).astype(F32), log_f, HI) + carry_ref[...]
    carry_ref[...] = cs[tm - 1:tm, :]
    c_hi, c_mid, c_lo = _split3(cs)

    lane = lax.broadcasted_iota(jnp.int32, (tm, LANES), 1)
    is_data = lane < HEAD_DIM
    ones3 = ((lane >= HEAD_DIM + 3) & (lane < HEAD_DIM + 6)).astype(F32)
    ones3_lo = ((lane >= HEAD_DIM) & (lane < HEAD_DIM + 3)).astype(F32)
    one_col = (lane == HEAD_DIM).astype(F32)

    def place(pieces, base):
        out = base
        for off, piece in pieces:
            out = jnp.where(lane == off, piece, out)
        return out

    for h in range(n_heads):
        pair = slice((h // 2) * LANES, (h // 2 + 1) * LANES)
        shift = lambda z: z[:, pair] if h % 2 == 0 else pltpu.roll(z[:, pair], HEAD_DIM, axis=1)
        ch = [jnp.broadcast_to(piece[:, h:h + 1], (tm, LANES)) for piece in (c_hi, c_mid, c_lo)]
        q_extra = place([(HEAD_DIM + i, ch[i]) for i in range(3)], ones3)
        k_extra = place([(HEAD_DIM + 3 + i, -ch[i]) for i in range(3)], ones3_lo)
        qa_ref[0, h] = jnp.where(is_data, shift(q), q_extra).astype(BF16)
        ka_ref[0, h] = jnp.where(is_data, shift(k), k_extra).astype(BF16)
        va_ref[0, h] = jnp.where(is_data, shift(v), one_col).astype(BF16)


def _foxprep(p_qkv, fl, q_g, k_g, b_f, gsum, tm):
    b, s, cols = p_qkv.shape
    d = cols // 3
    n_heads = d // HEAD_DIM
    const = lambda shape: pl.BlockSpec(shape, lambda i, j: (0,) * len(shape))
    out_spec = pl.BlockSpec((1, n_heads, tm, LANES), lambda i, j: (i, 0, j, 0))
    out_shape = jax.ShapeDtypeStruct((b, n_heads, s, LANES), BF16)
    return pl.pallas_call(
        functools.partial(_foxprep_body, n_heads=n_heads),
        grid=(b, s // tm),
        in_specs=[
            pl.BlockSpec((1, tm, cols), lambda i, j: (i, j, 0)),
            pl.BlockSpec((1, tm, LANES), lambda i, j: (i, j, 0)),
            const((1, d)), const((1, d)), const((1, LANES)), const((d, d)),
        ],
        out_specs=[out_spec] * 3,
        out_shape=[out_shape] * 3,
        scratch_shapes=[pltpu.VMEM((1, LANES), F32)],
        compiler_params=pltpu.CompilerParams(
            dimension_semantics=("arbitrary", "arbitrary"), vmem_limit_bytes=VMEM_LIMIT),
        name="foxprep",
    )(p_qkv, fl, q_g, k_g, b_f, gsum)


def _attn_body(q_ref, k_ref, v_ref, o_ref, *, heads_per_step, tq):
    qi = pl.program_id(2)
    row = lax.broadcasted_iota(jnp.int32, (tq, tq), 0)
    col = lax.broadcasted_iota(jnp.int32, (tq, tq), 1)
    causal = col <= row
    outs = []
    for hh in range(heads_per_step):
        q = q_ref[0, hh]

        def block(j, carry, masked):
            m, acc = carry
            kb = k_ref[0, hh, pl.ds(pl.multiple_of(j * tq, tq), tq), :]
            vb = v_ref[0, hh, pl.ds(pl.multiple_of(j * tq, tq), tq), :]
            s = _dot_nt(q, kb)
            if masked:
                s = jnp.where(causal, s, NEG_BIG)
            m_new = jnp.maximum(m, jnp.max(s, axis=-1, keepdims=True))
            p = jnp.exp(s - m_new)
            acc = acc * jnp.exp(m - m_new) + _dot(p.astype(BF16), vb)
            return m_new, acc

        init = (jnp.full((tq, 1), NEG_BIG, F32), jnp.zeros((tq, LANES), F32))
        carry = lax.fori_loop(0, qi, functools.partial(block, masked=False), init)
        _, acc = block(qi, carry, True)
        outs.append(acc[:, :HEAD_DIM] / acc[:, HEAD_DIM:HEAD_DIM + 1])
    o_ref[0] = jnp.concatenate(outs, axis=-1)


def _attention(qa, ka, va, tq):
    b, h, s, _ = qa.shape
    hps = LANES // HEAD_DIM
    return pl.pallas_call(
        functools.partial(_attn_body, heads_per_step=hps, tq=tq),
        grid=(b, h // hps, s // tq),
        in_specs=[
            pl.BlockSpec((1, hps, tq, LANES), lambda i, g, j: (i, g, j, 0)),
            pl.BlockSpec((1, hps, s, LANES), lambda i, g, j: (i, g, 0, 0)),
            pl.BlockSpec((1, hps, s, LANES), lambda i, g, j: (i, g, 0, 0)),
        ],
        out_specs=pl.BlockSpec((1, tq, LANES), lambda i, g, j: (i, j, g)),
        out_shape=jax.ShapeDtypeStruct((b, s, h * HEAD_DIM), F32),
        compiler_params=pltpu.CompilerParams(
            dimension_semantics=("parallel", "parallel", "arbitrary"), vmem_limit_bytes=VMEM_LIMIT),
        name="fox_attention",
    )(qa, ka, va)


def _outproj_body(x_ref, yrw_ref, yfox_ref, gate_ref, w_ref, fg_ref, o_ref, *, final):
    g = gate_ref[...]
    sg = g * _sigmoid(g)
    d_rw = yrw_ref.shape[1]
    y_rw = (yrw_ref[...] * sg[:, :d_rw]).astype(BF16)
    y_fox = (yfox_ref[...] * sg[:, d_rw:]).astype(BF16)
    z = x_ref[...] + _dot(y_rw, w_ref[0:d_rw, :]) + _dot(y_fox, w_ref[d_rw:, :])
    if final:
        z = (z * lax.rsqrt(jnp.mean(z * z, axis=-1, keepdims=True) + NORM_EPS)) * fg_ref[...]
    o_ref[...] = z


def _outproj(x2, y_rw, y_fox, gate, w_out, final_g, final, tm):
    n, d = x2.shape
    d_rw, d_fox = y_rw.shape[1], y_fox.shape[1]
    return pl.pallas_call(
        functools.partial(_outproj_body, final=final),
        grid=(n // tm,),
        in_specs=[
            pl.BlockSpec((tm, d), lambda i: (i, 0)),
            pl.BlockSpec((tm, d_rw), lambda i: (i, 0)),
            pl.BlockSpec((tm, d_fox), lambda i: (i, 0)),
            pl.BlockSpec((tm, d_rw + d_fox), lambda i: (i, 0)),
            pl.BlockSpec(w_out.shape, lambda i: (0, 0)),
            pl.BlockSpec((1, d), lambda i: (0, 0)),
        ],
        out_specs=pl.BlockSpec((tm, d), lambda i: (i, 0)),
        out_shape=jax.ShapeDtypeStruct((n, d), F32),
        compiler_params=pltpu.CompilerParams(
            dimension_semantics=("parallel",), vmem_limit_bytes=VMEM_LIMIT),
        name="outproj",
    )(x2, y_rw, y_fox, gate, w_out, final_g)


def _pick_tile(n, target):
    t = min(n, target)
    while n % t:
        t //= 2
    return t


def kernel(x, norm_g, w_in, rw_mu, rw_w_up, rw_w0, rw_a_up, rw_a0, rw_k_k, rw_k_a, rw_r_k, rw_gn_g, rw_gn_b,
           fox_q_g, fox_k_g, fox_b_f, w_out, final_g):
    b, s, d_model = x.shape
    depth = norm_g.shape[0]
    d_rw = rw_w0.shape[-1]
    lora = rw_w_up.shape[1]
    d_mix = w_out.shape[1]
    d_fox = d_mix - d_rw
    h_fox = fox_b_f.shape[-1]
    rw_cols = 3 * d_rw + 2 * lora
    fox_cols = 3 * d_fox + h_fox
    assert d_rw == d_fox and s % CHUNK == 0 and h_fox <= LANES

    head = lax.broadcasted_iota(jnp.int32, (d_rw, d_rw), 0) // HEAD_DIM
    gsum = (head == head.T).astype(F32)

    tm_proj = _pick_tile(b * s, 256)
    tm_prep = _pick_tile(s, 256)
    tq = _pick_tile(s, 512)

    x2 = x.reshape(b * s, d_model)
    for l in range(depth):
        w = w_in[l]
        w_fl = jnp.pad(w[:, rw_cols + 3 * d_fox:rw_cols + fox_cols], ((0, 0), (0, LANES - h_fox)))
        w_cat = jnp.concatenate(
            [w[:, :rw_cols], w[:, rw_cols:rw_cols + 3 * d_fox], w[:, rw_cols + fox_cols:], w_fl], axis=1
        ).astype(BF16)
        widths = (rw_cols, 3 * d_fox, d_mix, LANES)
        p_rw, p_qkv, gate, fl = _inproj(x2, norm_g[l].reshape(1, -1), w_cat, widths, tm_proj)

        y_rw = _rwkv(p_rw.reshape(b, s, rw_cols), rw_mu[l], rw_w_up[l], rw_w0[l], rw_a_up[l], rw_a0[l],
                     rw_k_k[l], rw_k_a[l], rw_r_k[l].reshape(-1), rw_gn_g[l], rw_gn_b[l], gsum)

        tile_heads = lambda g: jnp.tile(g, d_fox // HEAD_DIM).reshape(1, -1)
        b_f = jnp.pad(fox_b_f[l], (0, LANES - h_fox)).reshape(1, -1)
        qa, ka, va = _foxprep(p_qkv.reshape(b, s, 3 * d_fox), fl.reshape(b, s, LANES),
                              tile_heads(fox_q_g[l]), tile_heads(fox_k_g[l]), b_f, gsum, tm_prep)
        y_fox = _attention(qa, ka, va, tq)

        x2 = _outproj(x2, y_rw.reshape(b * s, d_rw), y_fox.reshape(b * s, d_fox), gate,
                      w_out[l].astype(BF16), final_g.reshape(1, -1), l == depth - 1, tm_proj)
    return x2.reshape(b, s, d_model)
```

```python
import functools
import math

import jax
import jax.numpy as jnp
from jax import lax
from jax.experimental import pallas as pl
from jax.experimental.pallas import tpu as pltpu

F32 = jnp.float32
BF16 = jnp.bfloat16
HI = lax.Precision.HIGHEST

HEAD_DIM = 64
NORM_EPS = 1e-6
GN_EPS = 64e-5
CHUNK = 64
RWKV_CHUNKS_PER_STEP = 1
LANES = 128
NEG_BIG = -1e30
VMEM_LIMIT = 56 * 1024 * 1024


def _dot(a, b, precision=None):
    return jnp.dot(a, b, preferred_element_type=F32, precision=precision)


def _dot_nt(a, b, precision=None):
    return lax.dot_general(a, b, (((1,), (1,)), ((), ())), preferred_element_type=F32, precision=precision)


def _dot_tn(a, b, precision=None):
    return lax.dot_general(a, b, (((0,), (0,)), ((), ())), preferred_element_type=F32, precision=precision)


def _sigmoid(x):
    return 1.0 / (1.0 + jnp.exp(-x))


def _inproj_body(x_ref, g_ref, w_ref, *out_refs, widths):
    x = x_ref[...]
    y = x * lax.rsqrt(jnp.mean(x * x, axis=-1, keepdims=True) + NORM_EPS)
    h = (y * g_ref[...]).astype(BF16)
    off = 0
    for ref, width in zip(out_refs, widths):
        ref[...] = _dot(h, w_ref[:, off:off + width])
        off += width


def _inproj(x2, g, w_cat, widths, tm):
    n, d = x2.shape
    return pl.pallas_call(
        functools.partial(_inproj_body, widths=widths),
        grid=(n // tm,),
        in_specs=[
            pl.BlockSpec((tm, d), lambda i: (i, 0)),
            pl.BlockSpec((1, d), lambda i: (0, 0)),
            pl.BlockSpec(w_cat.shape, lambda i: (0, 0)),
        ],
        out_specs=[pl.BlockSpec((tm, w), lambda i: (i, 0)) for w in widths],
        out_shape=[jax.ShapeDtypeStruct((n, w), F32) for w in widths],
        compiler_params=pltpu.CompilerParams(
            dimension_semantics=("parallel",), vmem_limit_bytes=VMEM_LIMIT),
        name="inproj",
    )(x2, g, w_cat)


def _bf(x):
    return x.astype(BF16)


def _split2(x):
    hi = x.astype(BF16)
    return hi, (x - hi.astype(F32)).astype(BF16)


def _dot_x3(a, b):
    ah, al = _split2(a)
    bh, bl = _split2(b)
    return _dot(ah, bh) + (_dot(ah, bl) + _dot(al, bh))


def _dot_pieces(x, ones_bf16, pieces):
    out = None
    for _ in range(pieces):
        hi = x.astype(BF16)
        term = _dot(hi, ones_bf16)
        out = term if out is None else out + term
        x = x - hi.astype(F32)
    return out


def _headsum(x, ones_bd):
    rows, width = x.shape
    n_pairs = width // LANES
    stacked = jnp.concatenate([x[:, j * LANES:(j + 1) * LANES] for j in range(n_pairs)], axis=0)
    s = _dot_pieces(stacked, ones_bd, 2)
    return jnp.concatenate([s[j * rows:(j + 1) * rows] for j in range(n_pairs)], axis=1)


def _rwkv_body(p_ref, mu_ref, wup_ref, w0_ref, aup_ref, a0_ref, kk_ref, ka_ref, rk_ref,
               gng_ref, gnb_ref, ones_ref, y_ref, state_ref, prev_ref, *, lora, n_chunks):
    @pl.when(pl.program_id(1) == 0)
    def _():
        state_ref[...] = jnp.zeros_like(state_ref)
        prev_ref[...] = jnp.zeros_like(prev_ref)

    p = p_ref[0]
    R = p.shape[0]
    L = CHUNK
    d = y_ref.shape[-1]
    n_pairs = d // LANES
    ones_bd = ones_ref[...]

    row_p = lax.broadcasted_iota(jnp.int32, p.shape, 0)
    prev = jnp.where(row_p == 0, prev_ref[...], pltpu.roll(p, 1, axis=0))
    prev_ref[...] = p[R - 1:R, :]
    xs = p + (prev - p) * mu_ref[...]

    r = xs[:, 0:d]
    k = xs[:, d:2 * d]
    v = xs[:, 2 * d:3 * d]
    wl = xs[:, 3 * d:3 * d + lora]
    al = xs[:, 3 * d + lora:3 * d + 2 * lora]

    w_raw = w0_ref[...] + _dot_x3(jnp.tanh(wl), wup_ref[...])
    logw = (-math.exp(-0.5)) * _sigmoid(w_raw)
    a = _sigmoid(a0_ref[...] + _dot_x3(al, aup_ref[...]))

    kkr = k * kk_ref[...]
    kk = kkr / jnp.maximum(jnp.sqrt(_headsum(kkr * kkr, ones_bd)), 1e-12)
    k2 = k * (1.0 + (a - 1.0) * ka_ref[...])
    bonus = _headsum(r * k2 * rk_ref[...], ones_bd) * v

    row_r = lax.broadcasted_iota(jnp.int32, (R, R), 0)
    col_r = lax.broadcasted_iota(jnp.int32, (R, R), 1)
    tri = jnp.where(((row_r >> 6) == (col_r >> 6)) & (row_r >= col_r), 1.0, 0.0).astype(BF16)
    cum = _cumsum_rows(tri, logw)
    cum_end = jnp.concatenate(
        [jnp.broadcast_to(cum[(c + 1) * L - 1:(c + 1) * L, :], (L, d)) for c in range(n_chunks)], axis=0)
    e_neg = jnp.exp(-cum)
    e_end = jnp.exp(cum_end - cum)
    g_end = jnp.exp(cum_end)
    r_t = r * jnp.exp(cum)
    a_t = -kk * jnp.exp(cum - logw)
    kka = kk * a
    b_t = kka * e_neg
    k_t = k2 * e_neg
    b_e = kka * e_end
    k_e = k2 * e_end

    P = LANES
    chains = [(c, j) for c in range(n_chunks) for j in range(n_pairs)]
    pair = lambda z, c, j: z[c * L:(c + 1) * L, j * P:(j + 1) * P]
    lane = lax.broadcasted_iota(jnp.int32, (L, P), 1)
    first = lane < HEAD_DIM
    stack = lambda z: jnp.concatenate([jnp.where(first, z, 0.0), jnp.where(first, 0.0, z)], axis=0)

    row = lax.broadcasted_iota(jnp.int32, (P, P), 0)
    col = lax.broadcasted_iota(jnp.int32, (P, P), 1)
    same = lambda s: (row >> s) == (col >> s)
    strict = same(6) & (row > col)
    incl = same(6) & (row >= col)
    eye = row == col
    eye_f = eye.astype(F32)

    amat = [_dot_nt(_bf(jnp.concatenate([stack(pair(a_t, c, j)), stack(pair(r_t, c, j))], axis=0)),
                    _bf(jnp.concatenate([pair(b_t, c, j)] * 2 + [pair(k_t, c, j)] * 2, axis=0)))
            for c, j in chains]
    n_st = [jnp.where(strict, m[:P, :P], 0.0) for m in amat]
    a_ak = [_bf(jnp.where(strict, m[:P, P:], 0.0)) for m in amat]
    a_r = [_bf(jnp.concatenate([jnp.where(incl, m[P:, :P], 0.0), jnp.where(incl, m[P:, P:], 0.0)], axis=1))
           for m in amat]

    n0 = [_bf(jnp.where(same(3), n, 0.0)) for n in n_st]
    n2 = [_bf(_dot(n, n)) for n in n0]
    n4 = [_bf(_dot(n, n)) for n in n2]
    t = [eye_f + n.astype(F32) for n in n0]
    t = [x + _dot(_bf(x), m) for x, m in zip(t, n2)]
    t = [x + _dot(_bf(x), m) for x, m in zip(t, n4)]
    for s in (3, 4, 5):
        off_mask = same(s + 1) & jnp.logical_not(same(s))
        n_off = [_bf(jnp.where(off_mask, n, 0.0)) for n in n_st]
        tb = [_bf(x) for x in t]
        half = [_bf(_dot(x, m)) for x, m in zip(tb, n_off)]
        t = [x + _dot(h, xb) for x, h, xb in zip(t, half, tb)]
    tb = [_bf(x) for x in t]

    v_st = [_bf(stack(pair(v, c, j))) for c, j in chains]
    x_st = [_dot(m, vs) for m, vs in zip(a_ak, v_st)]
    wu = [_dot(x, jnp.concatenate([_bf(stack(pair(a_t, c, j))), _bf(xs_)], axis=1))
          for x, xs_, (c, j) in zip(tb, x_st, chains)]
    zeros_b = jnp.zeros((P, P), BF16)
    rhs2 = [jnp.concatenate([_bf(w), jnp.concatenate([zeros_b, vs], axis=1)], axis=0)
            for w, vs in zip(wu, v_st)]
    top = [_dot(m, rh) for m, rh in zip(a_r, rhs2)]
    bot = [_dot_tn(_bf(jnp.concatenate([stack(pair(b_e, c, j)), stack(pair(k_e, c, j))], axis=0)), rh)
           for rh, (c, j) in zip(rhs2, chains)]

    fold = lambda z: z[:L] + z[L:]
    ys = {}
    for idx, (c, j) in enumerate(chains):
        rw = pair(r_t, c, j) + fold(top[idx][:, :P])
        m_mat = jnp.where(eye, pair(g_end, c, j)[0:1, :], 0.0) + bot[idx][:, :P]
        s0 = _bf(state_ref[j])
        ys[(c, j)] = _dot(_bf(rw), s0) + fold(top[idx][:, P:])
        state_ref[j] = _dot(_bf(m_mat), s0) + bot[idx][:, P:]

    y = jnp.concatenate(
        [jnp.concatenate([ys[(c, j)] for j in range(n_pairs)], axis=1) for c in range(n_chunks)], axis=0)
    inv_n = 1.0 / HEAD_DIM
    mean = _headsum(y, ones_bd) * inv_n
    yc = y - mean
    var = _headsum(yc * yc, ones_bd) * inv_n
    y_ref[0] = (yc * lax.rsqrt(var + GN_EPS)) * gng_ref[...] + gnb_ref[...] + bonus


def _cumsum_rows(tri_bf16, x):
    out = None
    for _ in range(3):
        hi = x.astype(BF16)
        term = _dot(tri_bf16, hi)
        out = term if out is None else out + term
        x = x - hi.astype(F32)
    return out


def _rwkv(p_rw, mu, w_up, w0, a_up, a0, k_k, k_a, r_k, gn_g, gn_b, ones_bd, n_chunks):
    b, s, cols = p_rw.shape
    lora, d = w_up.shape
    rows = n_chunks * CHUNK
    row = lambda a: a.reshape(1, -1)
    const = lambda shape: pl.BlockSpec(shape, lambda i, j: (0,) * len(shape))
    return pl.pallas_call(
        functools.partial(_rwkv_body, lora=lora, n_chunks=n_chunks),
        grid=(b, s // rows),
        in_specs=[
            pl.BlockSpec((1, rows, cols), lambda i, j: (i, j, 0)),
            const((1, cols)), const((lora, d)), const((1, d)), const((lora, d)), const((1, d)),
            const((1, d)), const((1, d)), const((1, d)), const((1, d)), const((1, d)), const((LANES, LANES)),
        ],
        out_specs=pl.BlockSpec((1, rows, d), lambda i, j: (i, j, 0)),
        out_shape=jax.ShapeDtypeStruct((b, s, d), F32),
        scratch_shapes=[pltpu.VMEM((d // LANES, LANES, LANES), F32), pltpu.VMEM((1, cols), F32)],
        compiler_params=pltpu.CompilerParams(
            dimension_semantics=("arbitrary", "arbitrary"), vmem_limit_bytes=VMEM_LIMIT),
        name="rwkv7",
    )(p_rw, row(mu), w_up, row(w0), a_up, row(a0), row(k_k), row(k_a), row(r_k), row(gn_g), row(gn_b), ones_bd)


def _split3(x):
    hi = x.astype(BF16)
    r1 = x - hi.astype(F32)
    mid = r1.astype(BF16)
    lo = (r1 - mid.astype(F32)).astype(BF16)
    return hi.astype(F32), mid.astype(F32), lo.astype(F32)


def _foxprep_body(qkv_ref, fl_ref, qg_ref, kg_ref, bf_ref, gsum_ref, qa_ref, ka_ref, va_ref, carry_ref,
                  *, n_heads):
    @pl.when(pl.program_id(1) == 0)
    def _():
        carry_ref[...] = jnp.zeros_like(carry_ref)

    t = qkv_ref[0]
    tm = t.shape[0]
    d = n_heads * HEAD_DIM
    gsum = gsum_ref[...]
    inv_n = 1.0 / HEAD_DIM

    def headnorm(z, g):
        ms = _dot(z * z, gsum, HI) * inv_n
        return (z * lax.rsqrt(ms + NORM_EPS)) * g

    q = headnorm(t[:, 0:d], qg_ref[...]) * (HEAD_DIM ** -0.5)
    k = headnorm(t[:, d:2 * d], kg_ref[...])
    v = t[:, 2 * d:3 * d]

    z = fl_ref[0] + bf_ref[...]
    log_f = jnp.minimum(z, 0.0) - jnp.log1p(jnp.exp(-jnp.abs(z)))
    row = lax.broadcasted_iota(jnp.int32, (tm, tm), 0)
    col = lax.broadcasted_iota(jnp.int32, (tm, tm), 1)
    cs = _dot((row >= col).astype(F32), log_f, HI) + carry_ref[...]
    carry_ref[...] = cs[tm - 1:tm, :]
    c_hi, c_mid, c_lo = _split3(cs)

    lane = lax.broadcasted_iota(jnp.int32, (tm, LANES), 1)
    is_data = lane < HEAD_DIM
    ones3 = ((lane >= HEAD_DIM + 3) & (lane < HEAD_DIM + 6)).astype(F32)
    ones3_lo = ((lane >= HEAD_DIM) & (lane < HEAD_DIM + 3)).astype(F32)
    one_col = (lane == HEAD_DIM).astype(F32)

    def place(pieces, base):
        out = base
        for off, piece in pieces:
            out = jnp.where(lane == off, piece, out)
        return out

    for h in range(n_heads):
        pair = slice((h // 2) * LANES, (h // 2 + 1) * LANES)
        shift = lambda z: z[:, pair] if h % 2 == 0 else pltpu.roll(z[:, pair], HEAD_DIM, axis=1)
        ch = [jnp.broadcast_to(piece[:, h:h + 1], (tm, LANES)) for piece in (c_hi, c_mid, c_lo)]
        q_extra = place([(HEAD_DIM + i, ch[i]) for i in range(3)], ones3)
        k_extra = place([(HEAD_DIM + 3 + i, -ch[i]) for i in range(3)], ones3_lo)
        qa_ref[0, h] = jnp.where(is_data, shift(q), q_extra).astype(BF16)
        ka_ref[0, h] = jnp.where(is_data, shift(k), k_extra).astype(BF16)
        va_ref[0, h] = jnp.where(is_data, shift(v), one_col).astype(BF16)


def _foxprep(p_qkv, fl, q_g, k_g, b_f, gsum, tm):
    b, s, cols = p_qkv.shape
    d = cols // 3
    n_heads = d // HEAD_DIM
    const = lambda shape: pl.BlockSpec(shape, lambda i, j: (0,) * len(shape))
    out_spec = pl.BlockSpec((1, n_heads, tm, LANES), lambda i, j: (i, 0, j, 0))
    out_shape = jax.ShapeDtypeStruct((b, n_heads, s, LANES), BF16)
    return pl.pallas_call(
        functools.partial(_foxprep_body, n_heads=n_heads),
        grid=(b, s // tm),
        in_specs=[
            pl.BlockSpec((1, tm, cols), lambda i, j: (i, j, 0)),
            pl.BlockSpec((1, tm, LANES), lambda i, j: (i, j, 0)),
            const((1, d)), const((1, d)), const((1, LANES)), const((d, d)),
        ],
        out_specs=[out_spec] * 3,
        out_shape=[out_shape] * 3,
        scratch_shapes=[pltpu.VMEM((1, LANES), F32)],
        compiler_params=pltpu.CompilerParams(
            dimension_semantics=("arbitrary", "arbitrary"), vmem_limit_bytes=VMEM_LIMIT),
        name="foxprep",
    )(p_qkv, fl, q_g, k_g, b_f, gsum)


def _attn_body(q_ref, k_ref, v_ref, o_ref, *, heads_per_step, tq):
    qi = pl.program_id(2)
    row = lax.broadcasted_iota(jnp.int32, (tq, tq), 0)
    col = lax.broadcasted_iota(jnp.int32, (tq, tq), 1)
    causal = col <= row
    outs = []
    for hh in range(heads_per_step):
        q = q_ref[0, hh]

        def block(j, carry, masked):
            m, acc = carry
            kb = k_ref[0, hh, pl.ds(pl.multiple_of(j * tq, tq), tq), :]
            vb = v_ref[0, hh, pl.ds(pl.multiple_of(j * tq, tq), tq), :]
            s = _dot_nt(q, kb)
            if masked:
                s = jnp.where(causal, s, NEG_BIG)
            m_new = jnp.maximum(m, jnp.max(s, axis=-1, keepdims=True))
            p = jnp.exp(s - m_new)
            acc = acc * jnp.exp(m - m_new) + _dot(p.astype(BF16), vb)
            return m_new, acc

        init = (jnp.full((tq, 1), NEG_BIG, F32), jnp.zeros((tq, LANES), F32))
        carry = lax.fori_loop(0, qi, functools.partial(block, masked=False), init)
        _, acc = block(qi, carry, True)
        outs.append(acc[:, :HEAD_DIM] / acc[:, HEAD_DIM:HEAD_DIM + 1])
    o_ref[0] = jnp.concatenate(outs, axis=-1)


def _attention(qa, ka, va, tq):
    b, h, s, _ = qa.shape
    hps = LANES // HEAD_DIM
    return pl.pallas_call(
        functools.partial(_attn_body, heads_per_step=hps, tq=tq),
        grid=(b, h // hps, s // tq),
        in_specs=[
            pl.BlockSpec((1, hps, tq, LANES), lambda i, g, j: (i, g, j, 0)),
            pl.BlockSpec((1, hps, s, LANES), lambda i, g, j: (i, g, 0, 0)),
            pl.BlockSpec((1, hps, s, LANES), lambda i, g, j: (i, g, 0, 0)),
        ],
        out_specs=pl.BlockSpec((1, tq, LANES), lambda i, g, j: (i, j, g)),
        out_shape=jax.ShapeDtypeStruct((b, s, h * HEAD_DIM), F32),
        compiler_params=pltpu.CompilerParams(
            dimension_semantics=("parallel", "parallel", "arbitrary"), vmem_limit_bytes=VMEM_LIMIT),
        name="fox_attention",
    )(qa, ka, va)


def _outproj_body(x_ref, yrw_ref, yfox_ref, gate_ref, w_ref, fg_ref, o_ref, *, final):
    g = gate_ref[...]
    sg = g * _sigmoid(g)
    d_rw = yrw_ref.shape[1]
    y_rw = (yrw_ref[...] * sg[:, :d_rw]).astype(BF16)
    y_fox = (yfox_ref[...] * sg[:, d_rw:]).astype(BF16)
    z = x_ref[...] + _dot(y_rw, w_ref[0:d_rw, :]) + _dot(y_fox, w_ref[d_rw:, :])
    if final:
        z = (z * lax.rsqrt(jnp.mean(z * z, axis=-1, keepdims=True) + NORM_EPS)) * fg_ref[...]
    o_ref[...] = z


def _outproj(x2, y_rw, y_fox, gate, w_out, final_g, final, tm):
    n, d = x2.shape
    d_rw, d_fox = y_rw.shape[1], y_fox.shape[1]
    return pl.pallas_call(
        functools.partial(_outproj_body, final=final),
        grid=(n // tm,),
        in_specs=[
            pl.BlockSpec((tm, d), lambda i: (i, 0)),
            pl.BlockSpec((tm, d_rw), lambda i: (i, 0)),
            pl.BlockSpec((tm, d_fox), lambda i: (i, 0)),
            pl.BlockSpec((tm, d_rw + d_fox), lambda i: (i, 0)),
            pl.BlockSpec(w_out.shape, lambda i: (0, 0)),
            pl.BlockSpec((1, d), lambda i: (0, 0)),
        ],
        out_specs=pl.BlockSpec((tm, d), lambda i: (i, 0)),
        out_shape=jax.ShapeDtypeStruct((n, d), F32),
        compiler_params=pltpu.CompilerParams(
            dimension_semantics=("parallel",), vmem_limit_bytes=VMEM_LIMIT),
        name="outproj",
    )(x2, y_rw, y_fox, gate, w_out, final_g)


def _pick_tile(n, target):
    t = min(n, target)
    while n % t:
        t //= 2
    return t


def kernel(x, norm_g, w_in, rw_mu, rw_w_up, rw_w0, rw_a_up, rw_a0, rw_k_k, rw_k_a, rw_r_k, rw_gn_g, rw_gn_b,
           fox_q_g, fox_k_g, fox_b_f, w_out, final_g):
    b, s, d_model = x.shape
    depth = norm_g.shape[0]
    d_rw = rw_w0.shape[-1]
    lora = rw_w_up.shape[1]
    d_mix = w_out.shape[1]
    d_fox = d_mix - d_rw
    h_fox = fox_b_f.shape[-1]
    rw_cols = 3 * d_rw + 2 * lora
    fox_cols = 3 * d_fox + h_fox
    assert d_rw == d_fox and s % CHUNK == 0 and h_fox <= LANES

    head = lax.broadcasted_iota(jnp.int32, (d_rw, d_rw), 0) // HEAD_DIM
    gsum = (head == head.T).astype(F32)
    ones_bd = gsum[:LANES, :LANES].astype(BF16)

    tm_proj = _pick_tile(b * s, 256)
    tm_prep = _pick_tile(s, 256)
    tq = _pick_tile(s, 512)

    x2 = x.reshape(b * s, d_model)
    for l in range(depth):
        w = w_in[l]
        w_fl = jnp.pad(w[:, rw_cols + 3 * d_fox:rw_cols + fox_cols], ((0, 0), (0, LANES - h_fox)))
        w_cat = jnp.concatenate(
            [w[:, :rw_cols], w[:, rw_cols:rw_cols + 3 * d_fox], w[:, rw_cols + fox_cols:], w_fl], axis=1
        ).astype(BF16)
        widths = (rw_cols, 3 * d_fox, d_mix, LANES)
        p_rw, p_qkv, gate, fl = _inproj(x2, norm_g[l].reshape(1, -1), w_cat, widths, tm_proj)

        y_rw = _rwkv(p_rw.reshape(b, s, rw_cols), rw_mu[l], rw_w_up[l], rw_w0[l], rw_a_up[l], rw_a0[l],
                     rw_k_k[l], rw_k_a[l], rw_r_k[l].reshape(-1), rw_gn_g[l], rw_gn_b[l], ones_bd,
                     RWKV_CHUNKS_PER_STEP if s % (RWKV_CHUNKS_PER_STEP * CHUNK) == 0 else 1)

        tile_heads = lambda g: jnp.tile(g, d_fox // HEAD_DIM).reshape(1, -1)
        b_f = jnp.pad(fox_b_f[l], (0, LANES - h_fox)).reshape(1, -1)
        qa, ka, va = _foxprep(p_qkv.reshape(b, s, 3 * d_fox), fl.reshape(b, s, LANES),
                              tile_heads(fox_q_g[l]), tile_heads(fox_k_g[l]), b_f, gsum, tm_prep)
        y_fox = _attention(qa, ka, va, tq)

        x2 = _outproj(x2, y_rw.reshape(b * s, d_rw), y_fox.reshape(b * s, d_fox), gate,
                      w_out[l].astype(BF16), final_g.reshape(1, -1), l == depth - 1, tm_proj)
    return x2.reshape(b, s, d_model)
```

```python
import functools
import math

import jax
import jax.numpy as jnp
from jax import lax
from jax.experimental import pallas as pl
from jax.experimental.pallas import tpu as pltpu

F32 = jnp.float32
BF16 = jnp.bfloat16
HI = lax.Precision.HIGHEST

HEAD_DIM = 64
NORM_EPS = 1e-6
GN_EPS = 64e-5
CHUNK = 64
RWKV_CHUNKS_PER_STEP = 1
LANES = 128
NEG_BIG = -1e30
SAFE_SCORE_BOUND = 40.0
VMEM_LIMIT = 56 * 1024 * 1024


def _dot(a, b, precision=None):
    return jnp.dot(a, b, preferred_element_type=F32, precision=precision)


def _dot_nt(a, b, precision=None):
    return lax.dot_general(a, b, (((1,), (1,)), ((), ())), preferred_element_type=F32, precision=precision)


def _dot_tn(a, b, precision=None):
    return lax.dot_general(a, b, (((0,), (0,)), ((), ())), preferred_element_type=F32, precision=precision)


def _sigmoid(x):
    return 1.0 / (1.0 + jnp.exp(-x))


def _inproj_body(x_ref, g_ref, w_ref, *out_refs, widths):
    x = x_ref[...]
    y = x * lax.rsqrt(jnp.mean(x * x, axis=-1, keepdims=True) + NORM_EPS)
    h = (y * g_ref[...]).astype(BF16)
    off = 0
    for ref, width in zip(out_refs, widths):
        ref[...] = _dot(h, w_ref[:, off:off + width])
        off += width


def _inproj(x2, g, w_cat, widths, tm):
    n, d = x2.shape
    return pl.pallas_call(
        functools.partial(_inproj_body, widths=widths),
        grid=(n // tm,),
        in_specs=[
            pl.BlockSpec((tm, d), lambda i: (i, 0)),
            pl.BlockSpec((1, d), lambda i: (0, 0)),
            pl.BlockSpec(w_cat.shape, lambda i: (0, 0)),
        ],
        out_specs=[pl.BlockSpec((tm, w), lambda i: (i, 0)) for w in widths],
        out_shape=[jax.ShapeDtypeStruct((n, w), F32) for w in widths],
        compiler_params=pltpu.CompilerParams(
            dimension_semantics=("parallel",), vmem_limit_bytes=VMEM_LIMIT),
        name="inproj",
    )(x2, g, w_cat)


def _bf(x):
    return x.astype(BF16)


def _split2(x):
    hi = x.astype(BF16)
    return hi, (x - hi.astype(F32)).astype(BF16)


def _dot_x3(a, b):
    ah, al = _split2(a)
    bh, bl = _split2(b)
    return _dot(ah, bh) + (_dot(ah, bl) + _dot(al, bh))


def _dot_pieces(x, ones_bf16, pieces):
    out = None
    for _ in range(pieces):
        hi = x.astype(BF16)
        term = _dot(hi, ones_bf16)
        out = term if out is None else out + term
        x = x - hi.astype(F32)
    return out


def _headsum(x, ones_bd):
    rows, width = x.shape
    n_pairs = width // LANES
    stacked = jnp.concatenate([x[:, j * LANES:(j + 1) * LANES] for j in range(n_pairs)], axis=0)
    s = _dot_pieces(stacked, ones_bd, 2)
    return jnp.concatenate([s[j * rows:(j + 1) * rows] for j in range(n_pairs)], axis=1)


def _rwkv_body(p_ref, mu_ref, wup_ref, w0_ref, aup_ref, a0_ref, kk_ref, ka_ref, rk_ref,
               gng_ref, gnb_ref, ones_ref, y_ref, state_ref, prev_ref, *, lora, n_chunks):
    @pl.when(pl.program_id(1) == 0)
    def _():
        state_ref[...] = jnp.zeros_like(state_ref)
        prev_ref[...] = jnp.zeros_like(prev_ref)

    p = p_ref[0]
    R = p.shape[0]
    L = CHUNK
    d = y_ref.shape[-1]
    n_pairs = d // LANES
    ones_bd = ones_ref[...]

    row_p = lax.broadcasted_iota(jnp.int32, p.shape, 0)
    prev = jnp.where(row_p == 0, prev_ref[...], pltpu.roll(p, 1, axis=0))
    prev_ref[...] = p[R - 1:R, :]
    xs = p + (prev - p) * mu_ref[...]

    r = xs[:, 0:d]
    k = xs[:, d:2 * d]
    v = xs[:, 2 * d:3 * d]
    wl = xs[:, 3 * d:3 * d + lora]
    al = xs[:, 3 * d + lora:3 * d + 2 * lora]

    w_raw = w0_ref[...] + _dot_x3(jnp.tanh(wl), wup_ref[...])
    logw = (-math.exp(-0.5)) * _sigmoid(w_raw)
    a = _sigmoid(a0_ref[...] + _dot_x3(al, aup_ref[...]))

    kkr = k * kk_ref[...]
    kk = kkr / jnp.maximum(jnp.sqrt(_headsum(kkr * kkr, ones_bd)), 1e-12)
    k2 = k * (1.0 + (a - 1.0) * ka_ref[...])
    bonus = _headsum(r * k2 * rk_ref[...], ones_bd) * v

    row_r = lax.broadcasted_iota(jnp.int32, (R, R), 0)
    col_r = lax.broadcasted_iota(jnp.int32, (R, R), 1)
    tri = jnp.where(((row_r >> 6) == (col_r >> 6)) & (row_r >= col_r), 1.0, 0.0).astype(BF16)
    cum = _cumsum_rows(tri, logw)
    cum_end = jnp.concatenate(
        [jnp.broadcast_to(cum[(c + 1) * L - 1:(c + 1) * L, :], (L, d)) for c in range(n_chunks)], axis=0)
    e_neg = jnp.exp(-cum)
    e_end = jnp.exp(cum_end - cum)
    g_end = jnp.exp(cum_end)
    r_t = r * jnp.exp(cum)
    a_t = -kk * jnp.exp(cum - logw)
    kka = kk * a
    b_t = kka * e_neg
    k_t = k2 * e_neg
    b_e = kka * e_end
    k_e = k2 * e_end

    P = LANES
    chains = [(c, j) for c in range(n_chunks) for j in range(n_pairs)]
    pair = lambda z, c, j: z[c * L:(c + 1) * L, j * P:(j + 1) * P]
    lane = lax.broadcasted_iota(jnp.int32, (L, P), 1)
    first = lane < HEAD_DIM
    stack = lambda z: jnp.concatenate([jnp.where(first, z, 0.0), jnp.where(first, 0.0, z)], axis=0)

    row = lax.broadcasted_iota(jnp.int32, (P, P), 0)
    col = lax.broadcasted_iota(jnp.int32, (P, P), 1)
    same = lambda s: (row >> s) == (col >> s)
    strict = same(6) & (row > col)
    incl = same(6) & (row >= col)
    eye = row == col
    eye_f = eye.astype(F32)

    amat = [_dot_nt(_bf(jnp.concatenate([stack(pair(a_t, c, j)), stack(pair(r_t, c, j))], axis=0)),
                    _bf(jnp.concatenate([pair(b_t, c, j)] * 2 + [pair(k_t, c, j)] * 2, axis=0)))
            for c, j in chains]
    n_st = [jnp.where(strict, m[:P, :P], 0.0) for m in amat]
    a_ak = [_bf(jnp.where(strict, m[:P, P:], 0.0)) for m in amat]
    a_r = [_bf(jnp.concatenate([jnp.where(incl, m[P:, :P], 0.0), jnp.where(incl, m[P:, P:], 0.0)], axis=1))
           for m in amat]

    n0 = [_bf(jnp.where(same(3), n, 0.0)) for n in n_st]
    n2 = [_bf(_dot(n, n)) for n in n0]
    n4 = [_bf(_dot(n, n)) for n in n2]
    t = [eye_f + n.astype(F32) for n in n0]
    t = [x + _dot(_bf(x), m) for x, m in zip(t, n2)]
    t = [x + _dot(_bf(x), m) for x, m in zip(t, n4)]
    for s in (3, 4, 5):
        off_mask = same(s + 1) & jnp.logical_not(same(s))
        n_off = [_bf(jnp.where(off_mask, n, 0.0)) for n in n_st]
        tb = [_bf(x) for x in t]
        half = [_bf(_dot(x, m)) for x, m in zip(tb, n_off)]
        t = [x + _dot(h, xb) for x, h, xb in zip(t, half, tb)]
    tb = [_bf(x) for x in t]

    v_st = [_bf(stack(pair(v, c, j))) for c, j in chains]
    x_st = [_dot(m, vs) for m, vs in zip(a_ak, v_st)]
    wu = [_dot(x, jnp.concatenate([_bf(stack(pair(a_t, c, j))), _bf(xs_)], axis=1))
          for x, xs_, (c, j) in zip(tb, x_st, chains)]
    zeros_b = jnp.zeros((P, P), BF16)
    rhs2 = [jnp.concatenate([_bf(w), jnp.concatenate([zeros_b, vs], axis=1)], axis=0)
            for w, vs in zip(wu, v_st)]
    top = [_dot(m, rh) for m, rh in zip(a_r, rhs2)]
    bot = [_dot_tn(_bf(jnp.concatenate([stack(pair(b_e, c, j)), stack(pair(k_e, c, j))], axis=0)), rh)
           for rh, (c, j) in zip(rhs2, chains)]

    fold = lambda z: z[:L] + z[L:]
    ys = {}
    for idx, (c, j) in enumerate(chains):
        rw = pair(r_t, c, j) + fold(top[idx][:, :P])
        m_mat = jnp.where(eye, pair(g_end, c, j)[0:1, :], 0.0) + bot[idx][:, :P]
        s0 = _bf(state_ref[j])
        ys[(c, j)] = _dot(_bf(rw), s0) + fold(top[idx][:, P:])
        state_ref[j] = _dot(_bf(m_mat), s0) + bot[idx][:, P:]

    y = jnp.concatenate(
        [jnp.concatenate([ys[(c, j)] for j in range(n_pairs)], axis=1) for c in range(n_chunks)], axis=0)
    inv_n = 1.0 / HEAD_DIM
    mean = _headsum(y, ones_bd) * inv_n
    yc = y - mean
    var = _headsum(yc * yc, ones_bd) * inv_n
    y_ref[0] = (yc * lax.rsqrt(var + GN_EPS)) * gng_ref[...] + gnb_ref[...] + bonus


def _cumsum_rows(tri_bf16, x):
    out = None
    for _ in range(3):
        hi = x.astype(BF16)
        term = _dot(tri_bf16, hi)
        out = term if out is None else out + term
        x = x - hi.astype(F32)
    return out


def _rwkv(p_rw, mu, w_up, w0, a_up, a0, k_k, k_a, r_k, gn_g, gn_b, ones_bd, n_chunks):
    b, s, cols = p_rw.shape
    lora, d = w_up.shape
    rows = n_chunks * CHUNK
    row = lambda a: a.reshape(1, -1)
    const = lambda shape: pl.BlockSpec(shape, lambda i, j: (0,) * len(shape))
    return pl.pallas_call(
        functools.partial(_rwkv_body, lora=lora, n_chunks=n_chunks),
        grid=(b, s // rows),
        in_specs=[
            pl.BlockSpec((1, rows, cols), lambda i, j: (i, j, 0)),
            const((1, cols)), const((lora, d)), const((1, d)), const((lora, d)), const((1, d)),
            const((1, d)), const((1, d)), const((1, d)), const((1, d)), const((1, d)), const((LANES, LANES)),
        ],
        out_specs=pl.BlockSpec((1, rows, d), lambda i, j: (i, j, 0)),
        out_shape=jax.ShapeDtypeStruct((b, s, d), F32),
        scratch_shapes=[pltpu.VMEM((d // LANES, LANES, LANES), F32), pltpu.VMEM((1, cols), F32)],
        compiler_params=pltpu.CompilerParams(
            dimension_semantics=("arbitrary", "arbitrary"), vmem_limit_bytes=VMEM_LIMIT),
        name="rwkv7",
    )(p_rw, row(mu), w_up, row(w0), a_up, row(a0), row(k_k), row(k_a), row(r_k), row(gn_g), row(gn_b), ones_bd)


def _split3(x):
    hi = x.astype(BF16)
    r1 = x - hi.astype(F32)
    mid = r1.astype(BF16)
    lo = (r1 - mid.astype(F32)).astype(BF16)
    return hi.astype(F32), mid.astype(F32), lo.astype(F32)


def _score_bound(q_g, k_g):
    amax = lambda g: jnp.max(jnp.abs(g), axis=-1, keepdims=True)
    return (HEAD_DIM ** 0.5) * amax(q_g) * amax(k_g)


def _foxprep_body(qkv_ref, fl_ref, qg_ref, kg_ref, bf_ref, ones_ref, qa_ref, ka_ref, va_ref, carry_ref,
                  *, n_heads):
    @pl.when(pl.program_id(1) == 0)
    def _():
        carry_ref[...] = jnp.zeros_like(carry_ref)

    t = qkv_ref[0]
    tm = t.shape[0]
    d = n_heads * HEAD_DIM
    ones_bd = ones_ref[...]
    inv_n = 1.0 / HEAD_DIM
    log2e = math.log2(math.e)

    def headnorm(z, g):
        ms = _headsum(z * z, ones_bd) * inv_n
        return (z * lax.rsqrt(ms + NORM_EPS)) * g

    q = headnorm(t[:, 0:d], qg_ref[...]) * (HEAD_DIM ** -0.5 * log2e)
    k = headnorm(t[:, d:2 * d], kg_ref[...])
    v = t[:, 2 * d:3 * d]
    neg_bound = -log2e * _score_bound(qg_ref[...], kg_ref[...])

    z = fl_ref[0] + bf_ref[...]
    log_f = jnp.minimum(z, 0.0) - jnp.log1p(jnp.exp(-jnp.abs(z)))
    row = lax.broadcasted_iota(jnp.int32, (tm, tm), 0)
    col = lax.broadcasted_iota(jnp.int32, (tm, tm), 1)
    cs = _cumsum_rows(jnp.where(row >= col, 1.0, 0.0).astype(BF16), log_f) + carry_ref[...]
    carry_ref[...] = cs[tm - 1:tm, :]
    c_hi, c_mid, c_lo = _split3(cs * log2e)

    lane = lax.broadcasted_iota(jnp.int32, (tm, LANES), 1)
    is_data = lane < HEAD_DIM
    in_range = lambda lo, hi: ((lane >= HEAD_DIM + lo) & (lane < HEAD_DIM + hi)).astype(F32)
    q_base = in_range(3, 7)
    k_base = jnp.where(lane == HEAD_DIM + 6, neg_bound, in_range(0, 3))
    one_col = in_range(0, 1)

    def place(pieces, base):
        out = base
        for off, piece in pieces:
            out = jnp.where(lane == off, piece, out)
        return out

    for h in range(n_heads):
        pair = slice((h // 2) * LANES, (h // 2 + 1) * LANES)
        shift = lambda z: z[:, pair] if h % 2 == 0 else pltpu.roll(z[:, pair], HEAD_DIM, axis=1)
        ch = [jnp.broadcast_to(piece[:, h:h + 1], (tm, LANES)) for piece in (c_hi, c_mid, c_lo)]
        q_extra = place([(HEAD_DIM + i, ch[i]) for i in range(3)], q_base)
        k_extra = place([(HEAD_DIM + 3 + i, -ch[i]) for i in range(3)], k_base)
        qa_ref[0, h] = jnp.where(is_data, shift(q), q_extra).astype(BF16)
        ka_ref[0, h] = jnp.where(is_data, shift(k), k_extra).astype(BF16)
        va_ref[0, h] = jnp.where(is_data, shift(v), one_col).astype(BF16)


def _foxprep(p_qkv, fl, q_g, k_g, b_f, ones_bd, tm):
    b, s, cols = p_qkv.shape
    d = cols // 3
    n_heads = d // HEAD_DIM
    const = lambda shape: pl.BlockSpec(shape, lambda i, j: (0,) * len(shape))
    out_spec = pl.BlockSpec((1, n_heads, tm, LANES), lambda i, j: (i, 0, j, 0))
    out_shape = jax.ShapeDtypeStruct((b, n_heads, s, LANES), BF16)
    return pl.pallas_call(
        functools.partial(_foxprep_body, n_heads=n_heads),
        grid=(b, s // tm),
        in_specs=[
            pl.BlockSpec((1, tm, cols), lambda i, j: (i, j, 0)),
            pl.BlockSpec((1, tm, LANES), lambda i, j: (i, j, 0)),
            const((1, d)), const((1, d)), const((1, LANES)), const((LANES, LANES)),
        ],
        out_specs=[out_spec] * 3,
        out_shape=[out_shape] * 3,
        scratch_shapes=[pltpu.VMEM((1, LANES), F32)],
        compiler_params=pltpu.CompilerParams(
            dimension_semantics=("arbitrary", "arbitrary"), vmem_limit_bytes=VMEM_LIMIT),
        name="foxprep",
    )(p_qkv, fl, q_g, k_g, b_f, ones_bd)


def _attn_body(q_ref, k_ref, v_ref, o_ref, *, heads_per_step, tq, online_max):
    qi = pl.program_id(2)
    row = lax.broadcasted_iota(jnp.int32, (tq, tq), 0)
    col = lax.broadcasted_iota(jnp.int32, (tq, tq), 1)
    causal = col <= row
    heads = range(heads_per_step)
    qs = [q_ref[0, hh] for hh in heads]

    def block(j, carry, masked):
        new = []
        for hh in heads:
            kb = k_ref[0, hh, pl.ds(pl.multiple_of(j * tq, tq), tq), :]
            vb = v_ref[0, hh, pl.ds(pl.multiple_of(j * tq, tq), tq), :]
            s = _dot_nt(qs[hh], kb)
            if masked:
                s = jnp.where(causal, s, NEG_BIG)
            if online_max:
                m, acc = carry[hh]
                m_new = jnp.maximum(m, jnp.max(s, axis=-1, keepdims=True))
                acc = acc * jnp.exp2(m - m_new) + _dot(jnp.exp2(s - m_new).astype(BF16), vb)
                new.append((m_new, acc))
            else:
                new.append(carry[hh] + _dot(jnp.exp2(s).astype(BF16), vb))
        return tuple(new)

    zero = jnp.zeros((tq, LANES), F32)
    init = tuple((jnp.full((tq, 1), NEG_BIG, F32), zero) if online_max else zero for _ in heads)
    carry = lax.fori_loop(0, qi, functools.partial(block, masked=False), init)
    carry = block(qi, carry, True)
    accs = [c[1] if online_max else c for c in carry]
    o_ref[0] = jnp.concatenate([acc[:, :HEAD_DIM] / acc[:, HEAD_DIM:HEAD_DIM + 1] for acc in accs], axis=-1)


def _attention(qa, ka, va, tq, online_max):
    b, h, s, _ = qa.shape
    hps = LANES // HEAD_DIM
    return pl.pallas_call(
        functools.partial(_attn_body, heads_per_step=hps, tq=tq, online_max=online_max),
        grid=(b, h // hps, s // tq),
        in_specs=[
            pl.BlockSpec((1, hps, tq, LANES), lambda i, g, j: (i, g, j, 0)),
            pl.BlockSpec((1, hps, s, LANES), lambda i, g, j: (i, g, 0, 0)),
            pl.BlockSpec((1, hps, s, LANES), lambda i, g, j: (i, g, 0, 0)),
        ],
        out_specs=pl.BlockSpec((1, tq, LANES), lambda i, g, j: (i, j, g)),
        out_shape=jax.ShapeDtypeStruct((b, s, h * HEAD_DIM), F32),
        compiler_params=pltpu.CompilerParams(
            dimension_semantics=("parallel", "parallel", "arbitrary"), vmem_limit_bytes=VMEM_LIMIT),
        name="fox_attention_online_max" if online_max else "fox_attention",
    )(qa, ka, va)


def _outproj_body(x_ref, yrw_ref, yfox_ref, gate_ref, w_ref, fg_ref, o_ref, *, final):
    g = gate_ref[...]
    sg = g * _sigmoid(g)
    d_rw = yrw_ref.shape[1]
    y_rw = (yrw_ref[...] * sg[:, :d_rw]).astype(BF16)
    y_fox = (yfox_ref[...] * sg[:, d_rw:]).astype(BF16)
    z = x_ref[...] + _dot(y_rw, w_ref[0:d_rw, :]) + _dot(y_fox, w_ref[d_rw:, :])
    if final:
        z = (z * lax.rsqrt(jnp.mean(z * z, axis=-1, keepdims=True) + NORM_EPS)) * fg_ref[...]
    o_ref[...] = z


def _outproj(x2, y_rw, y_fox, gate, w_out, final_g, final, tm):
    n, d = x2.shape
    d_rw, d_fox = y_rw.shape[1], y_fox.shape[1]
    return pl.pallas_call(
        functools.partial(_outproj_body, final=final),
        grid=(n // tm,),
        in_specs=[
            pl.BlockSpec((tm, d), lambda i: (i, 0)),
            pl.BlockSpec((tm, d_rw), lambda i: (i, 0)),
            pl.BlockSpec((tm, d_fox), lambda i: (i, 0)),
            pl.BlockSpec((tm, d_rw + d_fox), lambda i: (i, 0)),
            pl.BlockSpec(w_out.shape, lambda i: (0, 0)),
            pl.BlockSpec((1, d), lambda i: (0, 0)),
        ],
        out_specs=pl.BlockSpec((tm, d), lambda i: (i, 0)),
        out_shape=jax.ShapeDtypeStruct((n, d), F32),
        compiler_params=pltpu.CompilerParams(
            dimension_semantics=("parallel",), vmem_limit_bytes=VMEM_LIMIT),
        name="outproj",
    )(x2, y_rw, y_fox, gate, w_out, final_g)


def _pick_tile(n, target):
    t = min(n, target)
    while n % t:
        t //= 2
    return t


def kernel(x, norm_g, w_in, rw_mu, rw_w_up, rw_w0, rw_a_up, rw_a0, rw_k_k, rw_k_a, rw_r_k, rw_gn_g, rw_gn_b,
           fox_q_g, fox_k_g, fox_b_f, w_out, final_g):
    b, s, d_model = x.shape
    depth = norm_g.shape[0]
    d_rw = rw_w0.shape[-1]
    lora = rw_w_up.shape[1]
    d_mix = w_out.shape[1]
    d_fox = d_mix - d_rw
    h_fox = fox_b_f.shape[-1]
    rw_cols = 3 * d_rw + 2 * lora
    fox_cols = 3 * d_fox + h_fox
    assert d_rw == d_fox and s % CHUNK == 0 and h_fox <= LANES

    head = lax.broadcasted_iota(jnp.int32, (LANES, LANES), 0) // HEAD_DIM
    ones_bd = (head == head.T).astype(BF16)

    tm_proj = _pick_tile(b * s, 256)
    tm_prep = _pick_tile(s, 256)
    tq = _pick_tile(s, 512)

    x2 = x.reshape(b * s, d_model)
    for l in range(depth):
        w = w_in[l]
        w_fl = jnp.pad(w[:, rw_cols + 3 * d_fox:rw_cols + fox_cols], ((0, 0), (0, LANES - h_fox)))
        w_cat = jnp.concatenate(
            [w[:, :rw_cols], w[:, rw_cols:rw_cols + 3 * d_fox], w[:, rw_cols + fox_cols:], w_fl], axis=1
        ).astype(BF16)
        widths = (rw_cols, 3 * d_fox, d_mix, LANES)
        p_rw, p_qkv, gate, fl = _inproj(x2, norm_g[l].reshape(1, -1), w_cat, widths, tm_proj)

        y_rw = _rwkv(p_rw.reshape(b, s, rw_cols), rw_mu[l], rw_w_up[l], rw_w0[l], rw_a_up[l], rw_a0[l],
                     rw_k_k[l], rw_k_a[l], rw_r_k[l].reshape(-1), rw_gn_g[l], rw_gn_b[l], ones_bd,
                     RWKV_CHUNKS_PER_STEP if s % (RWKV_CHUNKS_PER_STEP * CHUNK) == 0 else 1)

        tile_heads = lambda g: jnp.tile(g, d_fox // HEAD_DIM).reshape(1, -1)
        b_f = jnp.pad(fox_b_f[l], (0, LANES - h_fox)).reshape(1, -1)
        q_g, k_g = tile_heads(fox_q_g[l]), tile_heads(fox_k_g[l])
        qa, ka, va = _foxprep(p_qkv.reshape(b, s, 3 * d_fox), fl.reshape(b, s, LANES), q_g, k_g, b_f, ones_bd,
                              tm_prep)
        y_fox = lax.cond(_score_bound(q_g, k_g)[0, 0] <= SAFE_SCORE_BOUND,
                         functools.partial(_attention, tq=tq, online_max=False),
                         functools.partial(_attention, tq=tq, online_max=True), qa, ka, va)

        x2 = _outproj(x2, y_rw.reshape(b * s, d_rw), y_fox.reshape(b * s, d_fox), gate,
                      w_out[l].astype(BF16), final_g.reshape(1, -1), l == depth - 1, tm_proj)
    return x2.reshape(b, s, d_model)
```

```python
import functools
import math

import jax
import jax.numpy as jnp
from jax import lax
from jax.experimental import pallas as pl
from jax.experimental.pallas import tpu as pltpu

F32 = jnp.float32
BF16 = jnp.bfloat16
HI = lax.Precision.HIGHEST

HEAD_DIM = 64
NORM_EPS = 1e-6
GN_EPS = 64e-5
CHUNK = 64
RWKV_CHUNKS_PER_STEP = 4
LANES = 128
NEG_BIG = -1e30
SAFE_SCORE_BOUND = 40.0
VMEM_LIMIT = 56 * 1024 * 1024


def _dot(a, b, precision=None):
    return jnp.dot(a, b, preferred_element_type=F32, precision=precision)


def _dot_nt(a, b, precision=None):
    return lax.dot_general(a, b, (((1,), (1,)), ((), ())), preferred_element_type=F32, precision=precision)


def _dot_tn(a, b, precision=None):
    return lax.dot_general(a, b, (((0,), (0,)), ((), ())), preferred_element_type=F32, precision=precision)


def _sigmoid(x):
    return 1.0 / (1.0 + jnp.exp(-x))


def _inproj_body(x_ref, g_ref, w_ref, *out_refs, widths):
    x = x_ref[...]
    y = x * lax.rsqrt(jnp.mean(x * x, axis=-1, keepdims=True) + NORM_EPS)
    h = (y * g_ref[...]).astype(BF16)
    off = 0
    for ref, width in zip(out_refs, widths):
        ref[...] = _dot(h, w_ref[:, off:off + width])
        off += width


def _inproj(x2, g, w_cat, widths, tm):
    n, d = x2.shape
    return pl.pallas_call(
        functools.partial(_inproj_body, widths=widths),
        grid=(n // tm,),
        in_specs=[
            pl.BlockSpec((tm, d), lambda i: (i, 0)),
            pl.BlockSpec((1, d), lambda i: (0, 0)),
            pl.BlockSpec(w_cat.shape, lambda i: (0, 0)),
        ],
        out_specs=[pl.BlockSpec((tm, w), lambda i: (i, 0)) for w in widths],
        out_shape=[jax.ShapeDtypeStruct((n, w), F32) for w in widths],
        compiler_params=pltpu.CompilerParams(
            dimension_semantics=("parallel",), vmem_limit_bytes=VMEM_LIMIT),
        name="inproj",
    )(x2, g, w_cat)


def _bf(x):
    return x.astype(BF16)


def _split2(x):
    hi = x.astype(BF16)
    return hi, (x - hi.astype(F32)).astype(BF16)


def _dot_x3(a, b):
    ah, al = _split2(a)
    bh, bl = _split2(b)
    return _dot(ah, bh) + (_dot(ah, bl) + _dot(al, bh))


def _dot_pieces(x, ones_bf16, pieces):
    out = None
    for _ in range(pieces):
        hi = x.astype(BF16)
        term = _dot(hi, ones_bf16)
        out = term if out is None else out + term
        x = x - hi.astype(F32)
    return out


def _headsum(x, ones_bd):
    rows, width = x.shape
    n_pairs = width // LANES
    stacked = jnp.concatenate([x[:, j * LANES:(j + 1) * LANES] for j in range(n_pairs)], axis=0)
    s = _dot_pieces(stacked, ones_bd, 2)
    return jnp.concatenate([s[j * rows:(j + 1) * rows] for j in range(n_pairs)], axis=1)


def _rwkv_body(p_ref, mu_ref, wup_ref, w0_ref, aup_ref, a0_ref, kk_ref, ka_ref, rk_ref,
               gng_ref, gnb_ref, ones_ref, y_ref, state_ref, prev_ref, *, lora, n_chunks):
    @pl.when(pl.program_id(1) == 0)
    def _():
        state_ref[...] = jnp.zeros_like(state_ref)
        prev_ref[...] = jnp.zeros_like(prev_ref)

    p = p_ref[0]
    R = p.shape[0]
    L = CHUNK
    d = y_ref.shape[-1]
    n_pairs = d // LANES
    ones_bd = ones_ref[...]

    row_p = lax.broadcasted_iota(jnp.int32, p.shape, 0)
    prev = jnp.where(row_p == 0, prev_ref[...], pltpu.roll(p, 1, axis=0))
    prev_ref[...] = p[R - 1:R, :]
    xs = p + (prev - p) * mu_ref[...]

    r = xs[:, 0:d]
    k = xs[:, d:2 * d]
    v = xs[:, 2 * d:3 * d]
    wl = xs[:, 3 * d:3 * d + lora]
    al = xs[:, 3 * d + lora:3 * d + 2 * lora]

    w_raw = w0_ref[...] + _dot_x3(jnp.tanh(wl), wup_ref[...])
    logw = (-math.exp(-0.5)) * _sigmoid(w_raw)
    a = _sigmoid(a0_ref[...] + _dot_x3(al, aup_ref[...]))

    kkr = k * kk_ref[...]
    kk = kkr / jnp.maximum(jnp.sqrt(_headsum(kkr * kkr, ones_bd)), 1e-12)
    k2 = k * (1.0 + (a - 1.0) * ka_ref[...])
    bonus = _headsum(r * k2 * rk_ref[...], ones_bd) * v

    row_r = lax.broadcasted_iota(jnp.int32, (R, R), 0)
    col_r = lax.broadcasted_iota(jnp.int32, (R, R), 1)
    tri = jnp.where(((row_r >> 6) == (col_r >> 6)) & (row_r >= col_r), 1.0, 0.0).astype(BF16)
    cum = _cumsum_rows(tri, logw)
    cum_end = jnp.concatenate(
        [jnp.broadcast_to(cum[(c + 1) * L - 1:(c + 1) * L, :], (L, d)) for c in range(n_chunks)], axis=0)
    e_neg = jnp.exp(-cum)
    e_end = jnp.exp(cum_end - cum)
    g_end = jnp.exp(cum_end)
    r_t = r * jnp.exp(cum)
    a_t = -kk * jnp.exp(cum - logw)
    kka = kk * a
    b_t = kka * e_neg
    k_t = k2 * e_neg
    b_e = kka * e_end
    k_e = k2 * e_end

    P = LANES
    chains = [(c, j) for c in range(n_chunks) for j in range(n_pairs)]
    pair = lambda z, c, j: z[c * L:(c + 1) * L, j * P:(j + 1) * P]
    lane = lax.broadcasted_iota(jnp.int32, (L, P), 1)
    first = lane < HEAD_DIM
    stack = lambda z: jnp.concatenate([jnp.where(first, z, 0.0), jnp.where(first, 0.0, z)], axis=0)

    row = lax.broadcasted_iota(jnp.int32, (P, P), 0)
    col = lax.broadcasted_iota(jnp.int32, (P, P), 1)
    same = lambda s: (row >> s) == (col >> s)
    strict = same(6) & (row > col)
    incl = same(6) & (row >= col)
    eye = row == col
    eye_f = eye.astype(F32)

    amat = [_dot_nt(_bf(jnp.concatenate([stack(pair(a_t, c, j)), stack(pair(r_t, c, j))], axis=0)),
                    _bf(jnp.concatenate([pair(b_t, c, j)] * 2 + [pair(k_t, c, j)] * 2, axis=0)))
            for c, j in chains]
    n_st = [jnp.where(strict, m[:P, :P], 0.0) for m in amat]
    a_ak = [_bf(jnp.where(strict, m[:P, P:], 0.0)) for m in amat]
    a_r = [_bf(jnp.concatenate([jnp.where(incl, m[P:, :P], 0.0), jnp.where(incl, m[P:, P:], 0.0)], axis=1))
           for m in amat]

    n0 = [_bf(jnp.where(same(3), n, 0.0)) for n in n_st]
    n2 = [_bf(_dot(n, n)) for n in n0]
    n4 = [_bf(_dot(n, n)) for n in n2]
    t = [eye_f + n.astype(F32) for n in n0]
    t = [x + _dot(_bf(x), m) for x, m in zip(t, n2)]
    t = [x + _dot(_bf(x), m) for x, m in zip(t, n4)]
    for s in (3, 4, 5):
        off_mask = same(s + 1) & jnp.logical_not(same(s))
        n_off = [_bf(jnp.where(off_mask, n, 0.0)) for n in n_st]
        tb = [_bf(x) for x in t]
        half = [_bf(_dot(x, m)) for x, m in zip(tb, n_off)]
        t = [x + _dot(h, xb) for x, h, xb in zip(t, half, tb)]
    tb = [_bf(x) for x in t]

    v_st = [_bf(stack(pair(v, c, j))) for c, j in chains]
    x_st = [_dot(m, vs) for m, vs in zip(a_ak, v_st)]
    wu = [_dot(x, jnp.concatenate([_bf(stack(pair(a_t, c, j))), _bf(xs_)], axis=1))
          for x, xs_, (c, j) in zip(tb, x_st, chains)]
    zeros_b = jnp.zeros((P, P), BF16)
    rhs2 = [jnp.concatenate([_bf(w), jnp.concatenate([zeros_b, vs], axis=1)], axis=0)
            for w, vs in zip(wu, v_st)]
    top = [_dot(m, rh) for m, rh in zip(a_r, rhs2)]
    bot = [_dot_tn(_bf(jnp.concatenate([stack(pair(b_e, c, j)), stack(pair(k_e, c, j))], axis=0)), rh)
           for rh, (c, j) in zip(rhs2, chains)]

    fold = lambda z: z[:L] + z[L:]
    ys = {}
    for idx, (c, j) in enumerate(chains):
        rw = pair(r_t, c, j) + fold(top[idx][:, :P])
        m_mat = jnp.where(eye, pair(g_end, c, j)[0:1, :], 0.0) + bot[idx][:, :P]
        s0 = _bf(state_ref[j])
        ys[(c, j)] = _dot(_bf(rw), s0) + fold(top[idx][:, P:])
        state_ref[j] = _dot(_bf(m_mat), s0) + bot[idx][:, P:]

    y = jnp.concatenate(
        [jnp.concatenate([ys[(c, j)] for j in range(n_pairs)], axis=1) for c in range(n_chunks)], axis=0)
    inv_n = 1.0 / HEAD_DIM
    mean = _headsum(y, ones_bd) * inv_n
    yc = y - mean
    var = _headsum(yc * yc, ones_bd) * inv_n
    y_ref[0] = (yc * lax.rsqrt(var + GN_EPS)) * gng_ref[...] + gnb_ref[...] + bonus


def _cumsum_rows(tri_bf16, x):
    out = None
    for _ in range(3):
        hi = x.astype(BF16)
        term = _dot(tri_bf16, hi)
        out = term if out is None else out + term
        x = x - hi.astype(F32)
    return out


def _rwkv(p_rw, mu, w_up, w0, a_up, a0, k_k, k_a, r_k, gn_g, gn_b, ones_bd, n_chunks):
    b, s, cols = p_rw.shape
    lora, d = w_up.shape
    rows = n_chunks * CHUNK
    row = lambda a: a.reshape(1, -1)
    const = lambda shape: pl.BlockSpec(shape, lambda i, j: (0,) * len(shape))
    return pl.pallas_call(
        functools.partial(_rwkv_body, lora=lora, n_chunks=n_chunks),
        grid=(b, s // rows),
        in_specs=[
            pl.BlockSpec((1, rows, cols), lambda i, j: (i, j, 0)),
            const((1, cols)), const((lora, d)), const((1, d)), const((lora, d)), const((1, d)),
            const((1, d)), const((1, d)), const((1, d)), const((1, d)), const((1, d)), const((LANES, LANES)),
        ],
        out_specs=pl.BlockSpec((1, rows, d), lambda i, j: (i, j, 0)),
        out_shape=jax.ShapeDtypeStruct((b, s, d), F32),
        scratch_shapes=[pltpu.VMEM((d // LANES, LANES, LANES), F32), pltpu.VMEM((1, cols), F32)],
        compiler_params=pltpu.CompilerParams(
            dimension_semantics=("arbitrary", "arbitrary"), vmem_limit_bytes=VMEM_LIMIT),
        name="rwkv7",
    )(p_rw, row(mu), w_up, row(w0), a_up, row(a0), row(k_k), row(k_a), row(r_k), row(gn_g), row(gn_b), ones_bd)


def _split3(x):
    hi = x.astype(BF16)
    r1 = x - hi.astype(F32)
    mid = r1.astype(BF16)
    lo = (r1 - mid.astype(F32)).astype(BF16)
    return hi.astype(F32), mid.astype(F32), lo.astype(F32)


def _score_bound(q_g, k_g):
    amax = lambda g: jnp.max(jnp.abs(g), axis=-1, keepdims=True)
    return (HEAD_DIM ** 0.5) * amax(q_g) * amax(k_g)


def _foxprep_body(qkv_ref, fl_ref, qg_ref, kg_ref, bf_ref, ones_ref, qa_ref, ka_ref, va_ref, carry_ref,
                  *, n_heads):
    @pl.when(pl.program_id(1) == 0)
    def _():
        carry_ref[...] = jnp.zeros_like(carry_ref)

    t = qkv_ref[0]
    tm = t.shape[0]
    d = n_heads * HEAD_DIM
    ones_bd = ones_ref[...]
    inv_n = 1.0 / HEAD_DIM
    log2e = math.log2(math.e)

    def headnorm(z, g):
        ms = _headsum(z * z, ones_bd) * inv_n
        return (z * lax.rsqrt(ms + NORM_EPS)) * g

    q = headnorm(t[:, 0:d], qg_ref[...]) * (HEAD_DIM ** -0.5 * log2e)
    k = headnorm(t[:, d:2 * d], kg_ref[...])
    v = t[:, 2 * d:3 * d]
    neg_bound = -log2e * _score_bound(qg_ref[...], kg_ref[...])

    z = fl_ref[0] + bf_ref[...]
    log_f = jnp.minimum(z, 0.0) - jnp.log1p(jnp.exp(-jnp.abs(z)))
    row = lax.broadcasted_iota(jnp.int32, (tm, tm), 0)
    col = lax.broadcasted_iota(jnp.int32, (tm, tm), 1)
    cs = _cumsum_rows(jnp.where(row >= col, 1.0, 0.0).astype(BF16), log_f) + carry_ref[...]
    carry_ref[...] = cs[tm - 1:tm, :]
    c_hi, c_mid, c_lo = _split3(cs * log2e)

    lane = lax.broadcasted_iota(jnp.int32, (tm, LANES), 1)
    is_data = lane < HEAD_DIM
    in_range = lambda lo, hi: ((lane >= HEAD_DIM + lo) & (lane < HEAD_DIM + hi)).astype(F32)
    q_base = in_range(3, 7)
    k_base = jnp.where(lane == HEAD_DIM + 6, neg_bound, in_range(0, 3))
    one_col = in_range(0, 1)

    def place(pieces, base):
        out = base
        for off, piece in pieces:
            out = jnp.where(lane == off, piece, out)
        return out

    for h in range(n_heads):
        pair = slice((h // 2) * LANES, (h // 2 + 1) * LANES)
        shift = lambda z: z[:, pair] if h % 2 == 0 else pltpu.roll(z[:, pair], HEAD_DIM, axis=1)
        ch = [jnp.broadcast_to(piece[:, h:h + 1], (tm, LANES)) for piece in (c_hi, c_mid, c_lo)]
        q_extra = place([(HEAD_DIM + i, ch[i]) for i in range(3)], q_base)
        k_extra = place([(HEAD_DIM + 3 + i, -ch[i]) for i in range(3)], k_base)
        qa_ref[0, h] = jnp.where(is_data, shift(q), q_extra).astype(BF16)
        ka_ref[0, h] = jnp.where(is_data, shift(k), k_extra).astype(BF16)
        va_ref[0, h] = jnp.where(is_data, shift(v), one_col).astype(BF16)


def _foxprep(p_qkv, fl, q_g, k_g, b_f, ones_bd, tm):
    b, s, cols = p_qkv.shape
    d = cols // 3
    n_heads = d // HEAD_DIM
    const = lambda shape: pl.BlockSpec(shape, lambda i, j: (0,) * len(shape))
    out_spec = pl.BlockSpec((1, n_heads, tm, LANES), lambda i, j: (i, 0, j, 0))
    out_shape = jax.ShapeDtypeStruct((b, n_heads, s, LANES), BF16)
    return pl.pallas_call(
        functools.partial(_foxprep_body, n_heads=n_heads),
        grid=(b, s // tm),
        in_specs=[
            pl.BlockSpec((1, tm, cols), lambda i, j: (i, j, 0)),
            pl.BlockSpec((1, tm, LANES), lambda i, j: (i, j, 0)),
            const((1, d)), const((1, d)), const((1, LANES)), const((LANES, LANES)),
        ],
        out_specs=[out_spec] * 3,
        out_shape=[out_shape] * 3,
        scratch_shapes=[pltpu.VMEM((1, LANES), F32)],
        compiler_params=pltpu.CompilerParams(
            dimension_semantics=("arbitrary", "arbitrary"), vmem_limit_bytes=VMEM_LIMIT),
        name="foxprep",
    )(p_qkv, fl, q_g, k_g, b_f, ones_bd)


def _attn_body(q_ref, k_ref, v_ref, o_ref, *, heads_per_step, tq, online_max):
    qi = pl.program_id(2)
    row = lax.broadcasted_iota(jnp.int32, (tq, tq), 0)
    col = lax.broadcasted_iota(jnp.int32, (tq, tq), 1)
    causal = col <= row
    heads = range(heads_per_step)
    qs = [q_ref[0, hh] for hh in heads]

    def block(j, carry, masked):
        new = []
        for hh in heads:
            kb = k_ref[0, hh, pl.ds(pl.multiple_of(j * tq, tq), tq), :]
            vb = v_ref[0, hh, pl.ds(pl.multiple_of(j * tq, tq), tq), :]
            s = _dot_nt(qs[hh], kb)
            if masked:
                s = jnp.where(causal, s, NEG_BIG)
            if online_max:
                m, acc = carry[hh]
                m_new = jnp.maximum(m, jnp.max(s, axis=-1, keepdims=True))
                acc = acc * jnp.exp2(m - m_new) + _dot(jnp.exp2(s - m_new).astype(BF16), vb)
                new.append((m_new, acc))
            else:
                new.append(carry[hh] + _dot(jnp.exp2(s).astype(BF16), vb))
        return tuple(new)

    zero = jnp.zeros((tq, LANES), F32)
    init = tuple((jnp.full((tq, 1), NEG_BIG, F32), zero) if online_max else zero for _ in heads)
    carry = lax.fori_loop(0, qi, functools.partial(block, masked=False), init)
    carry = block(qi, carry, True)
    accs = [c[1] if online_max else c for c in carry]
    o_ref[0] = jnp.concatenate([acc[:, :HEAD_DIM] / acc[:, HEAD_DIM:HEAD_DIM + 1] for acc in accs], axis=-1)


def _attention(qa, ka, va, tq, online_max):
    b, h, s, _ = qa.shape
    hps = LANES // HEAD_DIM
    return pl.pallas_call(
        functools.partial(_attn_body, heads_per_step=hps, tq=tq, online_max=online_max),
        grid=(b, h // hps, s // tq),
        in_specs=[
            pl.BlockSpec((1, hps, tq, LANES), lambda i, g, j: (i, g, j, 0)),
            pl.BlockSpec((1, hps, s, LANES), lambda i, g, j: (i, g, 0, 0)),
            pl.BlockSpec((1, hps, s, LANES), lambda i, g, j: (i, g, 0, 0)),
        ],
        out_specs=pl.BlockSpec((1, tq, LANES), lambda i, g, j: (i, j, g)),
        out_shape=jax.ShapeDtypeStruct((b, s, h * HEAD_DIM), F32),
        compiler_params=pltpu.CompilerParams(
            dimension_semantics=("parallel", "parallel", "arbitrary"), vmem_limit_bytes=VMEM_LIMIT),
        name="fox_attention_online_max" if online_max else "fox_attention",
    )(qa, ka, va)


def _outproj_body(x_ref, yrw_ref, yfox_ref, gate_ref, w_ref, fg_ref, o_ref, *, final):
    g = gate_ref[...]
    sg = g * _sigmoid(g)
    d_rw = yrw_ref.shape[1]
    y_rw = (yrw_ref[...] * sg[:, :d_rw]).astype(BF16)
    y_fox = (yfox_ref[...] * sg[:, d_rw:]).astype(BF16)
    z = x_ref[...] + _dot(y_rw, w_ref[0:d_rw, :]) + _dot(y_fox, w_ref[d_rw:, :])
    if final:
        z = (z * lax.rsqrt(jnp.mean(z * z, axis=-1, keepdims=True) + NORM_EPS)) * fg_ref[...]
    o_ref[...] = z


def _outproj(x2, y_rw, y_fox, gate, w_out, final_g, final, tm):
    n, d = x2.shape
    d_rw, d_fox = y_rw.shape[1], y_fox.shape[1]
    return pl.pallas_call(
        functools.partial(_outproj_body, final=final),
        grid=(n // tm,),
        in_specs=[
            pl.BlockSpec((tm, d), lambda i: (i, 0)),
            pl.BlockSpec((tm, d_rw), lambda i: (i, 0)),
            pl.BlockSpec((tm, d_fox), lambda i: (i, 0)),
            pl.BlockSpec((tm, d_rw + d_fox), lambda i: (i, 0)),
            pl.BlockSpec(w_out.shape, lambda i: (0, 0)),
            pl.BlockSpec((1, d), lambda i: (0, 0)),
        ],
        out_specs=pl.BlockSpec((tm, d), lambda i: (i, 0)),
        out_shape=jax.ShapeDtypeStruct((n, d), F32),
        compiler_params=pltpu.CompilerParams(
            dimension_semantics=("parallel",), vmem_limit_bytes=VMEM_LIMIT),
        name="outproj",
    )(x2, y_rw, y_fox, gate, w_out, final_g)


def _pick_tile(n, target):
    t = min(n, target)
    while n % t:
        t //= 2
    return t


def kernel(x, norm_g, w_in, rw_mu, rw_w_up, rw_w0, rw_a_up, rw_a0, rw_k_k, rw_k_a, rw_r_k, rw_gn_g, rw_gn_b,
           fox_q_g, fox_k_g, fox_b_f, w_out, final_g):
    b, s, d_model = x.shape
    depth = norm_g.shape[0]
    d_rw = rw_w0.shape[-1]
    lora = rw_w_up.shape[1]
    d_mix = w_out.shape[1]
    d_fox = d_mix - d_rw
    h_fox = fox_b_f.shape[-1]
    rw_cols = 3 * d_rw + 2 * lora
    fox_cols = 3 * d_fox + h_fox
    assert d_rw == d_fox and s % CHUNK == 0 and h_fox <= LANES

    head = lax.broadcasted_iota(jnp.int32, (LANES, LANES), 0) // HEAD_DIM
    ones_bd = (head == head.T).astype(BF16)

    tm_proj = _pick_tile(b * s, 256)
    tm_prep = _pick_tile(s, 256)
    tq = _pick_tile(s, 512)

    x2 = x.reshape(b * s, d_model)
    for l in range(depth):
        w = w_in[l]
        w_fl = jnp.pad(w[:, rw_cols + 3 * d_fox:rw_cols + fox_cols], ((0, 0), (0, LANES - h_fox)))
        w_cat = jnp.concatenate(
            [w[:, :rw_cols], w[:, rw_cols:rw_cols + 3 * d_fox], w[:, rw_cols + fox_cols:], w_fl], axis=1
        ).astype(BF16)
        widths = (rw_cols, 3 * d_fox, d_mix, LANES)
        p_rw, p_qkv, gate, fl = _inproj(x2, norm_g[l].reshape(1, -1), w_cat, widths, tm_proj)

        y_rw = _rwkv(p_rw.reshape(b, s, rw_cols), rw_mu[l], rw_w_up[l], rw_w0[l], rw_a_up[l], rw_a0[l],
                     rw_k_k[l], rw_k_a[l], rw_r_k[l].reshape(-1), rw_gn_g[l], rw_gn_b[l], ones_bd,
                     RWKV_CHUNKS_PER_STEP if s % (RWKV_CHUNKS_PER_STEP * CHUNK) == 0 else 1)

        tile_heads = lambda g: jnp.tile(g, d_fox // HEAD_DIM).reshape(1, -1)
        b_f = jnp.pad(fox_b_f[l], (0, LANES - h_fox)).reshape(1, -1)
        q_g, k_g = tile_heads(fox_q_g[l]), tile_heads(fox_k_g[l])
        qa, ka, va = _foxprep(p_qkv.reshape(b, s, 3 * d_fox), fl.reshape(b, s, LANES), q_g, k_g, b_f, ones_bd,
                              tm_prep)
        y_fox = lax.cond(_score_bound(q_g, k_g)[0, 0] <= SAFE_SCORE_BOUND,
                         functools.partial(_attention, tq=tq, online_max=False),
                         functools.partial(_attention, tq=tq, online_max=True), qa, ka, va)

        x2 = _outproj(x2, y_rw.reshape(b * s, d_rw), y_fox.reshape(b * s, d_fox), gate,
                      w_out[l].astype(BF16), final_g.reshape(1, -1), l == depth - 1, tm_proj)
    return x2.reshape(b, s, d_model)
```

```python
import functools
import math

import jax
import jax.numpy as jnp
from jax import lax
from jax.experimental import pallas as pl
from jax.experimental.pallas import tpu as pltpu

F32 = jnp.float32
BF16 = jnp.bfloat16
HI = lax.Precision.HIGHEST

HEAD_DIM = 64
NORM_EPS = 1e-6
GN_EPS = 64e-5
CHUNK = 64
RWKV_CHUNKS_PER_STEP = 4
LANES = 128
NEG_BIG = -1e30
ATTN_HEADS_PER_STEP = 4
V_ROWS = 80
SAFE_SCORE_BOUND = 40.0
VMEM_LIMIT = 56 * 1024 * 1024


def _dot(a, b, precision=None):
    return jnp.dot(a, b, preferred_element_type=F32, precision=precision)


def _dot_nt(a, b, precision=None):
    return lax.dot_general(a, b, (((1,), (1,)), ((), ())), preferred_element_type=F32, precision=precision)


def _dot_tn(a, b, precision=None):
    return lax.dot_general(a, b, (((0,), (0,)), ((), ())), preferred_element_type=F32, precision=precision)


def _sigmoid(x):
    return 1.0 / (1.0 + jnp.exp(-x))


def _inproj_body(x_ref, g_ref, w_ref, *out_refs, widths):
    x = x_ref[...]
    y = x * lax.rsqrt(jnp.mean(x * x, axis=-1, keepdims=True) + NORM_EPS)
    h = (y * g_ref[...]).astype(BF16)
    off = 0
    for ref, width in zip(out_refs, widths):
        ref[...] = _dot(h, w_ref[:, off:off + width])
        off += width


def _inproj(x2, g, w_cat, widths, tm):
    n, d = x2.shape
    return pl.pallas_call(
        functools.partial(_inproj_body, widths=widths),
        grid=(n // tm,),
        in_specs=[
            pl.BlockSpec((tm, d), lambda i: (i, 0)),
            pl.BlockSpec((1, d), lambda i: (0, 0)),
            pl.BlockSpec(w_cat.shape, lambda i: (0, 0)),
        ],
        out_specs=[pl.BlockSpec((tm, w), lambda i: (i, 0)) for w in widths],
        out_shape=[jax.ShapeDtypeStruct((n, w), F32) for w in widths],
        compiler_params=pltpu.CompilerParams(
            dimension_semantics=("parallel",), vmem_limit_bytes=VMEM_LIMIT),
        name="inproj",
    )(x2, g, w_cat)


def _bf(x):
    return x.astype(BF16)


def _split2(x):
    hi = x.astype(BF16)
    return hi, (x - hi.astype(F32)).astype(BF16)


def _dot_x3(a, b):
    ah, al = _split2(a)
    bh, bl = _split2(b)
    return _dot(ah, bh) + (_dot(ah, bl) + _dot(al, bh))


def _dot_pieces(x, ones_bf16, pieces):
    out = None
    for _ in range(pieces):
        hi = x.astype(BF16)
        term = _dot(hi, ones_bf16)
        out = term if out is None else out + term
        x = x - hi.astype(F32)
    return out


def _headsum(x, ones_bd):
    rows, width = x.shape
    n_pairs = width // LANES
    stacked = jnp.concatenate([x[:, j * LANES:(j + 1) * LANES] for j in range(n_pairs)], axis=0)
    s = _dot_pieces(stacked, ones_bd, 2)
    return jnp.concatenate([s[j * rows:(j + 1) * rows] for j in range(n_pairs)], axis=1)


def _rwkv_body(p_ref, mu_ref, wup_ref, w0_ref, aup_ref, a0_ref, kk_ref, ka_ref, rk_ref,
               gng_ref, gnb_ref, ones_ref, y_ref, state_ref, prev_ref, *, lora, n_chunks):
    @pl.when(pl.program_id(1) == 0)
    def _():
        state_ref[...] = jnp.zeros_like(state_ref)
        prev_ref[...] = jnp.zeros_like(prev_ref)

    p = p_ref[0]
    R = p.shape[0]
    L = CHUNK
    d = y_ref.shape[-1]
    n_pairs = d // LANES
    ones_bd = ones_ref[...]

    row_p = lax.broadcasted_iota(jnp.int32, p.shape, 0)
    prev = jnp.where(row_p == 0, prev_ref[...], pltpu.roll(p, 1, axis=0))
    prev_ref[...] = p[R - 1:R, :]
    xs = p + (prev - p) * mu_ref[...]

    r = xs[:, 0:d]
    k = xs[:, d:2 * d]
    v = xs[:, 2 * d:3 * d]
    wl = xs[:, 3 * d:3 * d + lora]
    al = xs[:, 3 * d + lora:3 * d + 2 * lora]

    w_raw = w0_ref[...] + _dot_x3(jnp.tanh(wl), wup_ref[...])
    logw = (-math.exp(-0.5)) * _sigmoid(w_raw)
    a = _sigmoid(a0_ref[...] + _dot_x3(al, aup_ref[...]))

    kkr = k * kk_ref[...]
    kk = kkr / jnp.maximum(jnp.sqrt(_headsum(kkr * kkr, ones_bd)), 1e-12)
    k2 = k * (1.0 + (a - 1.0) * ka_ref[...])
    bonus = _headsum(r * k2 * rk_ref[...], ones_bd) * v

    row_r = lax.broadcasted_iota(jnp.int32, (R, R), 0)
    col_r = lax.broadcasted_iota(jnp.int32, (R, R), 1)
    tri = jnp.where(((row_r >> 6) == (col_r >> 6)) & (row_r >= col_r), 1.0, 0.0).astype(BF16)
    cum = _cumsum_rows(tri, logw)
    cum_end = jnp.concatenate(
        [jnp.broadcast_to(cum[(c + 1) * L - 1:(c + 1) * L, :], (L, d)) for c in range(n_chunks)], axis=0)
    e_neg = jnp.exp(-cum)
    e_end = jnp.exp(cum_end - cum)
    g_end = jnp.exp(cum_end)
    r_t = r * jnp.exp(cum)
    a_t = -kk * jnp.exp(cum - logw)
    kka = kk * a
    b_t = kka * e_neg
    k_t = k2 * e_neg
    b_e = kka * e_end
    k_e = k2 * e_end

    P = LANES
    chains = [(c, j) for c in range(n_chunks) for j in range(n_pairs)]
    pair = lambda z, c, j: z[c * L:(c + 1) * L, j * P:(j + 1) * P]
    lane = lax.broadcasted_iota(jnp.int32, (L, P), 1)
    first = lane < HEAD_DIM
    stack = lambda z: jnp.concatenate([jnp.where(first, z, 0.0), jnp.where(first, 0.0, z)], axis=0)

    row = lax.broadcasted_iota(jnp.int32, (P, P), 0)
    col = lax.broadcasted_iota(jnp.int32, (P, P), 1)
    same = lambda s: (row >> s) == (col >> s)
    strict = same(6) & (row > col)
    incl = same(6) & (row >= col)
    eye = row == col
    eye_f = eye.astype(F32)

    amat = [_dot_nt(_bf(jnp.concatenate([stack(pair(a_t, c, j)), stack(pair(r_t, c, j))], axis=0)),
                    _bf(jnp.concatenate([pair(b_t, c, j)] * 2 + [pair(k_t, c, j)] * 2, axis=0)))
            for c, j in chains]
    n_st = [jnp.where(strict, m[:P, :P], 0.0) for m in amat]
    a_ak = [_bf(jnp.where(strict, m[:P, P:], 0.0)) for m in amat]
    a_r = [_bf(jnp.concatenate([jnp.where(incl, m[P:, :P], 0.0), jnp.where(incl, m[P:, P:], 0.0)], axis=1))
           for m in amat]

    n0 = [_bf(jnp.where(same(3), n, 0.0)) for n in n_st]
    n2 = [_bf(_dot(n, n)) for n in n0]
    n4 = [_bf(_dot(n, n)) for n in n2]
    t = [eye_f + n.astype(F32) for n in n0]
    t = [x + _dot(_bf(x), m) for x, m in zip(t, n2)]
    t = [x + _dot(_bf(x), m) for x, m in zip(t, n4)]
    for s in (3, 4, 5):
        off_mask = same(s + 1) & jnp.logical_not(same(s))
        n_off = [_bf(jnp.where(off_mask, n, 0.0)) for n in n_st]
        tb = [_bf(x) for x in t]
        half = [_bf(_dot(x, m)) for x, m in zip(tb, n_off)]
        t = [x + _dot(h, xb) for x, h, xb in zip(t, half, tb)]
    tb = [_bf(x) for x in t]

    v_st = [_bf(stack(pair(v, c, j))) for c, j in chains]
    x_st = [_dot(m, vs) for m, vs in zip(a_ak, v_st)]
    wu = [_dot(x, jnp.concatenate([_bf(stack(pair(a_t, c, j))), _bf(xs_)], axis=1))
          for x, xs_, (c, j) in zip(tb, x_st, chains)]
    zeros_b = jnp.zeros((P, P), BF16)
    rhs2 = [jnp.concatenate([_bf(w), jnp.concatenate([zeros_b, vs], axis=1)], axis=0)
            for w, vs in zip(wu, v_st)]
    top = [_dot(m, rh) for m, rh in zip(a_r, rhs2)]
    bot = [_dot_tn(_bf(jnp.concatenate([stack(pair(b_e, c, j)), stack(pair(k_e, c, j))], axis=0)), rh)
           for rh, (c, j) in zip(rhs2, chains)]

    fold = lambda z: z[:L] + z[L:]
    ys = {}
    for idx, (c, j) in enumerate(chains):
        rw = pair(r_t, c, j) + fold(top[idx][:, :P])
        m_mat = jnp.where(eye, pair(g_end, c, j)[0:1, :], 0.0) + bot[idx][:, :P]
        s0 = _bf(state_ref[j])
        ys[(c, j)] = _dot(_bf(rw), s0) + fold(top[idx][:, P:])
        state_ref[j] = _dot(_bf(m_mat), s0) + bot[idx][:, P:]

    y = jnp.concatenate(
        [jnp.concatenate([ys[(c, j)] for j in range(n_pairs)], axis=1) for c in range(n_chunks)], axis=0)
    inv_n = 1.0 / HEAD_DIM
    mean = _headsum(y, ones_bd) * inv_n
    yc = y - mean
    var = _headsum(yc * yc, ones_bd) * inv_n
    y_ref[0] = (yc * lax.rsqrt(var + GN_EPS)) * gng_ref[...] + gnb_ref[...] + bonus


def _cumsum_rows(tri_bf16, x):
    out = None
    for _ in range(3):
        hi = x.astype(BF16)
        term = _dot(tri_bf16, hi)
        out = term if out is None else out + term
        x = x - hi.astype(F32)
    return out


def _rwkv(p_rw, mu, w_up, w0, a_up, a0, k_k, k_a, r_k, gn_g, gn_b, ones_bd, n_chunks):
    b, s, cols = p_rw.shape
    lora, d = w_up.shape
    rows = n_chunks * CHUNK
    row = lambda a: a.reshape(1, -1)
    const = lambda shape: pl.BlockSpec(shape, lambda i, j: (0,) * len(shape))
    return pl.pallas_call(
        functools.partial(_rwkv_body, lora=lora, n_chunks=n_chunks),
        grid=(b, s // rows),
        in_specs=[
            pl.BlockSpec((1, rows, cols), lambda i, j: (i, j, 0)),
            const((1, cols)), const((lora, d)), const((1, d)), const((lora, d)), const((1, d)),
            const((1, d)), const((1, d)), const((1, d)), const((1, d)), const((1, d)), const((LANES, LANES)),
        ],
        out_specs=pl.BlockSpec((1, rows, d), lambda i, j: (i, j, 0)),
        out_shape=jax.ShapeDtypeStruct((b, s, d), F32),
        scratch_shapes=[pltpu.VMEM((d // LANES, LANES, LANES), F32), pltpu.VMEM((1, cols), F32)],
        compiler_params=pltpu.CompilerParams(
            dimension_semantics=("arbitrary", "arbitrary"), vmem_limit_bytes=VMEM_LIMIT),
        name="rwkv7",
    )(p_rw, row(mu), w_up, row(w0), a_up, row(a0), row(k_k), row(k_a), row(r_k), row(gn_g), row(gn_b), ones_bd)


def _split3(x):
    hi = x.astype(BF16)
    r1 = x - hi.astype(F32)
    mid = r1.astype(BF16)
    lo = (r1 - mid.astype(F32)).astype(BF16)
    return hi.astype(F32), mid.astype(F32), lo.astype(F32)


def _score_bound(q_g, k_g):
    amax = lambda g: jnp.max(jnp.abs(g), axis=-1, keepdims=True)
    return (HEAD_DIM ** 0.5) * amax(q_g) * amax(k_g)


def _foxprep_body(qkv_ref, fl_ref, qg_ref, kg_ref, bf_ref, ones_ref, qa_ref, ka_ref, va_ref, carry_ref,
                  *, n_heads):
    @pl.when(pl.program_id(1) == 0)
    def _():
        carry_ref[...] = jnp.zeros_like(carry_ref)

    t = qkv_ref[0]
    tm = t.shape[0]
    d = n_heads * HEAD_DIM
    ones_bd = ones_ref[...]
    inv_n = 1.0 / HEAD_DIM
    log2e = math.log2(math.e)

    def headnorm(z, g):
        ms = _headsum(z * z, ones_bd) * inv_n
        return (z * lax.rsqrt(ms + NORM_EPS)) * g

    q = headnorm(t[:, 0:d], qg_ref[...]) * (HEAD_DIM ** -0.5 * log2e)
    k = headnorm(t[:, d:2 * d], kg_ref[...])
    v = t[:, 2 * d:3 * d]
    neg_bound = -log2e * _score_bound(qg_ref[...], kg_ref[...])

    z = fl_ref[0] + bf_ref[...]
    log_f = jnp.minimum(z, 0.0) - jnp.log1p(jnp.exp(-jnp.abs(z)))
    row = lax.broadcasted_iota(jnp.int32, (tm, tm), 0)
    col = lax.broadcasted_iota(jnp.int32, (tm, tm), 1)
    cs = _cumsum_rows(jnp.where(row >= col, 1.0, 0.0).astype(BF16), log_f) + carry_ref[...]
    carry_ref[...] = cs[tm - 1:tm, :]
    c_hi, c_mid, c_lo = _split3(cs * log2e)

    lane = lax.broadcasted_iota(jnp.int32, (tm, LANES), 1)
    is_data = lane < HEAD_DIM
    in_range = lambda lo, hi: ((lane >= HEAD_DIM + lo) & (lane < HEAD_DIM + hi)).astype(F32)
    q_base = in_range(3, 7)
    k_base = jnp.where(lane == HEAD_DIM + 6, neg_bound, in_range(0, 3))
    one_col = in_range(0, 1)

    def place(pieces, base):
        out = base
        for off, piece in pieces:
            out = jnp.where(lane == off, piece, out)
        return out

    for h in range(n_heads):
        pair = slice((h // 2) * LANES, (h // 2 + 1) * LANES)
        shift = lambda z: z[:, pair] if h % 2 == 0 else pltpu.roll(z[:, pair], HEAD_DIM, axis=1)
        ch = [jnp.broadcast_to(piece[:, h:h + 1], (tm, LANES)) for piece in (c_hi, c_mid, c_lo)]
        q_extra = place([(HEAD_DIM + i, ch[i]) for i in range(3)], q_base)
        k_extra = place([(HEAD_DIM + 3 + i, -ch[i]) for i in range(3)], k_base)
        qa_ref[0, h] = jnp.where(is_data, shift(q), q_extra).astype(BF16)
        ka_ref[0, h] = jnp.where(is_data, shift(k), k_extra).astype(BF16)
        va_ref[0, h, 0] = jnp.where(is_data, shift(v), one_col).T.astype(BF16)


def _foxprep(p_qkv, fl, q_g, k_g, b_f, ones_bd, tm):
    b, s, cols = p_qkv.shape
    d = cols // 3
    n_heads = d // HEAD_DIM
    const = lambda shape: pl.BlockSpec(shape, lambda i, j: (0,) * len(shape))
    out_spec = pl.BlockSpec((1, n_heads, tm, LANES), lambda i, j: (i, 0, j, 0))
    out_shape = jax.ShapeDtypeStruct((b, n_heads, s, LANES), BF16)
    return pl.pallas_call(
        functools.partial(_foxprep_body, n_heads=n_heads),
        grid=(b, s // tm),
        in_specs=[
            pl.BlockSpec((1, tm, cols), lambda i, j: (i, j, 0)),
            pl.BlockSpec((1, tm, LANES), lambda i, j: (i, j, 0)),
            const((1, d)), const((1, d)), const((1, LANES)), const((LANES, LANES)),
        ],
        out_specs=[out_spec, out_spec,
                   pl.BlockSpec((1, n_heads, 1, LANES, tm), lambda i, j: (i, 0, j, 0, 0))],
        out_shape=[out_shape, out_shape, jax.ShapeDtypeStruct((b, n_heads, s // tm, LANES, tm), BF16)],
        scratch_shapes=[pltpu.VMEM((1, LANES), F32)],
        compiler_params=pltpu.CompilerParams(
            dimension_semantics=("arbitrary", "arbitrary"), vmem_limit_bytes=VMEM_LIMIT),
        name="foxprep",
    )(p_qkv, fl, q_g, k_g, b_f, ones_bd)


def _attn_body(q_ref, k_ref, vt_ref, o_ref, *, heads_per_step, tq, online_max):
    qi = pl.program_id(2)
    tk = vt_ref.shape[-1]
    n_sub = tq // tk
    key = lax.broadcasted_iota(jnp.int32, (tk, tq), 0)
    qry = lax.broadcasted_iota(jnp.int32, (tk, tq), 1)
    heads = range(heads_per_step)
    qs = [q_ref[0, hh] for hh in heads]

    def block(jb, carry, diag):
        items = [(u, hh) for u in range(n_sub) for hh in heads]
        sts = []
        for u, hh in items:
            kb = k_ref[0, hh, pl.ds(pl.multiple_of((jb * n_sub + u) * tk, tk), tk), :]
            st = _dot_nt(kb, qs[hh])
            sts.append(jnp.where(key + u * tk <= qry, st, NEG_BIG) if diag else st)
        vts = [vt_ref[0, hh, jb * n_sub + u, 0:V_ROWS, :] for u, hh in items]
        new = list(carry)
        if online_max:
            for (u, hh), st, vt in zip(items, sts, vts):
                m, acc = new[hh]
                m_new = jnp.maximum(m, jnp.max(st, axis=0, keepdims=True))
                new[hh] = (m_new, acc * jnp.exp2(m - m_new) + _dot(vt, jnp.exp2(st - m_new).astype(BF16)))
        else:
            ps = [jnp.exp2(st).astype(BF16) for st in sts]
            for (u, hh), p, vt in zip(items, ps, vts):
                new[hh] = new[hh] + _dot(vt, p)
        return tuple(new)

    zero = jnp.zeros((V_ROWS, tq), F32)
    init = tuple((jnp.full((1, tq), NEG_BIG, F32), zero) if online_max else zero for _ in heads)
    carry = lax.fori_loop(0, qi, functools.partial(block, diag=False), init)
    carry = block(qi, carry, True)
    accs = [c[1] if online_max else c for c in carry]
    out_t = jnp.concatenate([acc[:HEAD_DIM] / acc[HEAD_DIM:HEAD_DIM + 1] for acc in accs], axis=0)
    o_ref[0] = out_t.T


def _attention(qa, ka, vt, tq, online_max):
    b, h, s, _ = qa.shape
    hps = ATTN_HEADS_PER_STEP if h % ATTN_HEADS_PER_STEP == 0 else LANES // HEAD_DIM
    return pl.pallas_call(
        functools.partial(_attn_body, heads_per_step=hps, tq=tq, online_max=online_max),
        grid=(b, h // hps, s // tq),
        in_specs=[
            pl.BlockSpec((1, hps, tq, LANES), lambda i, g, j: (i, g, j, 0)),
            pl.BlockSpec((1, hps, s, LANES), lambda i, g, j: (i, g, 0, 0)),
            pl.BlockSpec((1, hps) + vt.shape[2:], lambda i, g, j: (i, g, 0, 0, 0)),
        ],
        out_specs=pl.BlockSpec((1, tq, hps * HEAD_DIM), lambda i, g, j: (i, j, g)),
        out_shape=jax.ShapeDtypeStruct((b, s, h * HEAD_DIM), F32),
        compiler_params=pltpu.CompilerParams(
            dimension_semantics=("parallel", "parallel", "arbitrary"), vmem_limit_bytes=VMEM_LIMIT),
        name="fox_attention_online_max" if online_max else "fox_attention",
    )(qa, ka, vt)


def _outproj_body(x_ref, yrw_ref, yfox_ref, gate_ref, w_ref, fg_ref, o_ref, *, final):
    g = gate_ref[...]
    sg = g * _sigmoid(g)
    d_rw = yrw_ref.shape[1]
    y_rw = (yrw_ref[...] * sg[:, :d_rw]).astype(BF16)
    y_fox = (yfox_ref[...] * sg[:, d_rw:]).astype(BF16)
    z = x_ref[...] + _dot(y_rw, w_ref[0:d_rw, :]) + _dot(y_fox, w_ref[d_rw:, :])
    if final:
        z = (z * lax.rsqrt(jnp.mean(z * z, axis=-1, keepdims=True) + NORM_EPS)) * fg_ref[...]
    o_ref[...] = z


def _outproj(x2, y_rw, y_fox, gate, w_out, final_g, final, tm):
    n, d = x2.shape
    d_rw, d_fox = y_rw.shape[1], y_fox.shape[1]
    return pl.pallas_call(
        functools.partial(_outproj_body, final=final),
        grid=(n // tm,),
        in_specs=[
            pl.BlockSpec((tm, d), lambda i: (i, 0)),
            pl.BlockSpec((tm, d_rw), lambda i: (i, 0)),
            pl.BlockSpec((tm, d_fox), lambda i: (i, 0)),
            pl.BlockSpec((tm, d_rw + d_fox), lambda i: (i, 0)),
            pl.BlockSpec(w_out.shape, lambda i: (0, 0)),
            pl.BlockSpec((1, d), lambda i: (0, 0)),
        ],
        out_specs=pl.BlockSpec((tm, d), lambda i: (i, 0)),
        out_shape=jax.ShapeDtypeStruct((n, d), F32),
        compiler_params=pltpu.CompilerParams(
            dimension_semantics=("parallel",), vmem_limit_bytes=VMEM_LIMIT),
        name="outproj",
    )(x2, y_rw, y_fox, gate, w_out, final_g)


def _pick_tile(n, target):
    t = min(n, target)
    while n % t:
        t //= 2
    return t


def kernel(x, norm_g, w_in, rw_mu, rw_w_up, rw_w0, rw_a_up, rw_a0, rw_k_k, rw_k_a, rw_r_k, rw_gn_g, rw_gn_b,
           fox_q_g, fox_k_g, fox_b_f, w_out, final_g):
    b, s, d_model = x.shape
    depth = norm_g.shape[0]
    d_rw = rw_w0.shape[-1]
    lora = rw_w_up.shape[1]
    d_mix = w_out.shape[1]
    d_fox = d_mix - d_rw
    h_fox = fox_b_f.shape[-1]
    rw_cols = 3 * d_rw + 2 * lora
    fox_cols = 3 * d_fox + h_fox
    assert d_rw == d_fox and s % CHUNK == 0 and h_fox <= LANES

    head = lax.broadcasted_iota(jnp.int32, (LANES, LANES), 0) // HEAD_DIM
    ones_bd = (head == head.T).astype(BF16)

    tm_proj = _pick_tile(b * s, 256)
    tm_prep = _pick_tile(s, 256)
    tq = _pick_tile(s, 512)

    x2 = x.reshape(b * s, d_model)
    for l in range(depth):
        w = w_in[l]
        w_fl = jnp.pad(w[:, rw_cols + 3 * d_fox:rw_cols + fox_cols], ((0, 0), (0, LANES - h_fox)))
        w_cat = jnp.concatenate(
            [w[:, :rw_cols], w[:, rw_cols:rw_cols + 3 * d_fox], w[:, rw_cols + fox_cols:], w_fl], axis=1
        ).astype(BF16)
        widths = (rw_cols, 3 * d_fox, d_mix, LANES)
        p_rw, p_qkv, gate, fl = _inproj(x2, norm_g[l].reshape(1, -1), w_cat, widths, tm_proj)

        y_rw = _rwkv(p_rw.reshape(b, s, rw_cols), rw_mu[l], rw_w_up[l], rw_w0[l], rw_a_up[l], rw_a0[l],
                     rw_k_k[l], rw_k_a[l], rw_r_k[l].reshape(-1), rw_gn_g[l], rw_gn_b[l], ones_bd,
                     RWKV_CHUNKS_PER_STEP if s % (RWKV_CHUNKS_PER_STEP * CHUNK) == 0 else 1)

        tile_heads = lambda g: jnp.tile(g, d_fox // HEAD_DIM).reshape(1, -1)
        b_f = jnp.pad(fox_b_f[l], (0, LANES - h_fox)).reshape(1, -1)
        q_g, k_g = tile_heads(fox_q_g[l]), tile_heads(fox_k_g[l])
        qa, ka, va = _foxprep(p_qkv.reshape(b, s, 3 * d_fox), fl.reshape(b, s, LANES), q_g, k_g, b_f, ones_bd,
                              tm_prep)
        y_fox = lax.cond(_score_bound(q_g, k_g)[0, 0] <= SAFE_SCORE_BOUND,
                         functools.partial(_attention, tq=tq, online_max=False),
                         functools.partial(_attention, tq=tq, online_max=True), qa, ka, va)

        x2 = _outproj(x2, y_rw.reshape(b * s, d_rw), y_fox.reshape(b * s, d_fox), gate,
                      w_out[l].astype(BF16), final_g.reshape(1, -1), l == depth - 1, tm_proj)
    return x2.reshape(b, s, d_model)
```

```python
import functools
import math

import jax
import jax.numpy as jnp
from jax import lax
from jax.experimental import pallas as pl
from jax.experimental.pallas import tpu as pltpu

F32 = jnp.float32
BF16 = jnp.bfloat16
HI = lax.Precision.HIGHEST

HEAD_DIM = 64
NORM_EPS = 1e-6
GN_EPS = 64e-5
CHUNK = 64
RWKV_CHUNKS_PER_STEP = 4
LANES = 128
MXU_DIM = 256
NEG_BIG = -1e30
ATTN_HEADS_PER_STEP = 4
ATTN_LAG = 2
V_ROWS = 80
SAFE_SCORE_BOUND = 40.0
VMEM_LIMIT = 56 * 1024 * 1024


def _dot(a, b, precision=None):
    return jnp.dot(a, b, preferred_element_type=F32, precision=precision)


def _dot_nt(a, b, precision=None):
    return lax.dot_general(a, b, (((1,), (1,)), ((), ())), preferred_element_type=F32, precision=precision)


def _dot_tn(a, b, precision=None):
    return lax.dot_general(a, b, (((0,), (0,)), ((), ())), preferred_element_type=F32, precision=precision)


def _sigmoid(x):
    return 1.0 / (1.0 + jnp.exp(-x))


def _inproj_body(x_ref, g_ref, w_ref, *out_refs, widths):
    x = x_ref[...]
    y = x * lax.rsqrt(jnp.mean(x * x, axis=-1, keepdims=True) + NORM_EPS)
    h = (y * g_ref[...]).astype(BF16)
    off = 0
    for ref, width in zip(out_refs, widths):
        ref[...] = _dot(h, w_ref[:, off:off + width])
        off += width


def _inproj(x2, g, w_cat, widths, tm):
    n, d = x2.shape
    return pl.pallas_call(
        functools.partial(_inproj_body, widths=widths),
        grid=(n // tm,),
        in_specs=[
            pl.BlockSpec((tm, d), lambda i: (i, 0)),
            pl.BlockSpec((1, d), lambda i: (0, 0)),
            pl.BlockSpec(w_cat.shape, lambda i: (0, 0)),
        ],
        out_specs=[pl.BlockSpec((tm, w), lambda i: (i, 0)) for w in widths],
        out_shape=[jax.ShapeDtypeStruct((n, w), F32) for w in widths],
        compiler_params=pltpu.CompilerParams(
            dimension_semantics=("parallel",), vmem_limit_bytes=VMEM_LIMIT),
        name="inproj",
    )(x2, g, w_cat)


def _bf(x):
    return x.astype(BF16)


def _split2(x):
    hi = x.astype(BF16)
    return hi, (x - hi.astype(F32)).astype(BF16)


def _dot_x3(a, b):
    ah, al = _split2(a)
    bh, bl = _split2(b)
    return _dot(jnp.concatenate([ah, ah, al], axis=1), jnp.concatenate([bh, bl, bh], axis=0))


def _dot_pieces(x, ones_bf16, pieces):
    out = None
    for _ in range(pieces):
        hi = x.astype(BF16)
        term = _dot(hi, ones_bf16)
        out = term if out is None else out + term
        x = x - hi.astype(F32)
    return out


def _headsum(x, ones_bd, pieces=2):
    width = x.shape[1]
    return jnp.concatenate([_dot_pieces(x[:, j:j + MXU_DIM], ones_bd, pieces) for j in range(0, width, MXU_DIM)],
                           axis=1)


def _rwkv_body(p_ref, mu_ref, wup_ref, w0_ref, aup_ref, a0_ref, kk_ref, ka_ref, rk_ref,
               gng_ref, gnb_ref, ones_ref, y_ref, state_ref, prev_ref, *, lora, n_chunks):
    @pl.when(pl.program_id(1) == 0)
    def _():
        state_ref[...] = jnp.zeros_like(state_ref)
        prev_ref[...] = jnp.zeros_like(prev_ref)

    p = p_ref[0]
    R = p.shape[0]
    L = CHUNK
    d = y_ref.shape[-1]
    n_pairs = d // LANES
    ones_bd = ones_ref[...]

    row_p = lax.broadcasted_iota(jnp.int32, p.shape, 0)
    prev = jnp.where(row_p == 0, prev_ref[...], pltpu.roll(p, 1, axis=0))
    prev_ref[...] = p[R - 1:R, :]
    xs = p + (prev - p) * mu_ref[...]

    r = xs[:, 0:d]
    k = xs[:, d:2 * d]
    v = xs[:, 2 * d:3 * d]
    wl = xs[:, 3 * d:3 * d + lora]
    al = xs[:, 3 * d + lora:3 * d + 2 * lora]

    w_raw = w0_ref[...] + _dot_x3(jnp.tanh(wl), wup_ref[...])
    logw = (-math.exp(-0.5)) * _sigmoid(w_raw)
    a = _sigmoid(a0_ref[...] + _dot_x3(al, aup_ref[...]))

    kkr = k * kk_ref[...]
    kk = kkr / jnp.maximum(jnp.sqrt(_headsum(kkr * kkr, ones_bd, pieces=1)), 1e-12)
    k2 = k * (1.0 + (a - 1.0) * ka_ref[...])
    bonus = _headsum(r * k2 * rk_ref[...], ones_bd, pieces=1) * v

    row_r = lax.broadcasted_iota(jnp.int32, (R, R), 0)
    col_r = lax.broadcasted_iota(jnp.int32, (R, R), 1)
    tri = jnp.where(((row_r >> 6) == (col_r >> 6)) & (row_r >= col_r), 1.0, 0.0).astype(BF16)
    cum = _cumsum_rows(tri, logw)
    cum_end = jnp.concatenate(
        [jnp.broadcast_to(cum[(c + 1) * L - 1:(c + 1) * L, :], (L, d)) for c in range(n_chunks)], axis=0)
    e_neg = jnp.exp(-cum)
    e_end = jnp.exp(cum_end - cum)
    g_end = jnp.exp(cum_end)
    r_t = r * jnp.exp(cum)
    a_t = -kk * jnp.exp(cum - logw)
    kka = kk * a
    b_t = kka * e_neg
    k_t = k2 * e_neg
    b_e = kka * e_end
    k_e = k2 * e_end

    P = LANES
    chains = [(c, j) for c in range(n_chunks) for j in range(n_pairs)]
    pair = lambda z, c, j: z[c * L:(c + 1) * L, j * P:(j + 1) * P]
    lane = lax.broadcasted_iota(jnp.int32, (L, P), 1)
    first = lane < HEAD_DIM
    stack = lambda z: jnp.concatenate([jnp.where(first, z, 0.0), jnp.where(first, 0.0, z)], axis=0)

    row = lax.broadcasted_iota(jnp.int32, (P, P), 0)
    col = lax.broadcasted_iota(jnp.int32, (P, P), 1)
    same = lambda s: (row >> s) == (col >> s)
    strict = same(6) & (row > col)
    incl = same(6) & (row >= col)
    eye = row == col
    eye_f = eye.astype(F32)

    amat = [_dot_nt(_bf(jnp.concatenate([stack(pair(a_t, c, j)), stack(pair(r_t, c, j))], axis=0)),
                    _bf(jnp.concatenate([pair(b_t, c, j)] * 2 + [pair(k_t, c, j)] * 2, axis=0)))
            for c, j in chains]
    n_st = [jnp.where(strict, m[:P, :P], 0.0) for m in amat]
    a_ak = [_bf(jnp.where(strict, m[:P, P:], 0.0)) for m in amat]
    a_r = [_bf(jnp.concatenate([jnp.where(incl, m[P:, :P], 0.0), jnp.where(incl, m[P:, P:], 0.0)], axis=1))
           for m in amat]

    n0 = [_bf(jnp.where(same(3), n, 0.0)) for n in n_st]
    n2 = [_bf(_dot(n, n)) for n in n0]
    n4 = [_bf(_dot(n, n)) for n in n2]
    t = [eye_f + n.astype(F32) for n in n0]
    t = [x + _dot(_bf(x), m) for x, m in zip(t, n2)]
    t = [x + _dot(_bf(x), m) for x, m in zip(t, n4)]
    for s in (3, 4, 5):
        off_mask = same(s + 1) & jnp.logical_not(same(s))
        n_off = [_bf(jnp.where(off_mask, n, 0.0)) for n in n_st]
        tb = [_bf(x) for x in t]
        half = [_bf(_dot(x, m)) for x, m in zip(tb, n_off)]
        t = [x + _dot(h, xb) for x, h, xb in zip(t, half, tb)]
    tb = [_bf(x) for x in t]

    v_st = [_bf(stack(pair(v, c, j))) for c, j in chains]
    x_st = [_dot(m, vs) for m, vs in zip(a_ak, v_st)]
    wu = [_dot(x, jnp.concatenate([_bf(stack(pair(a_t, c, j))), _bf(xs_)], axis=1))
          for x, xs_, (c, j) in zip(tb, x_st, chains)]
    zeros_b = jnp.zeros((P, P), BF16)
    rhs2 = [jnp.concatenate([_bf(w), jnp.concatenate([zeros_b, vs], axis=1)], axis=0)
            for w, vs in zip(wu, v_st)]
    top = [_dot(m, rh) for m, rh in zip(a_r, rhs2)]
    bot = [_dot_tn(_bf(jnp.concatenate([stack(pair(b_e, c, j)), stack(pair(k_e, c, j))], axis=0)), rh)
           for rh, (c, j) in zip(rhs2, chains)]

    fold = lambda z: z[:L] + z[L:]
    ys = {}
    for idx, (c, j) in enumerate(chains):
        rw = pair(r_t, c, j) + fold(top[idx][:, :P])
        m_mat = jnp.where(eye, pair(g_end, c, j)[0:1, :], 0.0) + bot[idx][:, :P]
        s0 = _bf(state_ref[j])
        ys[(c, j)] = _dot(_bf(rw), s0) + fold(top[idx][:, P:])
        state_ref[j] = _dot(_bf(m_mat), s0) + bot[idx][:, P:]

    y = jnp.concatenate(
        [jnp.concatenate([ys[(c, j)] for j in range(n_pairs)], axis=1) for c in range(n_chunks)], axis=0)
    inv_n = 1.0 / HEAD_DIM
    mean = _headsum(y, ones_bd) * inv_n
    yc = y - mean
    var = _headsum(yc * yc, ones_bd) * inv_n
    y_ref[0] = (yc * lax.rsqrt(var + GN_EPS)) * gng_ref[...] + gnb_ref[...] + bonus


def _cumsum_rows(tri_bf16, x):
    out = None
    for _ in range(3):
        hi = x.astype(BF16)
        term = _dot(tri_bf16, hi)
        out = term if out is None else out + term
        x = x - hi.astype(F32)
    return out


def _rwkv(p_rw, mu, w_up, w0, a_up, a0, k_k, k_a, r_k, gn_g, gn_b, ones_bd, n_chunks):
    b, s, cols = p_rw.shape
    lora, d = w_up.shape
    rows = n_chunks * CHUNK
    row = lambda a: a.reshape(1, -1)
    const = lambda shape: pl.BlockSpec(shape, lambda i, j: (0,) * len(shape))
    return pl.pallas_call(
        functools.partial(_rwkv_body, lora=lora, n_chunks=n_chunks),
        grid=(b, s // rows),
        in_specs=[
            pl.BlockSpec((1, rows, cols), lambda i, j: (i, j, 0)),
            const((1, cols)), const((lora, d)), const((1, d)), const((lora, d)), const((1, d)),
            const((1, d)), const((1, d)), const((1, d)), const((1, d)), const((1, d)), const((MXU_DIM, MXU_DIM)),
        ],
        out_specs=pl.BlockSpec((1, rows, d), lambda i, j: (i, j, 0)),
        out_shape=jax.ShapeDtypeStruct((b, s, d), F32),
        scratch_shapes=[pltpu.VMEM((d // LANES, LANES, LANES), F32), pltpu.VMEM((1, cols), F32)],
        compiler_params=pltpu.CompilerParams(
            dimension_semantics=("arbitrary", "arbitrary"), vmem_limit_bytes=VMEM_LIMIT),
        name="rwkv7",
    )(p_rw, row(mu), w_up, row(w0), a_up, row(a0), row(k_k), row(k_a), row(r_k), row(gn_g), row(gn_b), ones_bd)


def _split3(x):
    hi = x.astype(BF16)
    r1 = x - hi.astype(F32)
    mid = r1.astype(BF16)
    lo = (r1 - mid.astype(F32)).astype(BF16)
    return hi.astype(F32), mid.astype(F32), lo.astype(F32)


def _score_bound(q_g, k_g):
    amax = lambda g: jnp.max(jnp.abs(g), axis=-1, keepdims=True)
    return (HEAD_DIM ** 0.5) * amax(q_g) * amax(k_g)


def _foxprep_body(qkv_ref, fl_ref, qg_ref, kg_ref, bf_ref, ones_ref, qa_ref, ka_ref, va_ref, carry_ref,
                  *, n_heads):
    @pl.when(pl.program_id(1) == 0)
    def _():
        carry_ref[...] = jnp.zeros_like(carry_ref)

    t = qkv_ref[0]
    tm = t.shape[0]
    d = n_heads * HEAD_DIM
    ones_bd = ones_ref[...]
    inv_n = 1.0 / HEAD_DIM
    log2e = math.log2(math.e)

    def headnorm(z, g):
        ms = _headsum(z * z, ones_bd) * inv_n
        return (z * lax.rsqrt(ms + NORM_EPS)) * g

    q = headnorm(t[:, 0:d], qg_ref[...]) * (HEAD_DIM ** -0.5 * log2e)
    k = headnorm(t[:, d:2 * d], kg_ref[...])
    v = t[:, 2 * d:3 * d]
    neg_bound = -log2e * _score_bound(qg_ref[...], kg_ref[...])

    z = fl_ref[0] + bf_ref[...]
    log_f = jnp.minimum(z, 0.0) - jnp.log1p(jnp.exp(-jnp.abs(z)))
    row = lax.broadcasted_iota(jnp.int32, (tm, tm), 0)
    col = lax.broadcasted_iota(jnp.int32, (tm, tm), 1)
    cs = _cumsum_rows(jnp.where(row >= col, 1.0, 0.0).astype(BF16), log_f) + carry_ref[...]
    carry_ref[...] = cs[tm - 1:tm, :]
    c_hi, c_mid, c_lo = _split3(cs * log2e)

    lane = lax.broadcasted_iota(jnp.int32, (tm, LANES), 1)
    is_data = lane < HEAD_DIM
    in_range = lambda lo, hi: ((lane >= HEAD_DIM + lo) & (lane < HEAD_DIM + hi)).astype(F32)
    q_base = in_range(3, 7)
    k_base = jnp.where(lane == HEAD_DIM + 6, neg_bound, in_range(0, 3))
    one_col = in_range(0, 1)

    def place(pieces, base):
        out = base
        for off, piece in pieces:
            out = jnp.where(lane == off, piece, out)
        return out

    for h in range(n_heads):
        pair = slice((h // 2) * LANES, (h // 2 + 1) * LANES)
        shift = lambda z: z[:, pair] if h % 2 == 0 else pltpu.roll(z[:, pair], HEAD_DIM, axis=1)
        ch = [jnp.broadcast_to(piece[:, h:h + 1], (tm, LANES)) for piece in (c_hi, c_mid, c_lo)]
        q_extra = place([(HEAD_DIM + i, ch[i]) for i in range(3)], q_base)
        k_extra = place([(HEAD_DIM + 3 + i, -ch[i]) for i in range(3)], k_base)
        qa_ref[0, h] = jnp.where(is_data, shift(q), q_extra).astype(BF16)
        ka_ref[0, h] = jnp.where(is_data, shift(k), k_extra).astype(BF16)
        va_ref[0, h, 0] = jnp.where(is_data, shift(v), one_col).T.astype(BF16)


def _foxprep(p_qkv, fl, q_g, k_g, b_f, ones_bd, tm):
    b, s, cols = p_qkv.shape
    d = cols // 3
    n_heads = d // HEAD_DIM
    const = lambda shape: pl.BlockSpec(shape, lambda i, j: (0,) * len(shape))
    out_spec = pl.BlockSpec((1, n_heads, tm, LANES), lambda i, j: (i, 0, j, 0))
    out_shape = jax.ShapeDtypeStruct((b, n_heads, s, LANES), BF16)
    return pl.pallas_call(
        functools.partial(_foxprep_body, n_heads=n_heads),
        grid=(b, s // tm),
        in_specs=[
            pl.BlockSpec((1, tm, cols), lambda i, j: (i, j, 0)),
            pl.BlockSpec((1, tm, LANES), lambda i, j: (i, j, 0)),
            const((1, d)), const((1, d)), const((1, LANES)), const((MXU_DIM, MXU_DIM)),
        ],
        out_specs=[out_spec, out_spec,
                   pl.BlockSpec((1, n_heads, 1, LANES, tm), lambda i, j: (i, 0, j, 0, 0))],
        out_shape=[out_shape, out_shape, jax.ShapeDtypeStruct((b, n_heads, s // tm, LANES, tm), BF16)],
        scratch_shapes=[pltpu.VMEM((1, LANES), F32)],
        compiler_params=pltpu.CompilerParams(
            dimension_semantics=("arbitrary", "arbitrary"), vmem_limit_bytes=VMEM_LIMIT),
        name="foxprep",
    )(p_qkv, fl, q_g, k_g, b_f, ones_bd)


def _attn_body(q_ref, k_ref, vt_ref, o_ref, acc_ref, pend_ref, *, heads_per_step, tq, online_max):
    qi = pl.program_id(2)
    tk = vt_ref.shape[-1]
    n_sub = tq // tk
    key = lax.broadcasted_iota(jnp.int32, (tk, tq), 0)
    qry = lax.broadcasted_iota(jnp.int32, (tk, tq), 1)
    heads = range(heads_per_step)
    qs = [q_ref[0, hh] for hh in heads]

    items = [(u, hh) for u in range(n_sub) for hh in heads]
    n_items = len(items)
    zero = jnp.zeros((V_ROWS, tq), F32)

    def scores(jb, u, hh, diag):
        kb = k_ref[0, hh, pl.ds(pl.multiple_of((jb * n_sub + u) * tk, tk), tk), :]
        st = _dot_nt(kb, qs[hh])
        return jnp.where(key + u * tk <= qry, st, NEG_BIG) if diag else st

    def values(sub_block, hh):
        return vt_ref[0, hh, sub_block, 0:V_ROWS, :]

    if online_max:
        def block(jb, carry, diag):
            new = list(carry)
            for u, hh in items:
                st = scores(jb, u, hh, diag)
                m, acc = new[hh]
                m_new = jnp.maximum(m, jnp.max(st, axis=0, keepdims=True))
                p = jnp.exp2(st - m_new).astype(BF16)
                new[hh] = (m_new, acc * jnp.exp2(m - m_new) + _dot(values(jb * n_sub + u, hh), p))
            return tuple(new)

        init = tuple((jnp.full((1, tq), NEG_BIG, F32), zero) for _ in heads)
        carry = lax.fori_loop(0, qi, functools.partial(block, diag=False), init)
        accs = [c[1] for c in block(qi, carry, True)]
    else:
        def consume(st, sub_block, hh):
            acc_ref[hh] = acc_ref[hh] + _dot(values(sub_block, hh), jnp.exp2(st).astype(BF16))

        def block(jb, carry, diag):
            tiles = []
            for c, (u, hh) in enumerate(items):
                tiles.append(scores(jb, u, hh, diag))
                if c < ATTN_LAG:
                    pu, ph = items[n_items - ATTN_LAG + c]
                    consume(pend_ref[c], jnp.maximum((jb - 1) * n_sub + pu, 0), ph)
                else:
                    pu, ph = items[c - ATTN_LAG]
                    consume(tiles[c - ATTN_LAG], jb * n_sub + pu, ph)
            for c in range(ATTN_LAG):
                pend_ref[c] = tiles[n_items - ATTN_LAG + c]
            return carry

        acc_ref[...] = jnp.zeros_like(acc_ref)
        pend_ref[...] = jnp.full(pend_ref.shape, NEG_BIG, F32)
        lax.fori_loop(0, qi, functools.partial(block, diag=False), 0)
        block(qi, 0, True)
        for c in range(ATTN_LAG):
            pu, ph = items[n_items - ATTN_LAG + c]
            consume(pend_ref[c], qi * n_sub + pu, ph)
        accs = [acc_ref[hh] for hh in heads]
    out_t = jnp.concatenate([acc[:HEAD_DIM] / acc[HEAD_DIM:HEAD_DIM + 1] for acc in accs], axis=0)
    o_ref[0] = out_t.T


def _attention(qa, ka, vt, tq, online_max):
    b, h, s, _ = qa.shape
    hps = ATTN_HEADS_PER_STEP if h % ATTN_HEADS_PER_STEP == 0 else LANES // HEAD_DIM
    return pl.pallas_call(
        functools.partial(_attn_body, heads_per_step=hps, tq=tq, online_max=online_max),
        grid=(b, h // hps, s // tq),
        in_specs=[
            pl.BlockSpec((1, hps, tq, LANES), lambda i, g, j: (i, g, j, 0)),
            pl.BlockSpec((1, hps, s, LANES), lambda i, g, j: (i, g, 0, 0)),
            pl.BlockSpec((1, hps) + vt.shape[2:], lambda i, g, j: (i, g, 0, 0, 0)),
        ],
        out_specs=pl.BlockSpec((1, tq, hps * HEAD_DIM), lambda i, g, j: (i, j, g)),
        out_shape=jax.ShapeDtypeStruct((b, s, h * HEAD_DIM), F32),
        scratch_shapes=[pltpu.VMEM((hps, V_ROWS, tq), F32), pltpu.VMEM((ATTN_LAG, vt.shape[-1], tq), F32)],
        compiler_params=pltpu.CompilerParams(
            dimension_semantics=("parallel", "parallel", "arbitrary"), vmem_limit_bytes=VMEM_LIMIT),
        name="fox_attention_online_max" if online_max else "fox_attention",
    )(qa, ka, vt)


def _outproj_body(x_ref, yrw_ref, yfox_ref, gate_ref, w_ref, fg_ref, o_ref, *, final):
    g = gate_ref[...]
    sg = g * _sigmoid(g)
    d_rw = yrw_ref.shape[1]
    y_rw = (yrw_ref[...] * sg[:, :d_rw]).astype(BF16)
    y_fox = (yfox_ref[...] * sg[:, d_rw:]).astype(BF16)
    z = x_ref[...] + _dot(y_rw, w_ref[0:d_rw, :]) + _dot(y_fox, w_ref[d_rw:, :])
    if final:
        z = (z * lax.rsqrt(jnp.mean(z * z, axis=-1, keepdims=True) + NORM_EPS)) * fg_ref[...]
    o_ref[...] = z


def _outproj(x2, y_rw, y_fox, gate, w_out, final_g, final, tm):
    n, d = x2.shape
    d_rw, d_fox = y_rw.shape[1], y_fox.shape[1]
    return pl.pallas_call(
        functools.partial(_outproj_body, final=final),
        grid=(n // tm,),
        in_specs=[
            pl.BlockSpec((tm, d), lambda i: (i, 0)),
            pl.BlockSpec((tm, d_rw), lambda i: (i, 0)),
            pl.BlockSpec((tm, d_fox), lambda i: (i, 0)),
            pl.BlockSpec((tm, d_rw + d_fox), lambda i: (i, 0)),
            pl.BlockSpec(w_out.shape, lambda i: (0, 0)),
            pl.BlockSpec((1, d), lambda i: (0, 0)),
        ],
        out_specs=pl.BlockSpec((tm, d), lambda i: (i, 0)),
        out_shape=jax.ShapeDtypeStruct((n, d), F32),
        compiler_params=pltpu.CompilerParams(
            dimension_semantics=("parallel",), vmem_limit_bytes=VMEM_LIMIT),
        name="outproj",
    )(x2, y_rw, y_fox, gate, w_out, final_g)


def _pick_tile(n, target):
    t = min(n, target)
    while n % t:
        t //= 2
    return t


def kernel(x, norm_g, w_in, rw_mu, rw_w_up, rw_w0, rw_a_up, rw_a0, rw_k_k, rw_k_a, rw_r_k, rw_gn_g, rw_gn_b,
           fox_q_g, fox_k_g, fox_b_f, w_out, final_g):
    b, s, d_model = x.shape
    depth = norm_g.shape[0]
    d_rw = rw_w0.shape[-1]
    lora = rw_w_up.shape[1]
    d_mix = w_out.shape[1]
    d_fox = d_mix - d_rw
    h_fox = fox_b_f.shape[-1]
    rw_cols = 3 * d_rw + 2 * lora
    fox_cols = 3 * d_fox + h_fox
    assert d_rw == d_fox and s % CHUNK == 0 and h_fox <= LANES

    head = lax.broadcasted_iota(jnp.int32, (MXU_DIM, MXU_DIM), 0) // HEAD_DIM
    ones_bd = (head == head.T).astype(BF16)

    tm_proj = _pick_tile(b * s, 256)
    tm_prep = _pick_tile(s, 256)
    tq = _pick_tile(s, 512)

    x2 = x.reshape(b * s, d_model)
    for l in range(depth):
        w = w_in[l]
        w_fl = jnp.pad(w[:, rw_cols + 3 * d_fox:rw_cols + fox_cols], ((0, 0), (0, LANES - h_fox)))
        w_cat = jnp.concatenate(
            [w[:, :rw_cols], w[:, rw_cols:rw_cols + 3 * d_fox], w[:, rw_cols + fox_cols:], w_fl], axis=1
        ).astype(BF16)
        widths = (rw_cols, 3 * d_fox, d_mix, LANES)
        p_rw, p_qkv, gate, fl = _inproj(x2, norm_g[l].reshape(1, -1), w_cat, widths, tm_proj)

        y_rw = _rwkv(p_rw.reshape(b, s, rw_cols), rw_mu[l], rw_w_up[l], rw_w0[l], rw_a_up[l], rw_a0[l],
                     rw_k_k[l], rw_k_a[l], rw_r_k[l].reshape(-1), rw_gn_g[l], rw_gn_b[l], ones_bd,
                     RWKV_CHUNKS_PER_STEP if s % (RWKV_CHUNKS_PER_STEP * CHUNK) == 0 else 1)

        tile_heads = lambda g: jnp.tile(g, d_fox // HEAD_DIM).reshape(1, -1)
        b_f = jnp.pad(fox_b_f[l], (0, LANES - h_fox)).reshape(1, -1)
        q_g, k_g = tile_heads(fox_q_g[l]), tile_heads(fox_k_g[l])
        qa, ka, va = _foxprep(p_qkv.reshape(b, s, 3 * d_fox), fl.reshape(b, s, LANES), q_g, k_g, b_f, ones_bd,
                              tm_prep)
        y_fox = lax.cond(_score_bound(q_g, k_g)[0, 0] <= SAFE_SCORE_BOUND,
                         functools.partial(_attention, tq=tq, online_max=False),
                         functools.partial(_attention, tq=tq, online_max=True), qa, ka, va)

        x2 = _outproj(x2, y_rw.reshape(b * s, d_rw), y_fox.reshape(b * s, d_fox), gate,
                      w_out[l].astype(BF16), final_g.reshape(1, -1), l == depth - 1, tm_proj)
    return x2.reshape(b, s, d_model)
```

```python
import functools
import math

import jax
import jax.numpy as jnp
from jax import lax
from jax.experimental import pallas as pl
from jax.experimental.pallas import tpu as pltpu

F32 = jnp.float32
BF16 = jnp.bfloat16
HI = lax.Precision.HIGHEST

HEAD_DIM = 64
NORM_EPS = 1e-6
GN_EPS = 64e-5
CHUNK = 64
RWKV_CHUNKS_PER_STEP = 4
LANES = 128
MXU_DIM = 256
NEG_BIG = -1e30
ATTN_HEADS_PER_STEP = 4
V_ROWS = 80
SAFE_SCORE_BOUND = 40.0
VMEM_LIMIT = 56 * 1024 * 1024


def _dot(a, b, precision=None):
    return jnp.dot(a, b, preferred_element_type=F32, precision=precision)


def _dot_nt(a, b, precision=None):
    return lax.dot_general(a, b, (((1,), (1,)), ((), ())), preferred_element_type=F32, precision=precision)


def _dot_tn(a, b, precision=None):
    return lax.dot_general(a, b, (((0,), (0,)), ((), ())), preferred_element_type=F32, precision=precision)


def _sigmoid(x):
    return 1.0 / (1.0 + jnp.exp(-x))


def _bf(x):
    return x.astype(BF16)


def _split2(x):
    hi = x.astype(BF16)
    return hi, (x - hi.astype(F32)).astype(BF16)


def _dot_x3(a, b):
    ah, al = _split2(a)
    bh, bl = _split2(b)
    return _dot(jnp.concatenate([ah, ah, al], axis=1), jnp.concatenate([bh, bl, bh], axis=0))


def _dot_pieces(x, ones_bf16, pieces):
    out = None
    for _ in range(pieces):
        hi = x.astype(BF16)
        term = _dot(hi, ones_bf16)
        out = term if out is None else out + term
        x = x - hi.astype(F32)
    return out


def _headsum(x, ones_bd, pieces=2):
    width = x.shape[1]
    return jnp.concatenate([_dot_pieces(x[:, j:j + MXU_DIM], ones_bd, pieces) for j in range(0, width, MXU_DIM)],
                           axis=1)


def _rwkv_body(p_ref, mu_ref, wup_ref, w0_ref, aup_ref, a0_ref, kk_ref, ka_ref, rk_ref,
               gng_ref, gnb_ref, ones_ref, y_ref, state_ref, prev_ref, *, lora, n_chunks):
    @pl.when(pl.program_id(1) == 0)
    def _():
        state_ref[...] = jnp.zeros_like(state_ref)
        prev_ref[...] = jnp.zeros_like(prev_ref)

    p = p_ref[0]
    R = p.shape[0]
    L = CHUNK
    d = y_ref.shape[-1]
    n_pairs = d // LANES
    ones_bd = ones_ref[...]

    row_p = lax.broadcasted_iota(jnp.int32, p.shape, 0)
    prev = jnp.where(row_p == 0, prev_ref[...], pltpu.roll(p, 1, axis=0))
    prev_ref[...] = p[R - 1:R, :]
    xs = p + (prev - p) * mu_ref[...]

    r = xs[:, 0:d]
    k = xs[:, d:2 * d]
    v = xs[:, 2 * d:3 * d]
    wl = xs[:, 3 * d:3 * d + lora]
    al = xs[:, 3 * d + lora:3 * d + 2 * lora]

    w_raw = w0_ref[...] + _dot_x3(jnp.tanh(wl), wup_ref[...])
    logw = (-math.exp(-0.5)) * _sigmoid(w_raw)
    a = _sigmoid(a0_ref[...] + _dot_x3(al, aup_ref[...]))

    kkr = k * kk_ref[...]
    kk = kkr / jnp.maximum(jnp.sqrt(_headsum(kkr * kkr, ones_bd, pieces=1)), 1e-12)
    k2 = k * (1.0 + (a - 1.0) * ka_ref[...])
    bonus = _headsum(r * k2 * rk_ref[...], ones_bd, pieces=1) * v

    row_r = lax.broadcasted_iota(jnp.int32, (R, R), 0)
    col_r = lax.broadcasted_iota(jnp.int32, (R, R), 1)
    tri = jnp.where(((row_r >> 6) == (col_r >> 6)) & (row_r >= col_r), 1.0, 0.0).astype(BF16)
    cum = _cumsum_rows(tri, logw)
    cum_end = jnp.concatenate(
        [jnp.broadcast_to(cum[(c + 1) * L - 1:(c + 1) * L, :], (L, d)) for c in range(n_chunks)], axis=0)
    e_neg = jnp.exp(-cum)
    e_end = jnp.exp(cum_end - cum)
    g_end = jnp.exp(cum_end)
    r_t = r * jnp.exp(cum)
    a_t = -kk * jnp.exp(cum - logw)
    kka = kk * a
    b_t = kka * e_neg
    k_t = k2 * e_neg
    b_e = kka * e_end
    k_e = k2 * e_end

    P = LANES
    chains = [(c, j) for c in range(n_chunks) for j in range(n_pairs)]
    pair = lambda z, c, j: z[c * L:(c + 1) * L, j * P:(j + 1) * P]
    lane = lax.broadcasted_iota(jnp.int32, (L, P), 1)
    first = lane < HEAD_DIM
    stack = lambda z: jnp.concatenate([jnp.where(first, z, 0.0), jnp.where(first, 0.0, z)], axis=0)

    row = lax.broadcasted_iota(jnp.int32, (P, P), 0)
    col = lax.broadcasted_iota(jnp.int32, (P, P), 1)
    same = lambda s: (row >> s) == (col >> s)
    strict = same(6) & (row > col)
    incl = same(6) & (row >= col)
    eye = row == col
    eye_f = eye.astype(F32)

    amat = [_dot_nt(_bf(jnp.concatenate([stack(pair(a_t, c, j)), stack(pair(r_t, c, j))], axis=0)),
                    _bf(jnp.concatenate([pair(b_t, c, j)] * 2 + [pair(k_t, c, j)] * 2, axis=0)))
            for c, j in chains]
    n_st = [jnp.where(strict, m[:P, :P], 0.0) for m in amat]
    a_ak = [_bf(jnp.where(strict, m[:P, P:], 0.0)) for m in amat]
    a_r = [_bf(jnp.concatenate([jnp.where(incl, m[P:, :P], 0.0), jnp.where(incl, m[P:, P:], 0.0)], axis=1))
           for m in amat]

    n0 = [_bf(jnp.where(same(3), n, 0.0)) for n in n_st]
    n2 = [_bf(_dot(n, n)) for n in n0]
    n4 = [_bf(_dot(n, n)) for n in n2]
    t = [eye_f + n.astype(F32) for n in n0]
    t = [x + _dot(_bf(x), m) for x, m in zip(t, n2)]
    t = [x + _dot(_bf(x), m) for x, m in zip(t, n4)]
    for s in (3, 4, 5):
        off_mask = same(s + 1) & jnp.logical_not(same(s))
        n_off = [_bf(jnp.where(off_mask, n, 0.0)) for n in n_st]
        tb = [_bf(x) for x in t]
        half = [_bf(_dot(x, m)) for x, m in zip(tb, n_off)]
        t = [x + _dot(h, xb) for x, h, xb in zip(t, half, tb)]
    tb = [_bf(x) for x in t]

    v_st = [_bf(stack(pair(v, c, j))) for c, j in chains]
    x_st = [_dot(m, vs) for m, vs in zip(a_ak, v_st)]
    wu = [_dot(x, jnp.concatenate([_bf(stack(pair(a_t, c, j))), _bf(xs_)], axis=1))
          for x, xs_, (c, j) in zip(tb, x_st, chains)]
    zeros_b = jnp.zeros((P, P), BF16)
    rhs2 = [jnp.concatenate([_bf(w), jnp.concatenate([zeros_b, vs], axis=1)], axis=0)
            for w, vs in zip(wu, v_st)]
    top = [_dot(m, rh) for m, rh in zip(a_r, rhs2)]
    bot = [_dot_tn(_bf(jnp.concatenate([stack(pair(b_e, c, j)), stack(pair(k_e, c, j))], axis=0)), rh)
           for rh, (c, j) in zip(rhs2, chains)]

    fold = lambda z: z[:L] + z[L:]
    ys = {}
    for idx, (c, j) in enumerate(chains):
        rw = pair(r_t, c, j) + fold(top[idx][:, :P])
        m_mat = jnp.where(eye, pair(g_end, c, j)[0:1, :], 0.0) + bot[idx][:, :P]
        s0 = _bf(state_ref[j])
        ys[(c, j)] = _dot(_bf(rw), s0) + fold(top[idx][:, P:])
        state_ref[j] = _dot(_bf(m_mat), s0) + bot[idx][:, P:]

    y = jnp.concatenate(
        [jnp.concatenate([ys[(c, j)] for j in range(n_pairs)], axis=1) for c in range(n_chunks)], axis=0)
    inv_n = 1.0 / HEAD_DIM
    mean = _headsum(y, ones_bd) * inv_n
    yc = y - mean
    var = _headsum(yc * yc, ones_bd) * inv_n
    y_ref[0] = (yc * lax.rsqrt(var + GN_EPS)) * gng_ref[...] + gnb_ref[...] + bonus


def _cumsum_rows(tri_bf16, x):
    out = None
    for _ in range(3):
        hi = x.astype(BF16)
        term = _dot(tri_bf16, hi)
        out = term if out is None else out + term
        x = x - hi.astype(F32)
    return out


def _rwkv(p_rw, mu, w_up, w0, a_up, a0, k_k, k_a, r_k, gn_g, gn_b, ones_bd, n_chunks):
    b, s, cols = p_rw.shape
    lora, d = w_up.shape
    rows = n_chunks * CHUNK
    row = lambda a: a.reshape(1, -1)
    const = lambda shape: pl.BlockSpec(shape, lambda i, j: (0,) * len(shape))
    return pl.pallas_call(
        functools.partial(_rwkv_body, lora=lora, n_chunks=n_chunks),
        grid=(b, s // rows),
        in_specs=[
            pl.BlockSpec((1, rows, cols), lambda i, j: (i, j, 0)),
            const((1, cols)), const((lora, d)), const((1, d)), const((lora, d)), const((1, d)),
            const((1, d)), const((1, d)), const((1, d)), const((1, d)), const((1, d)), const((MXU_DIM, MXU_DIM)),
        ],
        out_specs=pl.BlockSpec((1, rows, d), lambda i, j: (i, j, 0)),
        out_shape=jax.ShapeDtypeStruct((b, s, d), F32),
        scratch_shapes=[pltpu.VMEM((d // LANES, LANES, LANES), F32), pltpu.VMEM((1, cols), F32)],
        compiler_params=pltpu.CompilerParams(
            dimension_semantics=("arbitrary", "arbitrary"), vmem_limit_bytes=VMEM_LIMIT),
        name="rwkv7",
    )(p_rw, row(mu), w_up, row(w0), a_up, row(a0), row(k_k), row(k_a), row(r_k), row(gn_g), row(gn_b), ones_bd)


def _split3(x):
    hi = x.astype(BF16)
    r1 = x - hi.astype(F32)
    mid = r1.astype(BF16)
    lo = (r1 - mid.astype(F32)).astype(BF16)
    return hi.astype(F32), mid.astype(F32), lo.astype(F32)


def _score_bound(q_g, k_g):
    amax = lambda g: jnp.max(jnp.abs(g), axis=-1, keepdims=True)
    return (HEAD_DIM ** 0.5) * amax(q_g) * amax(k_g)


def _inproj_fox_body(x_ref, g_ref, w_ref, qg_ref, kg_ref, bf_ref, ones_ref,
                     rw_ref, gate_ref, qa_ref, ka_ref, va_ref, carry_ref, *, n_heads):
    @pl.when(pl.program_id(1) == 0)
    def _():
        carry_ref[...] = jnp.zeros_like(carry_ref)

    x = x_ref[0]
    tm = x.shape[0]
    d = n_heads * HEAD_DIM
    rw_cols, gate_cols = rw_ref.shape[-1], gate_ref.shape[-1]
    off_qkv, off_gate, off_fl = rw_cols, rw_cols + 3 * d, rw_cols + 3 * d + gate_cols
    y = x * lax.rsqrt(jnp.mean(x * x, axis=-1, keepdims=True) + NORM_EPS)
    h = (y * g_ref[...]).astype(BF16)

    t = _dot(h, w_ref[:, off_qkv:off_qkv + 3 * d])
    fl = _dot(h, w_ref[:, off_fl:off_fl + LANES])

    tiles = [(rw_ref, c0, c0) for c0 in range(0, rw_cols, MXU_DIM)]
    tiles += [(gate_ref, c0, off_gate + c0) for c0 in range(0, gate_cols, MXU_DIM)]

    def project(tile):
        ref, dst, src = tile
        width = min(MXU_DIM, ref.shape[-1] - dst)
        ref[0, :, dst:dst + width] = _dot(h, w_ref[:, src:src + width])

    n_lead = max(len(tiles) - n_heads, 0)
    for tile in tiles[:n_lead]:
        project(tile)

    ones_bd = ones_ref[...]
    inv_n = 1.0 / HEAD_DIM
    log2e = math.log2(math.e)

    def headnorm(z, g):
        ms = _headsum(z * z, ones_bd) * inv_n
        return (z * lax.rsqrt(ms + NORM_EPS)) * g

    q = headnorm(t[:, 0:d], qg_ref[...]) * (HEAD_DIM ** -0.5 * log2e)
    k = headnorm(t[:, d:2 * d], kg_ref[...])
    v = t[:, 2 * d:3 * d]
    neg_bound = -log2e * _score_bound(qg_ref[...], kg_ref[...])

    z = fl + bf_ref[...]
    log_f = jnp.minimum(z, 0.0) - jnp.log1p(jnp.exp(-jnp.abs(z)))
    row = lax.broadcasted_iota(jnp.int32, (tm, tm), 0)
    col = lax.broadcasted_iota(jnp.int32, (tm, tm), 1)
    cs = _cumsum_rows(jnp.where(row >= col, 1.0, 0.0).astype(BF16), log_f) + carry_ref[...]
    carry_ref[...] = cs[tm - 1:tm, :]
    c_hi, c_mid, c_lo = _split3(cs * log2e)

    lane = lax.broadcasted_iota(jnp.int32, (tm, LANES), 1)
    is_data = lane < HEAD_DIM
    in_range = lambda lo, hi: ((lane >= HEAD_DIM + lo) & (lane < HEAD_DIM + hi)).astype(F32)
    q_base = in_range(3, 7)
    k_base = jnp.where(lane == HEAD_DIM + 6, neg_bound, in_range(0, 3))
    one_col = in_range(0, 1)

    def place(pieces, base):
        out = base
        for off, piece in pieces:
            out = jnp.where(lane == off, piece, out)
        return out

    for h_idx in range(n_heads):
        if n_lead + h_idx < len(tiles):
            project(tiles[n_lead + h_idx])
        pair = slice((h_idx // 2) * LANES, (h_idx // 2 + 1) * LANES)
        shift = lambda z: z[:, pair] if h_idx % 2 == 0 else pltpu.roll(z[:, pair], HEAD_DIM, axis=1)
        ch = [jnp.broadcast_to(piece[:, h_idx:h_idx + 1], (tm, LANES)) for piece in (c_hi, c_mid, c_lo)]
        q_extra = place([(HEAD_DIM + i, ch[i]) for i in range(3)], q_base)
        k_extra = place([(HEAD_DIM + 3 + i, -ch[i]) for i in range(3)], k_base)
        qa_ref[0, h_idx] = jnp.where(is_data, shift(q), q_extra).astype(BF16)
        ka_ref[0, h_idx] = jnp.where(is_data, shift(k), k_extra).astype(BF16)
        va_ref[0, h_idx, 0] = jnp.where(is_data, shift(v), one_col).T.astype(BF16)
    for tile in tiles[n_lead + n_heads:]:
        project(tile)


def _inproj_fox(x, g, w_cat, q_g, k_g, b_f, ones_bd, rw_cols, gate_cols, tm):
    b, s, d_model = x.shape
    d = q_g.shape[-1]
    n_heads = d // HEAD_DIM
    const = lambda shape: pl.BlockSpec(shape, lambda i, j: (0,) * len(shape))
    rows = lambda width: pl.BlockSpec((1, tm, width), lambda i, j: (i, j, 0))
    head_spec = pl.BlockSpec((1, n_heads, tm, LANES), lambda i, j: (i, 0, j, 0))
    head_shape = jax.ShapeDtypeStruct((b, n_heads, s, LANES), BF16)
    return pl.pallas_call(
        functools.partial(_inproj_fox_body, n_heads=n_heads),
        grid=(b, s // tm),
        in_specs=[
            rows(d_model), const((1, d_model)), const(w_cat.shape),
            const((1, d)), const((1, d)), const((1, LANES)), const((MXU_DIM, MXU_DIM)),
        ],
        out_specs=[rows(rw_cols), rows(gate_cols), head_spec, head_spec,
                   pl.BlockSpec((1, n_heads, 1, LANES, tm), lambda i, j: (i, 0, j, 0, 0))],
        out_shape=[jax.ShapeDtypeStruct((b, s, rw_cols), F32), jax.ShapeDtypeStruct((b, s, gate_cols), F32),
                   head_shape, head_shape, jax.ShapeDtypeStruct((b, n_heads, s // tm, LANES, tm), BF16)],
        scratch_shapes=[pltpu.VMEM((1, LANES), F32)],
        compiler_params=pltpu.CompilerParams(
            dimension_semantics=("arbitrary", "arbitrary"), vmem_limit_bytes=VMEM_LIMIT),
        name="inproj_fox",
    )(x, g, w_cat, q_g, k_g, b_f, ones_bd)


def _attn_body(q_ref, k_ref, vt_ref, o_ref, *, heads_per_step, tq, online_max):
    qi = pl.program_id(2)
    tk = vt_ref.shape[-1]
    n_sub = tq // tk
    key = lax.broadcasted_iota(jnp.int32, (tk, tq), 0)
    qry = lax.broadcasted_iota(jnp.int32, (tk, tq), 1)
    heads = range(heads_per_step)
    qs = [q_ref[0, hh] for hh in heads]

    items = [(u, hh) for u in range(n_sub) for hh in heads]
    n_items = len(items)
    zero = jnp.zeros((V_ROWS, tq), F32)

    def scores(jb, u, hh, diag):
        kb = k_ref[0, hh, pl.ds(pl.multiple_of((jb * n_sub + u) * tk, tk), tk), :]
        st = _dot_nt(kb, qs[hh])
        return jnp.where(key + u * tk <= qry, st, NEG_BIG) if diag else st

    def values(sub_block, hh):
        return vt_ref[0, hh, sub_block, 0:V_ROWS, :]

    if online_max:
        def block(jb, carry, diag):
            new = list(carry)
            for u, hh in items:
                st = scores(jb, u, hh, diag)
                m, acc = new[hh]
                m_new = jnp.maximum(m, jnp.max(st, axis=0, keepdims=True))
                p = jnp.exp2(st - m_new).astype(BF16)
                new[hh] = (m_new, acc * jnp.exp2(m - m_new) + _dot(values(jb * n_sub + u, hh), p))
            return tuple(new)

        init = tuple((jnp.full((1, tq), NEG_BIG, F32), zero) for _ in heads)
        carry = lax.fori_loop(0, qi, functools.partial(block, diag=False), init)
        accs = [c[1] for c in block(qi, carry, True)]
    else:
        def block(jb, carry, diag):
            tiles = [scores(jb, u, hh, diag) for u, hh in items]
            probs = [jnp.exp2(st).astype(BF16) for st in tiles]
            new = list(carry)
            for (u, hh), p in zip(items, probs):
                new[hh] = new[hh] + _dot(values(jb * n_sub + u, hh), p)
            return tuple(new)

        carry = lax.fori_loop(0, qi, functools.partial(block, diag=False), tuple(zero for _ in heads))
        accs = block(qi, carry, True)
    out_t = jnp.concatenate([acc[:HEAD_DIM] / acc[HEAD_DIM:HEAD_DIM + 1] for acc in accs], axis=0)
    o_ref[0] = out_t.T


def _attention(qa, ka, vt, tq, online_max):
    b, h, s, _ = qa.shape
    hps = ATTN_HEADS_PER_STEP if h % ATTN_HEADS_PER_STEP == 0 else LANES // HEAD_DIM
    return pl.pallas_call(
        functools.partial(_attn_body, heads_per_step=hps, tq=tq, online_max=online_max),
        grid=(b, h // hps, s // tq),
        in_specs=[
            pl.BlockSpec((1, hps, tq, LANES), lambda i, g, j: (i, g, j, 0)),
            pl.BlockSpec((1, hps, s, LANES), lambda i, g, j: (i, g, 0, 0)),
            pl.BlockSpec((1, hps) + vt.shape[2:], lambda i, g, j: (i, g, 0, 0, 0)),
        ],
        out_specs=pl.BlockSpec((1, tq, hps * HEAD_DIM), lambda i, g, j: (i, j, g)),
        out_shape=jax.ShapeDtypeStruct((b, s, h * HEAD_DIM), F32),
        compiler_params=pltpu.CompilerParams(
            dimension_semantics=("parallel", "parallel", "arbitrary"), vmem_limit_bytes=VMEM_LIMIT),
        name="fox_attention_online_max" if online_max else "fox_attention",
    )(qa, ka, vt)


def _outproj_body(x_ref, yrw_ref, yfox_ref, gate_ref, w_ref, fg_ref, o_ref, *, final):
    g = gate_ref[...]
    sg = g * _sigmoid(g)
    d_rw = yrw_ref.shape[1]
    y_rw = (yrw_ref[...] * sg[:, :d_rw]).astype(BF16)
    y_fox = (yfox_ref[...] * sg[:, d_rw:]).astype(BF16)
    z = x_ref[...] + _dot(y_rw, w_ref[0:d_rw, :]) + _dot(y_fox, w_ref[d_rw:, :])
    if final:
        z = (z * lax.rsqrt(jnp.mean(z * z, axis=-1, keepdims=True) + NORM_EPS)) * fg_ref[...]
    o_ref[...] = z


def _outproj(x2, y_rw, y_fox, gate, w_out, final_g, final, tm):
    n, d = x2.shape
    d_rw, d_fox = y_rw.shape[1], y_fox.shape[1]
    return pl.pallas_call(
        functools.partial(_outproj_body, final=final),
        grid=(n // tm,),
        in_specs=[
            pl.BlockSpec((tm, d), lambda i: (i, 0)),
            pl.BlockSpec((tm, d_rw), lambda i: (i, 0)),
            pl.BlockSpec((tm, d_fox), lambda i: (i, 0)),
            pl.BlockSpec((tm, d_rw + d_fox), lambda i: (i, 0)),
            pl.BlockSpec(w_out.shape, lambda i: (0, 0)),
            pl.BlockSpec((1, d), lambda i: (0, 0)),
        ],
        out_specs=pl.BlockSpec((tm, d), lambda i: (i, 0)),
        out_shape=jax.ShapeDtypeStruct((n, d), F32),
        compiler_params=pltpu.CompilerParams(
            dimension_semantics=("parallel",), vmem_limit_bytes=VMEM_LIMIT),
        name="outproj",
    )(x2, y_rw, y_fox, gate, w_out, final_g)


def _pick_tile(n, target):
    t = min(n, target)
    while n % t:
        t //= 2
    return t


def kernel(x, norm_g, w_in, rw_mu, rw_w_up, rw_w0, rw_a_up, rw_a0, rw_k_k, rw_k_a, rw_r_k, rw_gn_g, rw_gn_b,
           fox_q_g, fox_k_g, fox_b_f, w_out, final_g):
    b, s, d_model = x.shape
    depth = norm_g.shape[0]
    d_rw = rw_w0.shape[-1]
    lora = rw_w_up.shape[1]
    d_mix = w_out.shape[1]
    d_fox = d_mix - d_rw
    h_fox = fox_b_f.shape[-1]
    rw_cols = 3 * d_rw + 2 * lora
    fox_cols = 3 * d_fox + h_fox
    assert d_rw == d_fox and s % CHUNK == 0 and h_fox <= LANES

    head = lax.broadcasted_iota(jnp.int32, (MXU_DIM, MXU_DIM), 0) // HEAD_DIM
    ones_bd = (head == head.T).astype(BF16)

    tm_proj = _pick_tile(b * s, 256)
    tm_prep = _pick_tile(s, 256)
    tq = _pick_tile(s, 512)

    x2 = x.reshape(b * s, d_model)
    for l in range(depth):
        w = w_in[l]
        w_fl = jnp.pad(w[:, rw_cols + 3 * d_fox:rw_cols + fox_cols], ((0, 0), (0, LANES - h_fox)))
        w_cat = jnp.concatenate(
            [w[:, :rw_cols], w[:, rw_cols:rw_cols + 3 * d_fox], w[:, rw_cols + fox_cols:], w_fl], axis=1
        ).astype(BF16)
        tile_heads = lambda g: jnp.tile(g, d_fox // HEAD_DIM).reshape(1, -1)
        b_f = jnp.pad(fox_b_f[l], (0, LANES - h_fox)).reshape(1, -1)
        q_g, k_g = tile_heads(fox_q_g[l]), tile_heads(fox_k_g[l])
        p_rw, gate, qa, ka, va = _inproj_fox(x2.reshape(b, s, d_model), norm_g[l].reshape(1, -1), w_cat,
                                             q_g, k_g, b_f, ones_bd, rw_cols, d_mix, tm_prep)
        gate = gate.reshape(b * s, d_mix)

        y_rw = _rwkv(p_rw, rw_mu[l], rw_w_up[l], rw_w0[l], rw_a_up[l], rw_a0[l],
                     rw_k_k[l], rw_k_a[l], rw_r_k[l].reshape(-1), rw_gn_g[l], rw_gn_b[l], ones_bd,
                     RWKV_CHUNKS_PER_STEP if s % (RWKV_CHUNKS_PER_STEP * CHUNK) == 0 else 1)

        y_fox = lax.cond(_score_bound(q_g, k_g)[0, 0] <= SAFE_SCORE_BOUND,
                         functools.partial(_attention, tq=tq, online_max=False),
                         functools.partial(_attention, tq=tq, online_max=True), qa, ka, va)

        x2 = _outproj(x2, y_rw.reshape(b * s, d_rw), y_fox.reshape(b * s, d_fox), gate,
                      w_out[l].astype(BF16), final_g.reshape(1, -1), l == depth - 1, tm_proj)
    return x2.reshape(b, s, d_model)
```

```python
import functools
import math

import jax
import jax.numpy as jnp
from jax import lax
from jax.experimental import pallas as pl
from jax.experimental.pallas import tpu as pltpu

F32 = jnp.float32
BF16 = jnp.bfloat16

HEAD_DIM = 64
NORM_EPS = 1e-6
GN_EPS = 64e-5
CHUNK = 64
RWKV_CHUNKS_PER_STEP = 4
LANES = 128
MXU_DIM = 256
NEG_BIG = -1e30
ATTN_HEADS_PER_STEP = 4
ATTN_BLOCKS_PER_ITER = 2
V_ROWS = 80
SAFE_SCORE_BOUND = 40.0
VMEM_LIMIT = 56 * 1024 * 1024


def _dot(a, b):
    return jnp.dot(a, b, preferred_element_type=F32)


def _dot_nt(a, b):
    return lax.dot_general(a, b, (((1,), (1,)), ((), ())), preferred_element_type=F32)


def _dot_tn(a, b):
    return lax.dot_general(a, b, (((0,), (0,)), ((), ())), preferred_element_type=F32)


def _sigmoid(x):
    return 1.0 / (1.0 + jnp.exp(-x))


def _bf(x):
    return x.astype(BF16)


def _split2(x):
    hi = x.astype(BF16)
    return hi, (x - hi.astype(F32)).astype(BF16)


def _split3(x):
    hi = x.astype(BF16)
    r1 = x - hi.astype(F32)
    mid = r1.astype(BF16)
    lo = (r1 - mid.astype(F32)).astype(BF16)
    return hi.astype(F32), mid.astype(F32), lo.astype(F32)


def _dot_x3(a, b):
    ah, al = _split2(a)
    bh, bl = _split2(b)
    return _dot(jnp.concatenate([ah, ah, al], axis=1), jnp.concatenate([bh, bl, bh], axis=0))


def _dot_pieces(x, ones_bf16, pieces):
    out = None
    for _ in range(pieces):
        hi = x.astype(BF16)
        term = _dot(hi, ones_bf16)
        out = term if out is None else out + term
        x = x - hi.astype(F32)
    return out


def _cumsum_rows(tri_bf16, x):
    out = None
    for _ in range(3):
        hi = x.astype(BF16)
        term = _dot(tri_bf16, hi)
        out = term if out is None else out + term
        x = x - hi.astype(F32)
    return out


def _headsum(x, ones_bd, pieces=2):
    width = x.shape[1]
    return jnp.concatenate([_dot_pieces(x[:, j:j + MXU_DIM], ones_bd, pieces) for j in range(0, width, MXU_DIM)],
                           axis=1)


def _score_bound(q_g, k_g):
    amax = lambda g: jnp.max(jnp.abs(g), axis=-1, keepdims=True)
    return (HEAD_DIM ** 0.5) * amax(q_g) * amax(k_g)


def _inproj_fox_body(x_ref, g_ref, w_ref, qg_ref, kg_ref, bf_ref, ones_ref,
                     rw_ref, gate_ref, qa_ref, ka_ref, va_ref, carry_ref, *, n_heads):
    @pl.when(pl.program_id(1) == 0)
    def _():
        carry_ref[...] = jnp.zeros_like(carry_ref)

    x = x_ref[0]
    tm = x.shape[0]
    d = n_heads * HEAD_DIM
    rw_cols, gate_cols = rw_ref.shape[-1], gate_ref.shape[-1]
    off_qkv, off_gate, off_fl = rw_cols, rw_cols + 3 * d, rw_cols + 3 * d + gate_cols
    y = x * lax.rsqrt(jnp.mean(x * x, axis=-1, keepdims=True) + NORM_EPS)
    h = (y * g_ref[...]).astype(BF16)

    t = _dot(h, w_ref[:, off_qkv:off_qkv + 3 * d])
    fl = _dot(h, w_ref[:, off_fl:off_fl + LANES])

    tiles = [(rw_ref, c0, c0) for c0 in range(0, rw_cols, MXU_DIM)]
    tiles += [(gate_ref, c0, off_gate + c0) for c0 in range(0, gate_cols, MXU_DIM)]

    def project(tile):
        ref, dst, src = tile
        width = min(MXU_DIM, ref.shape[-1] - dst)
        ref[0, :, dst:dst + width] = _dot(h, w_ref[:, src:src + width]).astype(ref.dtype)

    n_lead = max(len(tiles) - n_heads, 0)
    for tile in tiles[:n_lead]:
        project(tile)

    ones_bd = ones_ref[...]
    inv_n = 1.0 / HEAD_DIM
    log2e = math.log2(math.e)

    def headnorm(z, g):
        ms = _headsum(z * z, ones_bd) * inv_n
        return (z * lax.rsqrt(ms + NORM_EPS)) * g

    q = headnorm(t[:, 0:d], qg_ref[...]) * (HEAD_DIM ** -0.5 * log2e)
    k = headnorm(t[:, d:2 * d], kg_ref[...])
    v = t[:, 2 * d:3 * d]
    neg_bound = -log2e * _score_bound(qg_ref[...], kg_ref[...])

    z = fl + bf_ref[...]
    log_f = jnp.minimum(z, 0.0) - jnp.log1p(jnp.exp(-jnp.abs(z)))
    row = lax.broadcasted_iota(jnp.int32, (tm, tm), 0)
    col = lax.broadcasted_iota(jnp.int32, (tm, tm), 1)
    cs = _cumsum_rows(jnp.where(row >= col, 1.0, 0.0).astype(BF16), log_f) + carry_ref[...]
    carry_ref[...] = cs[tm - 1:tm, :]
    c_hi, c_mid, c_lo = _split3(cs * log2e)

    lane = lax.broadcasted_iota(jnp.int32, (tm, LANES), 1)
    is_data = lane < HEAD_DIM
    in_range = lambda lo, hi: ((lane >= HEAD_DIM + lo) & (lane < HEAD_DIM + hi)).astype(F32)
    q_base = in_range(3, 7)
    k_base = jnp.where(lane == HEAD_DIM + 6, neg_bound, in_range(0, 3))
    one_col = in_range(0, 1)

    def place(pieces, base):
        out = base
        for off, piece in pieces:
            out = jnp.where(lane == off, piece, out)
        return out

    for h_idx in range(n_heads):
        if n_lead + h_idx < len(tiles):
            project(tiles[n_lead + h_idx])
        pair = slice((h_idx // 2) * LANES, (h_idx // 2 + 1) * LANES)
        shift = lambda z: z[:, pair] if h_idx % 2 == 0 else pltpu.roll(z[:, pair], HEAD_DIM, axis=1)
        ch = [jnp.broadcast_to(piece[:, h_idx:h_idx + 1], (tm, LANES)) for piece in (c_hi, c_mid, c_lo)]
        q_extra = place([(HEAD_DIM + i, ch[i]) for i in range(3)], q_base)
        k_extra = place([(HEAD_DIM + 3 + i, -ch[i]) for i in range(3)], k_base)
        qa_ref[0, h_idx] = jnp.where(is_data, shift(q), q_extra).astype(BF16)
        ka_ref[0, h_idx] = jnp.where(is_data, shift(k), k_extra).astype(BF16)
        va_ref[0, h_idx, 0] = jnp.where(is_data, shift(v), one_col).T.astype(BF16)
    for tile in tiles[n_lead + n_heads:]:
        project(tile)


def _inproj_fox(x, g, w_cat, q_g, k_g, b_f, ones_bd, rw_cols, gate_cols, tm):
    b, s, d_model = x.shape
    d = q_g.shape[-1]
    n_heads = d // HEAD_DIM
    const = lambda shape: pl.BlockSpec(shape, lambda i, j: (0,) * len(shape))
    rows = lambda width: pl.BlockSpec((1, tm, width), lambda i, j: (i, j, 0))
    head_spec = pl.BlockSpec((1, n_heads, tm, LANES), lambda i, j: (i, 0, j, 0))
    head_shape = jax.ShapeDtypeStruct((b, n_heads, s, LANES), BF16)
    return pl.pallas_call(
        functools.partial(_inproj_fox_body, n_heads=n_heads),
        grid=(b, s // tm),
        in_specs=[
            rows(d_model), const((1, d_model)), const(w_cat.shape),
            const((1, d)), const((1, d)), const((1, LANES)), const((MXU_DIM, MXU_DIM)),
        ],
        out_specs=[rows(rw_cols), rows(gate_cols), head_spec, head_spec,
                   pl.BlockSpec((1, n_heads, 1, LANES, tm), lambda i, j: (i, 0, j, 0, 0))],
        out_shape=[jax.ShapeDtypeStruct((b, s, rw_cols), F32), jax.ShapeDtypeStruct((b, s, gate_cols), BF16),
                   head_shape, head_shape, jax.ShapeDtypeStruct((b, n_heads, s // tm, LANES, tm), BF16)],
        scratch_shapes=[pltpu.VMEM((1, LANES), F32)],
        compiler_params=pltpu.CompilerParams(
            dimension_semantics=("arbitrary", "arbitrary"), vmem_limit_bytes=VMEM_LIMIT),
        name="inproj_fox",
    )(x, g, w_cat, q_g, k_g, b_f, ones_bd)


def _rwkv_body(p_ref, mu_ref, wup_ref, w0_ref, aup_ref, a0_ref, kk_ref, ka_ref, rk_ref,
               gng_ref, gnb_ref, ones_ref, y_ref, state_ref, prev_ref, *, lora, n_chunks):
    @pl.when(pl.program_id(1) == 0)
    def _():
        state_ref[...] = jnp.zeros_like(state_ref)
        prev_ref[...] = jnp.zeros_like(prev_ref)

    p = p_ref[0]
    R = p.shape[0]
    L = CHUNK
    d = y_ref.shape[-1]
    n_pairs = d // LANES
    ones_bd = ones_ref[...]

    row_p = lax.broadcasted_iota(jnp.int32, p.shape, 0)
    prev = jnp.where(row_p == 0, prev_ref[...], pltpu.roll(p, 1, axis=0))
    prev_ref[...] = p[R - 1:R, :]
    xs = p + (prev - p) * mu_ref[...]

    r = xs[:, 0:d]
    k = xs[:, d:2 * d]
    v = xs[:, 2 * d:3 * d]
    wl = xs[:, 3 * d:3 * d + lora]
    al = xs[:, 3 * d + lora:3 * d + 2 * lora]

    w_raw = w0_ref[...] + _dot_x3(jnp.tanh(wl), wup_ref[...])
    logw = (-math.exp(-0.5)) * _sigmoid(w_raw)
    a = _sigmoid(a0_ref[...] + _dot_x3(al, aup_ref[...]))

    kkr = k * kk_ref[...]
    kk = kkr / jnp.maximum(jnp.sqrt(_headsum(kkr * kkr, ones_bd, pieces=1)), 1e-12)
    k2 = k * (1.0 + (a - 1.0) * ka_ref[...])
    bonus = _headsum(r * k2 * rk_ref[...], ones_bd, pieces=1) * v

    row_r = lax.broadcasted_iota(jnp.int32, (R, R), 0)
    col_r = lax.broadcasted_iota(jnp.int32, (R, R), 1)
    tri = jnp.where(((row_r >> 6) == (col_r >> 6)) & (row_r >= col_r), 1.0, 0.0).astype(BF16)
    cum = _cumsum_rows(tri, logw)
    cum_end = jnp.concatenate(
        [jnp.broadcast_to(cum[(c + 1) * L - 1:(c + 1) * L, :], (L, d)) for c in range(n_chunks)], axis=0)
    e_neg = jnp.exp(-cum)
    e_end = jnp.exp(cum_end - cum)
    g_end = jnp.exp(cum_end)
    r_t = r * jnp.exp(cum)
    a_t = -kk * jnp.exp(cum - logw)
    kka = kk * a
    b_t = kka * e_neg
    k_t = k2 * e_neg
    b_e = kka * e_end
    k_e = k2 * e_end

    P = LANES
    chains = [(c, j) for c in range(n_chunks) for j in range(n_pairs)]
    pair = lambda z, c, j: z[c * L:(c + 1) * L, j * P:(j + 1) * P]
    lane = lax.broadcasted_iota(jnp.int32, (L, P), 1)
    first = lane < HEAD_DIM
    stack = lambda z: jnp.concatenate([jnp.where(first, z, 0.0), jnp.where(first, 0.0, z)], axis=0)

    row = lax.broadcasted_iota(jnp.int32, (P, P), 0)
    col = lax.broadcasted_iota(jnp.int32, (P, P), 1)
    same = lambda s: (row >> s) == (col >> s)
    strict = same(6) & (row > col)
    incl = same(6) & (row >= col)
    eye = row == col
    eye_f = eye.astype(F32)

    amat = [_dot_nt(_bf(jnp.concatenate([stack(pair(a_t, c, j)), stack(pair(r_t, c, j))], axis=0)),
                    _bf(jnp.concatenate([pair(b_t, c, j)] * 2 + [pair(k_t, c, j)] * 2, axis=0)))
            for c, j in chains]
    n_st = [jnp.where(strict, m[:P, :P], 0.0) for m in amat]
    a_ak = [_bf(jnp.where(strict, m[:P, P:], 0.0)) for m in amat]
    a_r = [_bf(jnp.concatenate([jnp.where(incl, m[P:, :P], 0.0), jnp.where(incl, m[P:, P:], 0.0)], axis=1))
           for m in amat]

    n0 = [_bf(jnp.where(same(3), n, 0.0)) for n in n_st]
    n2 = [_bf(_dot(n, n)) for n in n0]
    n4 = [_bf(_dot(n, n)) for n in n2]
    t = [eye_f + n.astype(F32) for n in n0]
    t = [x + _dot(_bf(x), m) for x, m in zip(t, n2)]
    t = [x + _dot(_bf(x), m) for x, m in zip(t, n4)]
    for s in (3, 4, 5):
        off_mask = same(s + 1) & jnp.logical_not(same(s))
        n_off = [_bf(jnp.where(off_mask, n, 0.0)) for n in n_st]
        tb = [_bf(x) for x in t]
        half = [_bf(_dot(x, m)) for x, m in zip(tb, n_off)]
        t = [x + _dot(h, xb) for x, h, xb in zip(t, half, tb)]
    tb = [_bf(x) for x in t]

    v_st = [_bf(stack(pair(v, c, j))) for c, j in chains]
    x_st = [_dot(m, vs) for m, vs in zip(a_ak, v_st)]
    wu = [_dot(x, jnp.concatenate([_bf(stack(pair(a_t, c, j))), _bf(xs_)], axis=1))
          for x, xs_, (c, j) in zip(tb, x_st, chains)]
    zeros_b = jnp.zeros((P, P), BF16)
    rhs2 = [jnp.concatenate([_bf(w), jnp.concatenate([zeros_b, vs], axis=1)], axis=0)
            for w, vs in zip(wu, v_st)]
    top = [_dot(m, rh) for m, rh in zip(a_r, rhs2)]
    bot = [_dot_tn(_bf(jnp.concatenate([stack(pair(b_e, c, j)), stack(pair(k_e, c, j))], axis=0)), rh)
           for rh, (c, j) in zip(rhs2, chains)]

    fold = lambda z: z[:L] + z[L:]
    ys = {}
    for idx, (c, j) in enumerate(chains):
        rw = pair(r_t, c, j) + fold(top[idx][:, :P])
        m_mat = jnp.where(eye, pair(g_end, c, j)[0:1, :], 0.0) + bot[idx][:, :P]
        s0 = _bf(state_ref[j])
        ys[(c, j)] = _dot(_bf(rw), s0) + fold(top[idx][:, P:])
        state_ref[j] = _dot(_bf(m_mat), s0) + bot[idx][:, P:]

    y = jnp.concatenate(
        [jnp.concatenate([ys[(c, j)] for j in range(n_pairs)], axis=1) for c in range(n_chunks)], axis=0)
    inv_n = 1.0 / HEAD_DIM
    mean = _headsum(y, ones_bd) * inv_n
    yc = y - mean
    var = _headsum(yc * yc, ones_bd) * inv_n
    y_ref[0] = ((yc * lax.rsqrt(var + GN_EPS)) * gng_ref[...] + gnb_ref[...] + bonus).astype(y_ref.dtype)


def _rwkv(p_rw, mu, w_up, w0, a_up, a0, k_k, k_a, r_k, gn_g, gn_b, ones_bd, n_chunks):
    b, s, cols = p_rw.shape
    lora, d = w_up.shape
    rows = n_chunks * CHUNK
    row = lambda a: a.reshape(1, -1)
    const = lambda shape: pl.BlockSpec(shape, lambda i, j: (0,) * len(shape))
    return pl.pallas_call(
        functools.partial(_rwkv_body, lora=lora, n_chunks=n_chunks),
        grid=(b, s // rows),
        in_specs=[
            pl.BlockSpec((1, rows, cols), lambda i, j: (i, j, 0)),
            const((1, cols)), const((lora, d)), const((1, d)), const((lora, d)), const((1, d)),
            const((1, d)), const((1, d)), const((1, d)), const((1, d)), const((1, d)), const((MXU_DIM, MXU_DIM)),
        ],
        out_specs=pl.BlockSpec((1, rows, d), lambda i, j: (i, j, 0)),
        out_shape=jax.ShapeDtypeStruct((b, s, d), BF16),
        scratch_shapes=[pltpu.VMEM((d // LANES, LANES, LANES), F32), pltpu.VMEM((1, cols), F32)],
        compiler_params=pltpu.CompilerParams(
            dimension_semantics=("arbitrary", "arbitrary"), vmem_limit_bytes=VMEM_LIMIT),
        name="rwkv7",
    )(p_rw, row(mu), w_up, row(w0), a_up, row(a0), row(k_k), row(k_a), row(r_k), row(gn_g), row(gn_b), ones_bd)


def _attn_body(q_ref, k_ref, vt_ref, o_ref, *, heads_per_step, tq, online_max):
    qi = pl.program_id(2)
    tk = vt_ref.shape[-1]
    n_sub = tq // tk
    heads = range(heads_per_step)
    qs = [q_ref[0, hh] for hh in heads]

    items = [(u, hh) for u in range(n_sub) for hh in heads]
    zero = jnp.zeros((V_ROWS, tq), F32)

    def scores(jb, u, hh, diag):
        kb = k_ref[0, hh, pl.ds(pl.multiple_of((jb * n_sub + u) * tk, tk), tk), :]
        if not diag:
            return _dot_nt(kb, qs[hh])
        st = _dot_nt(kb, qs[hh][u * tk:, :])
        key = lax.broadcasted_iota(jnp.int32, st.shape, 0)
        qry = lax.broadcasted_iota(jnp.int32, st.shape, 1)
        return jnp.where(key <= qry, st, NEG_BIG)

    def values(sub_block, hh):
        return vt_ref[0, hh, sub_block, 0:V_ROWS, :]

    def add_cols(acc, part):
        off = acc.shape[1] - part.shape[1]
        return acc + part if off == 0 else jnp.concatenate([acc[:, :off], acc[:, off:] + part], axis=1)

    if online_max:
        def block(jb, carry, diag):
            new = list(carry)
            for u, hh in items:
                st = scores(jb, u, hh, diag)
                off = tq - st.shape[1]
                m, acc = new[hh]
                m_new = jnp.maximum(m[:, off:], jnp.max(st, axis=0, keepdims=True))
                m_new = m_new if off == 0 else jnp.concatenate([m[:, :off], m_new], axis=1)
                p = jnp.exp2(st - m_new[:, off:]).astype(BF16)
                new[hh] = (m_new, add_cols(acc * jnp.exp2(m - m_new), _dot(values(jb * n_sub + u, hh), p)))
            return tuple(new)

        init = tuple((jnp.full((1, tq), NEG_BIG, F32), zero) for _ in heads)
        carry = lax.fori_loop(0, qi, functools.partial(block, diag=False), init)
        accs = [c[1] for c in block(qi, carry, True)]
    else:
        def blocks(jbs, carry, diag):
            work = [(jb, u, hh) for jb in jbs for u, hh in items]
            tiles = [scores(jb, u, hh, diag) for jb, u, hh in work]
            probs = [jnp.exp2(st).astype(BF16) for st in tiles]
            new = list(carry)
            for (jb, u, hh), p in zip(work, probs):
                new[hh] = add_cols(new[hh], _dot(values(jb * n_sub + u, hh), p))
            return tuple(new)

        per_iter = ATTN_BLOCKS_PER_ITER
        carry = lax.fori_loop(
            0, qi // per_iter,
            lambda i, c: blocks([i * per_iter + o for o in range(per_iter)], c, False),
            tuple(zero for _ in heads))
        first_left = (qi // per_iter) * per_iter
        carry = lax.fori_loop(first_left, qi, lambda jb, c: blocks([jb], c, False), carry)
        accs = blocks([qi], carry, True)
    out_t = jnp.concatenate([acc[:HEAD_DIM] / acc[HEAD_DIM:HEAD_DIM + 1] for acc in accs], axis=0)
    o_ref[0] = out_t.T.astype(o_ref.dtype)


def _attention(qa, ka, vt, tq, online_max):
    b, h, s, _ = qa.shape
    hps = ATTN_HEADS_PER_STEP if h % ATTN_HEADS_PER_STEP == 0 else LANES // HEAD_DIM
    return pl.pallas_call(
        functools.partial(_attn_body, heads_per_step=hps, tq=tq, online_max=online_max),
        grid=(b, h // hps, s // tq),
        in_specs=[
            pl.BlockSpec((1, hps, tq, LANES), lambda i, g, j: (i, g, j, 0)),
            pl.BlockSpec((1, hps, s, LANES), lambda i, g, j: (i, g, 0, 0)),
            pl.BlockSpec((1, hps) + vt.shape[2:], lambda i, g, j: (i, g, 0, 0, 0)),
        ],
        out_specs=pl.BlockSpec((1, tq, hps * HEAD_DIM), lambda i, g, j: (i, j, g)),
        out_shape=jax.ShapeDtypeStruct((b, s, h * HEAD_DIM), BF16),
        compiler_params=pltpu.CompilerParams(
            dimension_semantics=("parallel", "parallel", "arbitrary"), vmem_limit_bytes=VMEM_LIMIT),
        name="fox_attention_online_max" if online_max else "fox_attention",
    )(qa, ka, vt)


def _outproj_body(x_ref, yrw_ref, yfox_ref, gate_ref, w_ref, fg_ref, o_ref, *, final):
    g = gate_ref[...].astype(F32)
    sg = g * _sigmoid(g)
    d_rw = yrw_ref.shape[1]
    y_rw = (yrw_ref[...].astype(F32) * sg[:, :d_rw]).astype(BF16)
    y_fox = (yfox_ref[...].astype(F32) * sg[:, d_rw:]).astype(BF16)
    z = x_ref[...] + _dot(y_rw, w_ref[0:d_rw, :]) + _dot(y_fox, w_ref[d_rw:, :])
    if final:
        z = (z * lax.rsqrt(jnp.mean(z * z, axis=-1, keepdims=True) + NORM_EPS)) * fg_ref[...]
    o_ref[...] = z


def _outproj(x2, y_rw, y_fox, gate, w_out, final_g, final, tm):
    n, d = x2.shape
    d_rw, d_fox = y_rw.shape[1], y_fox.shape[1]
    return pl.pallas_call(
        functools.partial(_outproj_body, final=final),
        grid=(n // tm,),
        in_specs=[
            pl.BlockSpec((tm, d), lambda i: (i, 0)),
            pl.BlockSpec((tm, d_rw), lambda i: (i, 0)),
            pl.BlockSpec((tm, d_fox), lambda i: (i, 0)),
            pl.BlockSpec((tm, d_rw + d_fox), lambda i: (i, 0)),
            pl.BlockSpec(w_out.shape, lambda i: (0, 0)),
            pl.BlockSpec((1, d), lambda i: (0, 0)),
        ],
        out_specs=pl.BlockSpec((tm, d), lambda i: (i, 0)),
        out_shape=jax.ShapeDtypeStruct((n, d), F32),
        compiler_params=pltpu.CompilerParams(
            dimension_semantics=("parallel",), vmem_limit_bytes=VMEM_LIMIT),
        name="outproj",
    )(x2, y_rw, y_fox, gate, w_out, final_g)


def _pick_tile(n, target):
    t = min(n, target)
    while n % t:
        t //= 2
    return t


def kernel(x, norm_g, w_in, rw_mu, rw_w_up, rw_w0, rw_a_up, rw_a0, rw_k_k, rw_k_a, rw_r_k, rw_gn_g, rw_gn_b,
           fox_q_g, fox_k_g, fox_b_f, w_out, final_g):
    b, s, d_model = x.shape
    depth = norm_g.shape[0]
    d_rw = rw_w0.shape[-1]
    lora = rw_w_up.shape[1]
    d_mix = w_out.shape[1]
    d_fox = d_mix - d_rw
    h_fox = fox_b_f.shape[-1]
    rw_cols = 3 * d_rw + 2 * lora
    fox_cols = 3 * d_fox + h_fox
    assert d_rw == d_fox and s % CHUNK == 0 and h_fox <= LANES

    head = lax.broadcasted_iota(jnp.int32, (MXU_DIM, MXU_DIM), 0) // HEAD_DIM
    ones_bd = (head == head.T).astype(BF16)

    tm_out = _pick_tile(b * s, 256)
    tm_in = _pick_tile(s, 256)
    tq = _pick_tile(s, 512)

    x2 = x.reshape(b * s, d_model)
    for l in range(depth):
        w = w_in[l]
        w_fl = jnp.pad(w[:, rw_cols + 3 * d_fox:rw_cols + fox_cols], ((0, 0), (0, LANES - h_fox)))
        w_cat = jnp.concatenate(
            [w[:, :rw_cols], w[:, rw_cols:rw_cols + 3 * d_fox], w[:, rw_cols + fox_cols:], w_fl], axis=1
        ).astype(BF16)
        tile_heads = lambda g: jnp.tile(g, d_fox // HEAD_DIM).reshape(1, -1)
        b_f = jnp.pad(fox_b_f[l], (0, LANES - h_fox)).reshape(1, -1)
        q_g, k_g = tile_heads(fox_q_g[l]), tile_heads(fox_k_g[l])
        p_rw, gate, qa, ka, va = _inproj_fox(x2.reshape(b, s, d_model), norm_g[l].reshape(1, -1), w_cat,
                                             q_g, k_g, b_f, ones_bd, rw_cols, d_mix, tm_in)

        y_rw = _rwkv(p_rw, rw_mu[l], rw_w_up[l], rw_w0[l], rw_a_up[l], rw_a0[l],
                     rw_k_k[l], rw_k_a[l], rw_r_k[l].reshape(-1), rw_gn_g[l], rw_gn_b[l], ones_bd,
                     RWKV_CHUNKS_PER_STEP if s % (RWKV_CHUNKS_PER_STEP * CHUNK) == 0 else 1)

        y_fox = lax.cond(_score_bound(q_g, k_g)[0, 0] <= SAFE_SCORE_BOUND,
                         functools.partial(_attention, tq=tq, online_max=False),
                         functools.partial(_attention, tq=tq, online_max=True), qa, ka, va)

        x2 = _outproj(x2, y_rw.reshape(b * s, d_rw), y_fox.reshape(b * s, d_fox), gate.reshape(b * s, d_mix),
                      w_out[l].astype(BF16), final_g.reshape(1, -1), l == depth - 1, tm_out)
    return x2.reshape(b, s, d_model)
```

```python
import functools
import math

import jax
import jax.numpy as jnp
from jax import lax
from jax.experimental import pallas as pl
from jax.experimental.pallas import tpu as pltpu

F32 = jnp.float32
BF16 = jnp.bfloat16

HEAD_DIM = 64
NORM_EPS = 1e-6
GN_EPS = 64e-5
CHUNK = 64
RWKV_CHUNKS_PER_STEP = 4
LANES = 128
MXU_DIM = 256
NEG_BIG = -1e30
ATTN_HEADS_PER_STEP = 4
ATTN_BLOCKS_PER_ITER = 2
V_ROWS = 80
SAFE_SCORE_BOUND = 40.0
VMEM_LIMIT = 56 * 1024 * 1024


def _dot(a, b):
    return jnp.dot(a, b, preferred_element_type=F32)


def _dot_nt(a, b):
    return lax.dot_general(a, b, (((1,), (1,)), ((), ())), preferred_element_type=F32)


def _dot_tn(a, b):
    return lax.dot_general(a, b, (((0,), (0,)), ((), ())), preferred_element_type=F32)


def _sigmoid(x):
    return 1.0 / (1.0 + jnp.exp(-x))


def _bf(x):
    return x.astype(BF16)


def _split2(x):
    hi = x.astype(BF16)
    return hi, (x - hi.astype(F32)).astype(BF16)


def _split3(x):
    hi = x.astype(BF16)
    r1 = x - hi.astype(F32)
    mid = r1.astype(BF16)
    lo = (r1 - mid.astype(F32)).astype(BF16)
    return hi.astype(F32), mid.astype(F32), lo.astype(F32)


def _stack_x3_rhs(b):
    bh, bl = _split2(b)
    return jnp.concatenate([bh, bl, bh], axis=0)


def _dot_x3(a, b_stacked):
    ah, al = _split2(a)
    return _dot(jnp.concatenate([ah, ah, al], axis=1), b_stacked)


def _dot_pieces(x, ones_bf16, pieces):
    out = None
    for _ in range(pieces):
        hi = x.astype(BF16)
        term = _dot(hi, ones_bf16)
        out = term if out is None else out + term
        x = x - hi.astype(F32)
    return out


def _cumsum_rows(tri_bf16, x):
    out = None
    for _ in range(3):
        hi = x.astype(BF16)
        term = _dot(tri_bf16, hi)
        out = term if out is None else out + term
        x = x - hi.astype(F32)
    return out


def _headsum(x, ones_bd, pieces=2):
    width = x.shape[1]
    return jnp.concatenate([_dot_pieces(x[:, j:j + MXU_DIM], ones_bd, pieces) for j in range(0, width, MXU_DIM)],
                           axis=1)


def _score_bound(q_g, k_g):
    amax = lambda g: jnp.max(jnp.abs(g), axis=-1, keepdims=True)
    return (HEAD_DIM ** 0.5) * amax(q_g) * amax(k_g)


def _inproj_fox_body(x_ref, g_ref, w_ref, qg_ref, kg_ref, bf_ref, ones_ref,
                     rw_ref, gate_ref, qa_ref, ka_ref, va_ref, carry_ref, *, n_heads):
    @pl.when(pl.program_id(1) == 0)
    def _():
        carry_ref[...] = jnp.zeros_like(carry_ref)

    x = x_ref[0]
    tm = x.shape[0]
    d = n_heads * HEAD_DIM
    rw_cols, gate_cols = rw_ref.shape[-1], gate_ref.shape[-1]
    off_qkv, off_gate, off_fl = rw_cols, rw_cols + 3 * d, rw_cols + 3 * d + gate_cols
    y = x * lax.rsqrt(jnp.mean(x * x, axis=-1, keepdims=True) + NORM_EPS)
    h = (y * g_ref[...]).astype(BF16)

    t = _dot(h, w_ref[:, off_qkv:off_qkv + 3 * d])
    fl = _dot(h, w_ref[:, off_fl:off_fl + LANES])

    tiles = [(rw_ref, c0, c0) for c0 in range(0, rw_cols, MXU_DIM)]
    tiles += [(gate_ref, c0, off_gate + c0) for c0 in range(0, gate_cols, MXU_DIM)]

    def project(tile):
        ref, dst, src = tile
        width = min(MXU_DIM, ref.shape[-1] - dst)
        ref[0, :, dst:dst + width] = _dot(h, w_ref[:, src:src + width]).astype(ref.dtype)

    n_lead = max(len(tiles) - n_heads, 0)
    for tile in tiles[:n_lead]:
        project(tile)

    ones_bd = ones_ref[...]
    inv_n = 1.0 / HEAD_DIM
    log2e = math.log2(math.e)

    def headnorm(z, g):
        ms = _headsum(z * z, ones_bd) * inv_n
        return (z * lax.rsqrt(ms + NORM_EPS)) * g

    q = headnorm(t[:, 0:d], qg_ref[...]) * (HEAD_DIM ** -0.5 * log2e)
    k = headnorm(t[:, d:2 * d], kg_ref[...])
    v = t[:, 2 * d:3 * d]
    neg_bound = -log2e * _score_bound(qg_ref[...], kg_ref[...])

    z = fl + bf_ref[...]
    log_f = jnp.minimum(z, 0.0) - jnp.log1p(jnp.exp(-jnp.abs(z)))
    row = lax.broadcasted_iota(jnp.int32, (tm, tm), 0)
    col = lax.broadcasted_iota(jnp.int32, (tm, tm), 1)
    cs = _cumsum_rows(jnp.where(row >= col, 1.0, 0.0).astype(BF16), log_f) + carry_ref[...]
    carry_ref[...] = cs[tm - 1:tm, :]
    c_hi, c_mid, c_lo = _split3(cs * log2e)

    lane = lax.broadcasted_iota(jnp.int32, (tm, LANES), 1)
    is_data = lane < HEAD_DIM
    in_range = lambda lo, hi: ((lane >= HEAD_DIM + lo) & (lane < HEAD_DIM + hi)).astype(F32)
    q_base = in_range(3, 7)
    k_base = jnp.where(lane == HEAD_DIM + 6, neg_bound, in_range(0, 3))
    one_col = in_range(0, 1)

    def place(pieces, base):
        out = base
        for off, piece in pieces:
            out = jnp.where(lane == off, piece, out)
        return out

    for h_idx in range(n_heads):
        if n_lead + h_idx < len(tiles):
            project(tiles[n_lead + h_idx])
        pair = slice((h_idx // 2) * LANES, (h_idx // 2 + 1) * LANES)
        shift = lambda z: z[:, pair] if h_idx % 2 == 0 else pltpu.roll(z[:, pair], HEAD_DIM, axis=1)
        ch = [jnp.broadcast_to(piece[:, h_idx:h_idx + 1], (tm, LANES)) for piece in (c_hi, c_mid, c_lo)]
        q_extra = place([(HEAD_DIM + i, ch[i]) for i in range(3)], q_base)
        k_extra = place([(HEAD_DIM + 3 + i, -ch[i]) for i in range(3)], k_base)
        qa_ref[0, h_idx] = jnp.where(is_data, shift(q), q_extra).astype(BF16)
        ka_ref[0, h_idx] = jnp.where(is_data, shift(k), k_extra).astype(BF16)
        va_ref[0, h_idx, 0] = jnp.where(is_data, shift(v), one_col).T.astype(BF16)
    for tile in tiles[n_lead + n_heads:]:
        project(tile)


def _inproj_fox(x, g, w_cat, q_g, k_g, b_f, ones_bd, rw_cols, gate_cols, tm):
    b, s, d_model = x.shape
    d = q_g.shape[-1]
    n_heads = d // HEAD_DIM
    const = lambda shape: pl.BlockSpec(shape, lambda i, j: (0,) * len(shape))
    rows = lambda width: pl.BlockSpec((1, tm, width), lambda i, j: (i, j, 0))
    head_spec = pl.BlockSpec((1, n_heads, tm, LANES), lambda i, j: (i, 0, j, 0))
    head_shape = jax.ShapeDtypeStruct((b, n_heads, s, LANES), BF16)
    return pl.pallas_call(
        functools.partial(_inproj_fox_body, n_heads=n_heads),
        grid=(b, s // tm),
        in_specs=[
            rows(d_model), const((1, d_model)), const(w_cat.shape),
            const((1, d)), const((1, d)), const((1, LANES)), const((MXU_DIM, MXU_DIM)),
        ],
        out_specs=[rows(rw_cols), rows(gate_cols), head_spec, head_spec,
                   pl.BlockSpec((1, n_heads, 1, LANES, tm), lambda i, j: (i, 0, j, 0, 0))],
        out_shape=[jax.ShapeDtypeStruct((b, s, rw_cols), F32), jax.ShapeDtypeStruct((b, s, gate_cols), BF16),
                   head_shape, head_shape, jax.ShapeDtypeStruct((b, n_heads, s // tm, LANES, tm), BF16)],
        scratch_shapes=[pltpu.VMEM((1, LANES), F32)],
        compiler_params=pltpu.CompilerParams(
            dimension_semantics=("arbitrary", "arbitrary"), vmem_limit_bytes=VMEM_LIMIT),
        name="inproj_fox",
    )(x, g, w_cat, q_g, k_g, b_f, ones_bd)


def _rwkv_body(p_ref, mu_ref, wup_ref, w0_ref, aup_ref, a0_ref, kk_ref, ka_ref, rk_ref, gng_ref, gnb_ref, ones_ref,
               y_ref, state_ref, prev_ref, m_ref, n_ref, rw_ref, yp_ref, bonus_ref, *, lora, n_chunks):
    @pl.when(pl.program_id(1) == 0)
    def _():
        state_ref[...] = jnp.zeros_like(state_ref)
        prev_ref[...] = jnp.zeros_like(prev_ref)
        for ref in (m_ref, n_ref, rw_ref, yp_ref, bonus_ref):
            ref[...] = jnp.zeros_like(ref)

    L = CHUNK
    P = LANES
    R = n_chunks * L
    d = y_ref.shape[-1]
    n_pairs = d // P
    ones_bd = ones_ref[...]
    chains = [(c, j) for c in range(n_chunks) for j in range(n_pairs)]
    row_l = lax.broadcasted_iota(jnp.int32, (L, L), 0)
    col_l = lax.broadcasted_iota(jnp.int32, (L, L), 1)
    tri = jnp.where(row_l >= col_l, 1.0, 0.0).astype(BF16)
    wup_x3 = _stack_x3_rhs(wup_ref[...])
    aup_x3 = _stack_x3_rhs(aup_ref[...])

    def state_pass(c):
        out = []
        for j in range(n_pairs):
            idx = c * n_pairs + j
            s0 = _bf(state_ref[j])
            out.append(_dot(rw_ref[idx], s0) + yp_ref[idx])
            state_ref[j] = _dot(m_ref[idx], s0) + n_ref[idx]
        return jnp.concatenate(out, axis=1)

    def prep(c):
        p = p_ref[0, c * L:(c + 1) * L, :]
        before = prev_ref[...] if c == 0 else p_ref[0, c * L - 1:c * L, :]
        row_p = lax.broadcasted_iota(jnp.int32, p.shape, 0)
        prev = jnp.where(row_p == 0, before, pltpu.roll(p, 1, axis=0))
        xs = p + (prev - p) * mu_ref[...]
        r = xs[:, 0:d]
        k = xs[:, d:2 * d]
        v = xs[:, 2 * d:3 * d]
        wl = xs[:, 3 * d:3 * d + lora]
        al = xs[:, 3 * d + lora:3 * d + 2 * lora]
        w_raw = w0_ref[...] + _dot_x3(jnp.tanh(wl), wup_x3)
        logw = (-math.exp(-0.5)) * _sigmoid(w_raw)
        a = _sigmoid(a0_ref[...] + _dot_x3(al, aup_x3))
        kkr = k * kk_ref[...]
        kk = kkr * lax.rsqrt(jnp.maximum(_headsum(kkr * kkr, ones_bd, pieces=1), 1e-24))
        k2 = k * (1.0 + (a - 1.0) * ka_ref[...])
        bonus = _headsum(r * k2 * rk_ref[...], ones_bd, pieces=1) * v
        cum = _cumsum_rows(tri, logw)
        cum_end = cum[L - 1:L, :]
        e_neg = jnp.exp(-cum)
        e_end = jnp.exp(cum_end - cum)
        kka = kk * a
        return dict(
            v=v, bonus=bonus,
            g_end=jnp.exp(cum_end),
            r_t=r * jnp.exp(cum),
            a_t=-kk * jnp.exp(cum - logw),
            b_t=kka * e_neg, k_t=k2 * e_neg,
            b_e=kka * e_end, k_e=k2 * e_end)

    ys, cur = [], []
    for c in range(n_chunks):
        ys.append(state_pass(c))
        cur.append(prep(c))
    prev_ref[...] = p_ref[0, R - 1:R, :]

    y = jnp.concatenate(ys, axis=0)
    inv_n = 1.0 / HEAD_DIM
    mean = _headsum(y, ones_bd) * inv_n
    yc = y - mean
    var = _headsum(yc * yc, ones_bd) * inv_n
    y_ref[0] = ((yc * lax.rsqrt(var + GN_EPS)) * gng_ref[...] + gnb_ref[...] + bonus_ref[...]).astype(y_ref.dtype)
    bonus_ref[...] = jnp.concatenate([t["bonus"] for t in cur], axis=0)

    pair = lambda name, c, j: cur[c][name][:, j * P:(j + 1) * P]
    lane = lax.broadcasted_iota(jnp.int32, (L, P), 1)
    first = lane < HEAD_DIM
    stack = lambda z: jnp.concatenate([jnp.where(first, z, 0.0), jnp.where(first, 0.0, z)], axis=0)

    row = lax.broadcasted_iota(jnp.int32, (P, P), 0)
    col = lax.broadcasted_iota(jnp.int32, (P, P), 1)
    same = lambda s: (row >> s) == (col >> s)
    strict = same(6) & (row > col)
    incl = same(6) & (row >= col)
    eye = row == col

    amat = [_dot_nt(_bf(jnp.concatenate([stack(pair("a_t", c, j)), stack(pair("r_t", c, j))], axis=0)),
                    _bf(jnp.concatenate([pair("b_t", c, j)] * 2 + [pair("k_t", c, j)] * 2, axis=0)))
            for c, j in chains]
    n_st = [jnp.where(strict, m[:P, :P], 0.0) for m in amat]
    a_ak = [_bf(jnp.where(strict, m[:P, P:], 0.0)) for m in amat]
    a_r = [_bf(jnp.concatenate([jnp.where(incl, m[P:, :P], 0.0), jnp.where(incl, m[P:, P:], 0.0)], axis=1))
           for m in amat]

    n0_f = [jnp.where(same(3), n, 0.0) for n in n_st]
    n0 = [_bf(n) for n in n0_f]
    n2 = [_bf(_dot(n, n)) for n in n0]
    n4 = [_bf(_dot(n, n)) for n in n2]
    t = [jnp.where(eye, 1.0, n) for n in n0_f]
    t = [x + _dot(_bf(x), m) for x, m in zip(t, n2)]
    t = [x + _dot(_bf(x), m) for x, m in zip(t, n4)]
    for s in (3, 4, 5):
        off_mask = same(s + 1) & jnp.logical_not(same(s))
        n_off = [_bf(jnp.where(off_mask, n, 0.0)) for n in n_st]
        tb = [_bf(x) for x in t]
        half = [_bf(_dot(x, m)) for x, m in zip(tb, n_off)]
        t = [x + _dot(h, xb) for x, h, xb in zip(t, half, tb)]
    tb = [_bf(x) for x in t]

    v_st = [_bf(stack(pair("v", c, j))) for c, j in chains]
    x_st = [_dot(m, vs) for m, vs in zip(a_ak, v_st)]
    wu = [_dot(x, jnp.concatenate([_bf(stack(pair("a_t", c, j))), _bf(xs_)], axis=1))
          for x, xs_, (c, j) in zip(tb, x_st, chains)]
    zeros_b = jnp.zeros((P, P), BF16)
    rhs2 = [jnp.concatenate([_bf(w), jnp.concatenate([zeros_b, vs], axis=1)], axis=0)
            for w, vs in zip(wu, v_st)]
    top = [_dot(m, rh) for m, rh in zip(a_r, rhs2)]
    bot = [_dot_tn(_bf(jnp.concatenate([stack(pair("b_e", c, j)), stack(pair("k_e", c, j))], axis=0)), rh)
           for rh, (c, j) in zip(rhs2, chains)]

    fold = lambda z: z[:L] + z[L:]
    for idx, (c, j) in enumerate(chains):
        rw_ref[idx] = _bf(pair("r_t", c, j) + fold(top[idx][:, :P]))
        yp_ref[idx] = fold(top[idx][:, P:])
        m_ref[idx] = _bf(jnp.where(eye, pair("g_end", c, j), 0.0) + bot[idx][:, :P])
        n_ref[idx] = bot[idx][:, P:]


def _rwkv(p_rw, mu, w_up, w0, a_up, a0, k_k, k_a, r_k, gn_g, gn_b, ones_bd, n_chunks):
    b, s, cols = p_rw.shape
    lora, d = w_up.shape
    rows = n_chunks * CHUNK
    n_tiles = s // rows
    n_chains = n_chunks * (d // LANES)
    row = lambda a: a.reshape(1, -1)
    const = lambda shape: pl.BlockSpec(shape, lambda i, j: (0,) * len(shape))
    return pl.pallas_call(
        functools.partial(_rwkv_body, lora=lora, n_chunks=n_chunks),
        grid=(b, n_tiles + 1),
        in_specs=[
            pl.BlockSpec((1, rows, cols), lambda i, j: (i, jnp.minimum(j, n_tiles - 1), 0)),
            const((1, cols)), const((lora, d)), const((1, d)), const((lora, d)), const((1, d)),
            const((1, d)), const((1, d)), const((1, d)), const((1, d)), const((1, d)), const((MXU_DIM, MXU_DIM)),
        ],
        out_specs=pl.BlockSpec((1, rows, d), lambda i, j: (i, jnp.maximum(j - 1, 0), 0)),
        out_shape=jax.ShapeDtypeStruct((b, s, d), BF16),
        scratch_shapes=[
            pltpu.VMEM((d // LANES, LANES, LANES), F32),
            pltpu.VMEM((1, cols), F32),
            pltpu.VMEM((n_chains, LANES, LANES), BF16),
            pltpu.VMEM((n_chains, LANES, LANES), F32),
            pltpu.VMEM((n_chains, CHUNK, LANES), BF16),
            pltpu.VMEM((n_chains, CHUNK, LANES), F32),
            pltpu.VMEM((rows, d), F32),
        ],
        compiler_params=pltpu.CompilerParams(
            dimension_semantics=("arbitrary", "arbitrary"), vmem_limit_bytes=VMEM_LIMIT),
        name="rwkv7",
    )(p_rw, row(mu), w_up, row(w0), a_up, row(a0), row(k_k), row(k_a), row(r_k), row(gn_g), row(gn_b), ones_bd)


def _attn_body(q_ref, k_ref, vt_ref, o_ref, *, heads_per_step, tq, online_max):
    qi = pl.program_id(2)
    tk = vt_ref.shape[-1]
    n_sub = tq // tk
    heads = range(heads_per_step)
    qs = [q_ref[0, hh] for hh in heads]

    items = [(u, hh) for u in range(n_sub) for hh in heads]
    zero = jnp.zeros((V_ROWS, tq), F32)

    def scores(jb, u, hh, diag):
        kb = k_ref[0, hh, pl.ds(pl.multiple_of((jb * n_sub + u) * tk, tk), tk), :]
        if not diag:
            return _dot_nt(kb, qs[hh])
        st = _dot_nt(kb, qs[hh][u * tk:, :])
        key = lax.broadcasted_iota(jnp.int32, st.shape, 0)
        qry = lax.broadcasted_iota(jnp.int32, st.shape, 1)
        return jnp.where(key <= qry, st, NEG_BIG)

    def values(sub_block, hh):
        return vt_ref[0, hh, sub_block, 0:V_ROWS, :]

    def add_cols(acc, part):
        off = acc.shape[1] - part.shape[1]
        return acc + part if off == 0 else jnp.concatenate([acc[:, :off], acc[:, off:] + part], axis=1)

    if online_max:
        def block(jb, carry, diag):
            new = list(carry)
            for u, hh in items:
                st = scores(jb, u, hh, diag)
                off = tq - st.shape[1]
                m, acc = new[hh]
                m_new = jnp.maximum(m[:, off:], jnp.max(st, axis=0, keepdims=True))
                m_new = m_new if off == 0 else jnp.concatenate([m[:, :off], m_new], axis=1)
                p = jnp.exp2(st - m_new[:, off:]).astype(BF16)
                new[hh] = (m_new, add_cols(acc * jnp.exp2(m - m_new), _dot(values(jb * n_sub + u, hh), p)))
            return tuple(new)

        init = tuple((jnp.full((1, tq), NEG_BIG, F32), zero) for _ in heads)
        carry = lax.fori_loop(0, qi, functools.partial(block, diag=False), init)
        accs = [c[1] for c in block(qi, carry, True)]
    else:
        def blocks(jbs, carry, diag):
            work = [(jb, u, hh) for jb in jbs for u, hh in items]
            tiles = [scores(jb, u, hh, diag) for jb, u, hh in work]
            probs = [jnp.exp2(st).astype(BF16) for st in tiles]
            new = list(carry)
            for (jb, u, hh), p in zip(work, probs):
                new[hh] = add_cols(new[hh], _dot(values(jb * n_sub + u, hh), p))
            return tuple(new)

        per_iter = ATTN_BLOCKS_PER_ITER
        carry = lax.fori_loop(
            0, qi // per_iter,
            lambda i, c: blocks([i * per_iter + o for o in range(per_iter)], c, False),
            tuple(zero for _ in heads))
        first_left = (qi // per_iter) * per_iter
        carry = lax.fori_loop(first_left, qi, lambda jb, c: blocks([jb], c, False), carry)
        accs = blocks([qi], carry, True)
    out_t = jnp.concatenate([acc[:HEAD_DIM] / acc[HEAD_DIM:HEAD_DIM + 1] for acc in accs], axis=0)
    o_ref[0] = out_t.T.astype(o_ref.dtype)


def _attention(qa, ka, vt, tq, online_max):
    b, h, s, _ = qa.shape
    hps = ATTN_HEADS_PER_STEP if h % ATTN_HEADS_PER_STEP == 0 else LANES // HEAD_DIM
    return pl.pallas_call(
        functools.partial(_attn_body, heads_per_step=hps, tq=tq, online_max=online_max),
        grid=(b, h // hps, s // tq),
        in_specs=[
            pl.BlockSpec((1, hps, tq, LANES), lambda i, g, j: (i, g, j, 0)),
            pl.BlockSpec((1, hps, s, LANES), lambda i, g, j: (i, g, 0, 0)),
            pl.BlockSpec((1, hps) + vt.shape[2:], lambda i, g, j: (i, g, 0, 0, 0)),
        ],
        out_specs=pl.BlockSpec((1, tq, hps * HEAD_DIM), lambda i, g, j: (i, j, g)),
        out_shape=jax.ShapeDtypeStruct((b, s, h * HEAD_DIM), BF16),
        compiler_params=pltpu.CompilerParams(
            dimension_semantics=("parallel", "parallel", "arbitrary"), vmem_limit_bytes=VMEM_LIMIT),
        name="fox_attention_online_max" if online_max else "fox_attention",
    )(qa, ka, vt)


def _outproj_body(x_ref, yrw_ref, yfox_ref, gate_ref, w_ref, fg_ref, o_ref, *, final):
    g = gate_ref[...].astype(F32)
    sg = g * _sigmoid(g)
    d_rw = yrw_ref.shape[1]
    y_rw = (yrw_ref[...].astype(F32) * sg[:, :d_rw]).astype(BF16)
    y_fox = (yfox_ref[...].astype(F32) * sg[:, d_rw:]).astype(BF16)
    z = x_ref[...] + _dot(y_rw, w_ref[0:d_rw, :]) + _dot(y_fox, w_ref[d_rw:, :])
    if final:
        z = (z * lax.rsqrt(jnp.mean(z * z, axis=-1, keepdims=True) + NORM_EPS)) * fg_ref[...]
    o_ref[...] = z


def _outproj(x2, y_rw, y_fox, gate, w_out, final_g, final, tm):
    n, d = x2.shape
    d_rw, d_fox = y_rw.shape[1], y_fox.shape[1]
    return pl.pallas_call(
        functools.partial(_outproj_body, final=final),
        grid=(n // tm,),
        in_specs=[
            pl.BlockSpec((tm, d), lambda i: (i, 0)),
            pl.BlockSpec((tm, d_rw), lambda i: (i, 0)),
            pl.BlockSpec((tm, d_fox), lambda i: (i, 0)),
            pl.BlockSpec((tm, d_rw + d_fox), lambda i: (i, 0)),
            pl.BlockSpec(w_out.shape, lambda i: (0, 0)),
            pl.BlockSpec((1, d), lambda i: (0, 0)),
        ],
        out_specs=pl.BlockSpec((tm, d), lambda i: (i, 0)),
        out_shape=jax.ShapeDtypeStruct((n, d), F32),
        compiler_params=pltpu.CompilerParams(
            dimension_semantics=("parallel",), vmem_limit_bytes=VMEM_LIMIT),
        name="outproj",
    )(x2, y_rw, y_fox, gate, w_out, final_g)


def _pick_tile(n, target):
    t = min(n, target)
    while n % t:
        t //= 2
    return t


def kernel(x, norm_g, w_in, rw_mu, rw_w_up, rw_w0, rw_a_up, rw_a0, rw_k_k, rw_k_a, rw_r_k, rw_gn_g, rw_gn_b,
           fox_q_g, fox_k_g, fox_b_f, w_out, final_g):
    b, s, d_model = x.shape
    depth = norm_g.shape[0]
    d_rw = rw_w0.shape[-1]
    lora = rw_w_up.shape[1]
    d_mix = w_out.shape[1]
    d_fox = d_mix - d_rw
    h_fox = fox_b_f.shape[-1]
    rw_cols = 3 * d_rw + 2 * lora
    fox_cols = 3 * d_fox + h_fox
    assert d_rw == d_fox and s % CHUNK == 0 and h_fox <= LANES

    head = lax.broadcasted_iota(jnp.int32, (MXU_DIM, MXU_DIM), 0) // HEAD_DIM
    ones_bd = (head == head.T).astype(BF16)

    tm_out = _pick_tile(b * s, 512)
    tm_in = _pick_tile(s, 256)
    tq = _pick_tile(s, 512)

    x2 = x.reshape(b * s, d_model)
    for l in range(depth):
        w = w_in[l]
        w_fl = jnp.pad(w[:, rw_cols + 3 * d_fox:rw_cols + fox_cols], ((0, 0), (0, LANES - h_fox)))
        w_cat = jnp.concatenate(
            [w[:, :rw_cols], w[:, rw_cols:rw_cols + 3 * d_fox], w[:, rw_cols + fox_cols:], w_fl], axis=1
        ).astype(BF16)
        tile_heads = lambda g: jnp.tile(g, d_fox // HEAD_DIM).reshape(1, -1)
        b_f = jnp.pad(fox_b_f[l], (0, LANES - h_fox)).reshape(1, -1)
        q_g, k_g = tile_heads(fox_q_g[l]), tile_heads(fox_k_g[l])
        p_rw, gate, qa, ka, va = _inproj_fox(x2.reshape(b, s, d_model), norm_g[l].reshape(1, -1), w_cat,
                                             q_g, k_g, b_f, ones_bd, rw_cols, d_mix, tm_in)

        y_rw = _rwkv(p_rw, rw_mu[l], rw_w_up[l], rw_w0[l], rw_a_up[l], rw_a0[l],
                     rw_k_k[l], rw_k_a[l], rw_r_k[l].reshape(-1), rw_gn_g[l], rw_gn_b[l], ones_bd,
                     RWKV_CHUNKS_PER_STEP if s % (RWKV_CHUNKS_PER_STEP * CHUNK) == 0 else 1)

        y_fox = lax.cond(_score_bound(q_g, k_g)[0, 0] <= SAFE_SCORE_BOUND,
                         functools.partial(_attention, tq=tq, online_max=False),
                         functools.partial(_attention, tq=tq, online_max=True), qa, ka, va)

        x2 = _outproj(x2, y_rw.reshape(b * s, d_rw), y_fox.reshape(b * s, d_fox), gate.reshape(b * s, d_mix),
                      w_out[l].astype(BF16), final_g.reshape(1, -1), l == depth - 1, tm_out)
    return x2.reshape(b, s, d_model)
```

```python
import functools
import math

import jax
import jax.numpy as jnp
from jax import lax
from jax.experimental import pallas as pl
from jax.experimental.pallas import tpu as pltpu

F32 = jnp.float32
BF16 = jnp.bfloat16

HEAD_DIM = 64
NORM_EPS = 1e-6
GN_EPS = 64e-5
CHUNK = 64
RWKV_CHUNKS_PER_STEP = 4
LANES = 128
MXU_DIM = 256
NEG_BIG = -1e30
ATTN_HEADS_PER_STEP = 4
ATTN_KEY_SUB = 256
ATTN_BLOCKS_PER_ITER = 2
V_ROWS = 80
SAFE_SCORE_BOUND = 40.0
VMEM_LIMIT = 56 * 1024 * 1024


def _dot(a, b):
    return jnp.dot(a, b, preferred_element_type=F32)


def _dot_nt(a, b):
    return lax.dot_general(a, b, (((1,), (1,)), ((), ())), preferred_element_type=F32)


def _dot_tn(a, b):
    return lax.dot_general(a, b, (((0,), (0,)), ((), ())), preferred_element_type=F32)


def _sigmoid(x):
    return 1.0 / (1.0 + jnp.exp(-x))


def _bf(x):
    return x.astype(BF16)


def _split2(x):
    hi = x.astype(BF16)
    return hi, (x - hi.astype(F32)).astype(BF16)


def _split3(x):
    hi = x.astype(BF16)
    r1 = x - hi.astype(F32)
    mid = r1.astype(BF16)
    lo = (r1 - mid.astype(F32)).astype(BF16)
    return hi.astype(F32), mid.astype(F32), lo.astype(F32)


def _stack_x3_rhs(b):
    bh, bl = _split2(b)
    return jnp.concatenate([bh, bl, bh], axis=0)


def _dot_x3(a, b_stacked):
    ah, al = _split2(a)
    return _dot(jnp.concatenate([ah, ah, al], axis=1), b_stacked)


def _dot_pieces(x, ones_bf16, pieces):
    out = None
    for _ in range(pieces):
        hi = x.astype(BF16)
        term = _dot(hi, ones_bf16)
        out = term if out is None else out + term
        x = x - hi.astype(F32)
    return out


def _cumsum_rows(tri_bf16, x):
    out = None
    for _ in range(3):
        hi = x.astype(BF16)
        term = _dot(tri_bf16, hi)
        out = term if out is None else out + term
        x = x - hi.astype(F32)
    return out


def _headsum(x, ones_bd, pieces=2):
    width = x.shape[1]
    return jnp.concatenate([_dot_pieces(x[:, j:j + MXU_DIM], ones_bd, pieces) for j in range(0, width, MXU_DIM)],
                           axis=1)


def _score_bound(q_g, k_g):
    amax = lambda g: jnp.max(jnp.abs(g), axis=-1, keepdims=True)
    return (HEAD_DIM ** 0.5) * amax(q_g) * amax(k_g)


def _inproj_fox_body(x_ref, g_ref, w_ref, mu_ref, qg_ref, kg_ref, bf_ref, ones_ref,
                     rw_ref, gate_ref, qa_ref, ka_ref, va_ref, carry_ref, shift_ref, *, n_heads):
    @pl.when(pl.program_id(1) == 0)
    def _():
        carry_ref[...] = jnp.zeros_like(carry_ref)
        shift_ref[...] = jnp.zeros_like(shift_ref)

    x = x_ref[0]
    tm = x.shape[0]
    d = n_heads * HEAD_DIM
    rw_cols, gate_cols = rw_ref.shape[-1], gate_ref.shape[-1]
    off_qkv, off_gate, off_fl = rw_cols, rw_cols + 3 * d, rw_cols + 3 * d + gate_cols
    y = x * lax.rsqrt(jnp.mean(x * x, axis=-1, keepdims=True) + NORM_EPS)
    h = (y * g_ref[...]).astype(BF16)

    t = _dot(h, w_ref[:, off_qkv:off_qkv + 3 * d])
    fl = _dot(h, w_ref[:, off_fl:off_fl + LANES])

    tiles = [(rw_ref, c0, c0) for c0 in range(0, rw_cols, MXU_DIM)]
    tiles += [(gate_ref, c0, off_gate + c0) for c0 in range(0, gate_cols, MXU_DIM)]

    row8 = lax.broadcasted_iota(jnp.int32, (8, MXU_DIM), 0)

    def project(tile):
        ref, dst, src = tile
        width = min(MXU_DIM, ref.shape[-1] - dst)
        out = _dot(h, w_ref[:, src:src + width])
        if ref is rw_ref:
            cols = slice(dst, dst + width)
            before = shift_ref[:, cols]
            shift_ref[:, cols] = out[tm - 1:tm, :]
            rolled = pltpu.roll(out, 1, axis=0)
            prev = jnp.concatenate([jnp.where(row8[:, :width] == 0, before, rolled[:8]), rolled[8:]], axis=0)
            out = out + (prev - out) * mu_ref[:, cols]
        ref[0, :, dst:dst + width] = out.astype(ref.dtype)

    n_lead = max(len(tiles) - n_heads, 0)
    for tile in tiles[:n_lead]:
        project(tile)

    ones_bd = ones_ref[...]
    inv_n = 1.0 / HEAD_DIM
    log2e = math.log2(math.e)

    def headnorm(z, g):
        ms = _headsum(z * z, ones_bd) * inv_n
        return (z * lax.rsqrt(ms + NORM_EPS)) * g

    q = headnorm(t[:, 0:d], qg_ref[...]) * (HEAD_DIM ** -0.5 * log2e)
    k = headnorm(t[:, d:2 * d], kg_ref[...])
    v = t[:, 2 * d:3 * d]
    neg_bound = -log2e * _score_bound(qg_ref[...], kg_ref[...])

    z = fl + bf_ref[...]
    log_f = jnp.minimum(z, 0.0) - jnp.log1p(jnp.exp(-jnp.abs(z)))
    row = lax.broadcasted_iota(jnp.int32, (tm, tm), 0)
    col = lax.broadcasted_iota(jnp.int32, (tm, tm), 1)
    cs = _cumsum_rows(jnp.where(row >= col, 1.0, 0.0).astype(BF16), log_f) + carry_ref[...]
    carry_ref[...] = cs[tm - 1:tm, :]
    c_hi, c_mid, c_lo = _split3(cs * log2e)

    lane = lax.broadcasted_iota(jnp.int32, (tm, LANES), 1)
    is_data = lane < HEAD_DIM
    in_range = lambda lo, hi: ((lane >= HEAD_DIM + lo) & (lane < HEAD_DIM + hi)).astype(F32)
    q_base = in_range(3, 7)
    k_base = jnp.where(lane == HEAD_DIM + 6, neg_bound, in_range(0, 3))
    one_col = in_range(0, 1)

    def place(pieces, base):
        out = base
        for off, piece in pieces:
            out = jnp.where(lane == off, piece, out)
        return out

    for h_idx in range(n_heads):
        if n_lead + h_idx < len(tiles):
            project(tiles[n_lead + h_idx])
        pair = slice((h_idx // 2) * LANES, (h_idx // 2 + 1) * LANES)
        shift = lambda z: z[:, pair] if h_idx % 2 == 0 else pltpu.roll(z[:, pair], HEAD_DIM, axis=1)
        ch = [jnp.broadcast_to(piece[:, h_idx:h_idx + 1], (tm, LANES)) for piece in (c_hi, c_mid, c_lo)]
        q_extra = place([(HEAD_DIM + i, ch[i]) for i in range(3)], q_base)
        k_extra = place([(HEAD_DIM + 3 + i, -ch[i]) for i in range(3)], k_base)
        qa_ref[0, h_idx] = jnp.where(is_data, shift(q), q_extra).astype(BF16)
        ka_ref[0, h_idx] = jnp.where(is_data, shift(k), k_extra).astype(BF16)
        v_t = jnp.where(is_data, shift(v), one_col).T.astype(BF16)
        for sub in range(tm // ATTN_KEY_SUB):
            va_ref[0, h_idx, sub] = v_t[:, sub * ATTN_KEY_SUB:(sub + 1) * ATTN_KEY_SUB]
    for tile in tiles[n_lead + n_heads:]:
        project(tile)


def _inproj_fox(x, g, w_cat, mu, q_g, k_g, b_f, ones_bd, rw_cols, gate_cols, tm):
    b, s, d_model = x.shape
    d = q_g.shape[-1]
    n_heads = d // HEAD_DIM
    const = lambda shape: pl.BlockSpec(shape, lambda i, j: (0,) * len(shape))
    rows = lambda width: pl.BlockSpec((1, tm, width), lambda i, j: (i, j, 0))
    head_spec = pl.BlockSpec((1, n_heads, tm, LANES), lambda i, j: (i, 0, j, 0))
    head_shape = jax.ShapeDtypeStruct((b, n_heads, s, LANES), BF16)
    return pl.pallas_call(
        functools.partial(_inproj_fox_body, n_heads=n_heads),
        grid=(b, s // tm),
        in_specs=[
            rows(d_model), const((1, d_model)),
            pl.BlockSpec(w_cat.shape, lambda i, j: (0, 0), pipeline_mode=pl.Buffered(1)),
            const((1, rw_cols)),
            const((1, d)), const((1, d)), const((1, LANES)), const((MXU_DIM, MXU_DIM)),
        ],
        out_specs=[rows(rw_cols), rows(gate_cols), head_spec, head_spec,
                   pl.BlockSpec((1, n_heads, tm // ATTN_KEY_SUB, LANES, ATTN_KEY_SUB),
                                lambda i, j: (i, 0, j, 0, 0))],
        out_shape=[jax.ShapeDtypeStruct((b, s, rw_cols), F32), jax.ShapeDtypeStruct((b, s, gate_cols), BF16),
                   head_shape, head_shape,
                   jax.ShapeDtypeStruct((b, n_heads, s // ATTN_KEY_SUB, LANES, ATTN_KEY_SUB), BF16)],
        scratch_shapes=[pltpu.VMEM((1, LANES), F32),
                        pltpu.VMEM((1, rw_cols), F32)],
        compiler_params=pltpu.CompilerParams(
            dimension_semantics=("arbitrary", "arbitrary"), vmem_limit_bytes=VMEM_LIMIT),
        name="inproj_fox",
    )(x, g, w_cat, mu, q_g, k_g, b_f, ones_bd)


def _rwkv_body(xs_ref, wup_ref, w0_ref, aup_ref, a0_ref, kk_ref, ka_ref, rk_ref, gng_ref, gnb_ref, ones_ref,
               y_ref, state_ref, m_ref, n_ref, rw_ref, yp_ref, bonus_ref, *, lora, n_chunks):
    @pl.when(pl.program_id(1) == 0)
    def _():
        state_ref[...] = jnp.zeros_like(state_ref)
        for ref in (m_ref, n_ref, rw_ref, yp_ref, bonus_ref):
            ref[...] = jnp.zeros_like(ref)

    L = CHUNK
    P = LANES
    d = y_ref.shape[-1]
    n_pairs = d // P
    ones_bd = ones_ref[...]
    chains = [(c, j) for c in range(n_chunks) for j in range(n_pairs)]
    row_l = lax.broadcasted_iota(jnp.int32, (L, L), 0)
    col_l = lax.broadcasted_iota(jnp.int32, (L, L), 1)
    tri = jnp.where(row_l >= col_l, 1.0, 0.0).astype(BF16)
    wup_x3 = _stack_x3_rhs(wup_ref[...])
    aup_x3 = _stack_x3_rhs(aup_ref[...])

    def state_pass(c):
        out = []
        for j in range(n_pairs):
            idx = c * n_pairs + j
            s0 = _bf(state_ref[j])
            out.append(_dot(rw_ref[idx], s0) + yp_ref[idx])
            state_ref[j] = _dot(m_ref[idx], s0) + n_ref[idx]
        return jnp.concatenate(out, axis=1)

    def prep(c):
        xs = xs_ref[0, c * L:(c + 1) * L, :]
        r = xs[:, 0:d]
        k = xs[:, d:2 * d]
        v = xs[:, 2 * d:3 * d]
        wl = xs[:, 3 * d:3 * d + lora]
        al = xs[:, 3 * d + lora:3 * d + 2 * lora]
        w_raw = w0_ref[...] + _dot_x3(jnp.tanh(wl), wup_x3)
        logw = (-math.exp(-0.5)) * _sigmoid(w_raw)
        a = _sigmoid(a0_ref[...] + _dot_x3(al, aup_x3))
        kkr = k * kk_ref[...]
        kk = kkr * lax.rsqrt(jnp.maximum(_headsum(kkr * kkr, ones_bd, pieces=1), 1e-24))
        k2 = k * (1.0 + (a - 1.0) * ka_ref[...])
        bonus = _headsum(r * k2 * rk_ref[...], ones_bd, pieces=1) * v
        cum = _cumsum_rows(tri, logw)
        cum_end = cum[L - 1:L, :]
        e_neg = jnp.exp(-cum)
        e_end = jnp.exp(cum_end - cum)
        kka = kk * a
        return dict(
            v=v, bonus=bonus,
            g_end=jnp.exp(cum_end),
            r_t=r * jnp.exp(cum),
            a_t=-kk * jnp.exp(cum - logw),
            b_t=kka * e_neg, k_t=k2 * e_neg,
            b_e=kka * e_end, k_e=k2 * e_end)

    ys, cur = [], []
    for c in range(n_chunks):
        ys.append(state_pass(c))
        cur.append(prep(c))

    y = jnp.concatenate(ys, axis=0)
    inv_n = 1.0 / HEAD_DIM
    mean = _headsum(y, ones_bd) * inv_n
    yc = y - mean
    var = _headsum(yc * yc, ones_bd) * inv_n
    y_ref[0] = ((yc * lax.rsqrt(var + GN_EPS)) * gng_ref[...] + gnb_ref[...] + bonus_ref[...]).astype(y_ref.dtype)
    bonus_ref[...] = jnp.concatenate([t["bonus"] for t in cur], axis=0)

    pair = lambda name, c, j: cur[c][name][:, j * P:(j + 1) * P]
    lane = lax.broadcasted_iota(jnp.int32, (L, P), 1)
    first = lane < HEAD_DIM
    stack = lambda z: jnp.concatenate([jnp.where(first, z, 0.0), jnp.where(first, 0.0, z)], axis=0)

    row = lax.broadcasted_iota(jnp.int32, (P, P), 0)
    col = lax.broadcasted_iota(jnp.int32, (P, P), 1)
    same = lambda s: (row >> s) == (col >> s)
    strict = same(6) & (row > col)
    incl = same(6) & (row >= col)
    eye = row == col

    amat = [_dot_nt(_bf(jnp.concatenate([stack(pair("a_t", c, j)), stack(pair("r_t", c, j))], axis=0)),
                    _bf(jnp.concatenate([pair("b_t", c, j)] * 2 + [pair("k_t", c, j)] * 2, axis=0)))
            for c, j in chains]
    n_st = [jnp.where(strict, m[:P, :P], 0.0) for m in amat]
    a_ak = [_bf(jnp.where(strict, m[:P, P:], 0.0)) for m in amat]
    a_r = [_bf(jnp.concatenate([jnp.where(incl, m[P:, :P], 0.0), jnp.where(incl, m[P:, P:], 0.0)], axis=1))
           for m in amat]

    n0_f = [jnp.where(same(3), n, 0.0) for n in n_st]
    n0 = [_bf(n) for n in n0_f]
    n2 = [_bf(_dot(n, n)) for n in n0]
    n4 = [_bf(_dot(n, n)) for n in n2]
    t = [jnp.where(eye, 1.0, n) for n in n0_f]
    t = [x + _dot(_bf(x), m) for x, m in zip(t, n2)]
    t = [x + _dot(_bf(x), m) for x, m in zip(t, n4)]
    for s in (3, 4, 5):
        off_mask = same(s + 1) & jnp.logical_not(same(s))
        n_off = [_bf(jnp.where(off_mask, n, 0.0)) for n in n_st]
        tb = [_bf(x) for x in t]
        half = [_bf(_dot(x, m)) for x, m in zip(tb, n_off)]
        t = [x + _dot(h, xb) for x, h, xb in zip(t, half, tb)]
    tb = [_bf(x) for x in t]

    v_st = [_bf(stack(pair("v", c, j))) for c, j in chains]
    x_st = [_dot(m, vs) for m, vs in zip(a_ak, v_st)]
    wu = [_dot(x, jnp.concatenate([_bf(stack(pair("a_t", c, j))), _bf(xs_)], axis=1))
          for x, xs_, (c, j) in zip(tb, x_st, chains)]
    zeros_b = jnp.zeros((P, P), BF16)
    rhs2 = [jnp.concatenate([_bf(w), jnp.concatenate([zeros_b, vs], axis=1)], axis=0)
            for w, vs in zip(wu, v_st)]
    top = [_dot(m, rh) for m, rh in zip(a_r, rhs2)]
    bot = [_dot_tn(_bf(jnp.concatenate([stack(pair("b_e", c, j)), stack(pair("k_e", c, j))], axis=0)), rh)
           for rh, (c, j) in zip(rhs2, chains)]

    fold = lambda z: z[:L] + z[L:]
    for idx, (c, j) in enumerate(chains):
        rw_ref[idx] = _bf(pair("r_t", c, j) + fold(top[idx][:, :P]))
        yp_ref[idx] = fold(top[idx][:, P:])
        m_ref[idx] = _bf(jnp.where(eye, pair("g_end", c, j), 0.0) + bot[idx][:, :P])
        n_ref[idx] = bot[idx][:, P:]


def _rwkv(xs_rw, w_up, w0, a_up, a0, k_k, k_a, r_k, gn_g, gn_b, ones_bd, n_chunks):
    b, s, cols = xs_rw.shape
    lora, d = w_up.shape
    rows = n_chunks * CHUNK
    n_tiles = s // rows
    n_chains = n_chunks * (d // LANES)
    row = lambda a: a.reshape(1, -1)
    const = lambda shape: pl.BlockSpec(shape, lambda i, j: (0,) * len(shape))
    return pl.pallas_call(
        functools.partial(_rwkv_body, lora=lora, n_chunks=n_chunks),
        grid=(b, n_tiles + 1),
        in_specs=[
            pl.BlockSpec((1, rows, cols), lambda i, j: (i, jnp.minimum(j, n_tiles - 1), 0)),
            const((lora, d)), const((1, d)), const((lora, d)), const((1, d)),
            const((1, d)), const((1, d)), const((1, d)), const((1, d)), const((1, d)), const((MXU_DIM, MXU_DIM)),
        ],
        out_specs=pl.BlockSpec((1, rows, d), lambda i, j: (i, jnp.maximum(j - 1, 0), 0)),
        out_shape=jax.ShapeDtypeStruct((b, s, d), BF16),
        scratch_shapes=[
            pltpu.VMEM((d // LANES, LANES, LANES), F32),
            pltpu.VMEM((n_chains, LANES, LANES), BF16),
            pltpu.VMEM((n_chains, LANES, LANES), F32),
            pltpu.VMEM((n_chains, CHUNK, LANES), BF16),
            pltpu.VMEM((n_chains, CHUNK, LANES), F32),
            pltpu.VMEM((rows, d), F32),
        ],
        compiler_params=pltpu.CompilerParams(
            dimension_semantics=("arbitrary", "arbitrary"), vmem_limit_bytes=VMEM_LIMIT),
        name="rwkv7",
    )(xs_rw, w_up, row(w0), a_up, row(a0), row(k_k), row(k_a), row(r_k), row(gn_g), row(gn_b), ones_bd)


def _attn_body(q_ref, k_ref, vt_ref, o_ref, *, heads_per_step, tq, online_max):
    qi = pl.program_id(2)
    tk = vt_ref.shape[-1]
    n_sub = tq // tk
    heads = range(heads_per_step)
    qs = [q_ref[0, hh] for hh in heads]

    items = [(u, hh) for u in range(n_sub) for hh in heads]
    zero = jnp.zeros((V_ROWS, tq), F32)

    def scores(jb, u, hh, diag):
        kb = k_ref[0, hh, pl.ds(pl.multiple_of((jb * n_sub + u) * tk, tk), tk), :]
        if not diag:
            return _dot_nt(kb, qs[hh])
        st = _dot_nt(kb, qs[hh][u * tk:, :])
        key = lax.broadcasted_iota(jnp.int32, st.shape, 0)
        qry = lax.broadcasted_iota(jnp.int32, st.shape, 1)
        return jnp.where(key <= qry, st, NEG_BIG)

    def values(sub_block, hh):
        return vt_ref[0, hh, sub_block, 0:V_ROWS, :]

    def add_cols(acc, part):
        off = acc.shape[1] - part.shape[1]
        return acc + part if off == 0 else jnp.concatenate([acc[:, :off], acc[:, off:] + part], axis=1)

    if online_max:
        def block(jb, carry, diag):
            new = list(carry)
            for u, hh in items:
                st = scores(jb, u, hh, diag)
                off = tq - st.shape[1]
                m, acc = new[hh]
                m_new = jnp.maximum(m[:, off:], jnp.max(st, axis=0, keepdims=True))
                m_new = m_new if off == 0 else jnp.concatenate([m[:, :off], m_new], axis=1)
                p = jnp.exp2(st - m_new[:, off:]).astype(BF16)
                new[hh] = (m_new, add_cols(acc * jnp.exp2(m - m_new), _dot(values(jb * n_sub + u, hh), p)))
            return tuple(new)

        init = tuple((jnp.full((1, tq), NEG_BIG, F32), zero) for _ in heads)
        carry = lax.fori_loop(0, qi, functools.partial(block, diag=False), init)
        accs = [c[1] for c in block(qi, carry, True)]
    else:
        def blocks(jbs, carry, diag):
            work = [(jb, u, hh) for jb in jbs for u, hh in items]
            tiles = [scores(jb, u, hh, diag) for jb, u, hh in work]
            probs = [jnp.exp2(st).astype(BF16) for st in tiles]
            new = list(carry)
            for (jb, u, hh), p in zip(work, probs):
                new[hh] = add_cols(new[hh], _dot(values(jb * n_sub + u, hh), p))
            return tuple(new)

        per_iter = ATTN_BLOCKS_PER_ITER
        carry = lax.fori_loop(
            0, qi // per_iter,
            lambda i, c: blocks([i * per_iter + o for o in range(per_iter)], c, False),
            tuple(zero for _ in heads))
        first_left = (qi // per_iter) * per_iter
        carry = lax.fori_loop(first_left, qi, lambda jb, c: blocks([jb], c, False), carry)
        accs = blocks([qi], carry, True)
    out_t = jnp.concatenate([acc[:HEAD_DIM] / acc[HEAD_DIM:HEAD_DIM + 1] for acc in accs], axis=0)
    o_ref[0] = out_t.astype(o_ref.dtype).T


def _attention(qa, ka, vt, tq, online_max):
    b, h, s, _ = qa.shape
    hps = ATTN_HEADS_PER_STEP if h % ATTN_HEADS_PER_STEP == 0 else LANES // HEAD_DIM
    return pl.pallas_call(
        functools.partial(_attn_body, heads_per_step=hps, tq=tq, online_max=online_max),
        grid=(b, h // hps, s // tq),
        in_specs=[
            pl.BlockSpec((1, hps, tq, LANES), lambda i, g, j: (i, g, j, 0)),
            pl.BlockSpec((1, hps, s, LANES), lambda i, g, j: (i, g, 0, 0)),
            pl.BlockSpec((1, hps) + vt.shape[2:], lambda i, g, j: (i, g, 0, 0, 0)),
        ],
        out_specs=pl.BlockSpec((1, tq, hps * HEAD_DIM), lambda i, g, j: (i, j, g)),
        out_shape=jax.ShapeDtypeStruct((b, s, h * HEAD_DIM), BF16),
        compiler_params=pltpu.CompilerParams(
            dimension_semantics=("parallel", "parallel", "arbitrary"), vmem_limit_bytes=VMEM_LIMIT),
        name="fox_attention_online_max" if online_max else "fox_attention",
    )(qa, ka, vt)


def _outproj_body(x_ref, yrw_ref, yfox_ref, gate_ref, w_ref, fg_ref, o_ref, *, final):
    g = gate_ref[...].astype(F32)
    sg = g * _sigmoid(g)
    d_rw = yrw_ref.shape[1]
    y_rw = (yrw_ref[...].astype(F32) * sg[:, :d_rw]).astype(BF16)
    y_fox = (yfox_ref[...].astype(F32) * sg[:, d_rw:]).astype(BF16)
    z = x_ref[...] + _dot(y_rw, w_ref[0:d_rw, :]) + _dot(y_fox, w_ref[d_rw:, :])
    if final:
        z = (z * lax.rsqrt(jnp.mean(z * z, axis=-1, keepdims=True) + NORM_EPS)) * fg_ref[...]
    o_ref[...] = z


def _outproj(x2, y_rw, y_fox, gate, w_out, final_g, final, tm):
    n, d = x2.shape
    d_rw, d_fox = y_rw.shape[1], y_fox.shape[1]
    return pl.pallas_call(
        functools.partial(_outproj_body, final=final),
        grid=(n // tm,),
        in_specs=[
            pl.BlockSpec((tm, d), lambda i: (i, 0)),
            pl.BlockSpec((tm, d_rw), lambda i: (i, 0)),
            pl.BlockSpec((tm, d_fox), lambda i: (i, 0)),
            pl.BlockSpec((tm, d_rw + d_fox), lambda i: (i, 0)),
            pl.BlockSpec(w_out.shape, lambda i: (0, 0)),
            pl.BlockSpec((1, d), lambda i: (0, 0)),
        ],
        out_specs=pl.BlockSpec((tm, d), lambda i: (i, 0)),
        out_shape=jax.ShapeDtypeStruct((n, d), F32),
        compiler_params=pltpu.CompilerParams(
            dimension_semantics=("parallel",), vmem_limit_bytes=VMEM_LIMIT),
        name="outproj",
    )(x2, y_rw, y_fox, gate, w_out, final_g)


def _pick_tile(n, target):
    t = min(n, target)
    while n % t:
        t //= 2
    return t


def kernel(x, norm_g, w_in, rw_mu, rw_w_up, rw_w0, rw_a_up, rw_a0, rw_k_k, rw_k_a, rw_r_k, rw_gn_g, rw_gn_b,
           fox_q_g, fox_k_g, fox_b_f, w_out, final_g):
    b, s, d_model = x.shape
    depth = norm_g.shape[0]
    d_rw = rw_w0.shape[-1]
    lora = rw_w_up.shape[1]
    d_mix = w_out.shape[1]
    d_fox = d_mix - d_rw
    h_fox = fox_b_f.shape[-1]
    rw_cols = 3 * d_rw + 2 * lora
    fox_cols = 3 * d_fox + h_fox
    assert d_rw == d_fox and s % CHUNK == 0 and h_fox <= LANES

    head = lax.broadcasted_iota(jnp.int32, (MXU_DIM, MXU_DIM), 0) // HEAD_DIM
    ones_bd = (head == head.T).astype(BF16)

    tm_out = _pick_tile(b * s, 1024)
    tm_in = _pick_tile(s, 512)
    tq = _pick_tile(s, 512)

    x2 = x.reshape(b * s, d_model)
    for l in range(depth):
        w = w_in[l]
        w_fl = jnp.pad(w[:, rw_cols + 3 * d_fox:rw_cols + fox_cols], ((0, 0), (0, LANES - h_fox)))
        w_cat = jnp.concatenate(
            [w[:, :rw_cols], w[:, rw_cols:rw_cols + 3 * d_fox], w[:, rw_cols + fox_cols:], w_fl], axis=1
        ).astype(BF16)
        tile_heads = lambda g: jnp.tile(g, d_fox // HEAD_DIM).reshape(1, -1)
        b_f = jnp.pad(fox_b_f[l], (0, LANES - h_fox)).reshape(1, -1)
        q_g, k_g = tile_heads(fox_q_g[l]), tile_heads(fox_k_g[l])
        xs_rw, gate, qa, ka, va = _inproj_fox(x2.reshape(b, s, d_model), norm_g[l].reshape(1, -1), w_cat,
                                              rw_mu[l].reshape(1, -1), q_g, k_g, b_f, ones_bd, rw_cols, d_mix,
                                              tm_in)

        y_rw = _rwkv(xs_rw, rw_w_up[l], rw_w0[l], rw_a_up[l], rw_a0[l],
                     rw_k_k[l], rw_k_a[l], rw_r_k[l].reshape(-1), rw_gn_g[l], rw_gn_b[l], ones_bd,
                     RWKV_CHUNKS_PER_STEP if s % (RWKV_CHUNKS_PER_STEP * CHUNK) == 0 else 1)

        y_fox = lax.cond(_score_bound(q_g, k_g)[0, 0] <= SAFE_SCORE_BOUND,
                         functools.partial(_attention, tq=tq, online_max=False),
                         functools.partial(_attention, tq=tq, online_max=True), qa, ka, va)

        x2 = _outproj(x2, y_rw.reshape(b * s, d_rw), y_fox.reshape(b * s, d_fox), gate.reshape(b * s, d_mix),
                      w_out[l].astype(BF16), final_g.reshape(1, -1), l == depth - 1, tm_out)
    return x2.reshape(b, s, d_model)
```

```python
import functools
import math

import jax
import jax.numpy as jnp
from jax import lax
from jax.experimental import pallas as pl
from jax.experimental.pallas import tpu as pltpu

F32 = jnp.float32
BF16 = jnp.bfloat16

HEAD_DIM = 64
NORM_EPS = 1e-6
GN_EPS = 64e-5
CHUNK = 64
LOG2_CHUNK = CHUNK.bit_length() - 1
LOG2_BASE = 3
RWKV_CHUNKS_PER_STEP = 4
LANES = 128
MXU_DIM = 256
NEG_BIG = -1e30
ATTN_HEADS_PER_STEP = 4
ATTN_KEY_SUB = 256
ATTN_BLOCKS_PER_ITER = 2
V_ROWS = 80
SAFE_SCORE_BOUND = 40.0
VMEM_LIMIT = 56 * 1024 * 1024


def _dot(a, b):
    return jnp.dot(a, b, preferred_element_type=F32)


def _dot_nt(a, b):
    return lax.dot_general(a, b, (((1,), (1,)), ((), ())), preferred_element_type=F32)


def _dot_tn(a, b):
    return lax.dot_general(a, b, (((0,), (0,)), ((), ())), preferred_element_type=F32)


def _sigmoid(x):
    return 1.0 / (1.0 + jnp.exp(-x))


def _bf(x):
    return x.astype(BF16)


def _split2(x):
    hi = x.astype(BF16)
    return hi, (x - hi.astype(F32)).astype(BF16)


def _split3(x):
    hi = x.astype(BF16)
    r1 = x - hi.astype(F32)
    mid = r1.astype(BF16)
    lo = (r1 - mid.astype(F32)).astype(BF16)
    return hi.astype(F32), mid.astype(F32), lo.astype(F32)


def _stack_x3_rhs(b):
    bh, bl = _split2(b)
    return jnp.concatenate([bh, bl, bh], axis=0)


def _dot_x3(a, b_stacked):
    ah, al = _split2(a)
    return _dot(jnp.concatenate([ah, ah, al], axis=1), b_stacked)


def _dot_pieces(x, ones_bf16, pieces):
    out = None
    for _ in range(pieces):
        hi = x.astype(BF16)
        term = _dot(hi, ones_bf16)
        out = term if out is None else out + term
        x = x - hi.astype(F32)
    return out


def _cumsum_rows(tri_bf16, x):
    out = None
    for _ in range(3):
        hi = x.astype(BF16)
        term = _dot(tri_bf16, hi)
        out = term if out is None else out + term
        x = x - hi.astype(F32)
    return out


def _headsum(x, ones_bd, pieces=2):
    width = x.shape[1]
    return jnp.concatenate([_dot_pieces(x[:, j:j + MXU_DIM], ones_bd, pieces) for j in range(0, width, MXU_DIM)],
                           axis=1)


def _score_bound(q_g, k_g):
    amax = lambda g: jnp.max(jnp.abs(g), axis=-1, keepdims=True)
    return (HEAD_DIM ** 0.5) * amax(q_g) * amax(k_g)


def _inproj_fox_body(x_ref, g_ref, w_ref, mu_ref, qg_ref, kg_ref, bf_ref, ones_ref,
                     rw_ref, gate_ref, qa_ref, ka_ref, va_ref, carry_ref, shift_ref, *, n_heads):
    @pl.when(pl.program_id(1) == 0)
    def _():
        carry_ref[...] = jnp.zeros_like(carry_ref)
        shift_ref[...] = jnp.zeros_like(shift_ref)

    x = x_ref[0]
    tm = x.shape[0]
    d = n_heads * HEAD_DIM
    rw_cols, gate_cols = rw_ref.shape[-1], gate_ref.shape[-1]
    off_qkv, off_gate, off_fl = rw_cols, rw_cols + 3 * d, rw_cols + 3 * d + gate_cols
    y = x * lax.rsqrt(jnp.mean(x * x, axis=-1, keepdims=True) + NORM_EPS)
    h = (y * g_ref[...]).astype(BF16)

    t = _dot(h, w_ref[:, off_qkv:off_qkv + 3 * d])
    fl = _dot(h, w_ref[:, off_fl:off_fl + LANES])

    tiles = [(rw_ref, c0, c0) for c0 in range(0, rw_cols, MXU_DIM)]
    tiles += [(gate_ref, c0, off_gate + c0) for c0 in range(0, gate_cols, MXU_DIM)]

    row8 = lax.broadcasted_iota(jnp.int32, (8, MXU_DIM), 0)

    def project(tile):
        ref, dst, src = tile
        width = min(MXU_DIM, ref.shape[-1] - dst)
        out = _dot(h, w_ref[:, src:src + width])
        if ref is rw_ref:
            cols = slice(dst, dst + width)
            before = shift_ref[:, cols]
            shift_ref[:, cols] = out[tm - 1:tm, :]
            rolled = pltpu.roll(out, 1, axis=0)
            prev = jnp.concatenate([jnp.where(row8[:, :width] == 0, before, rolled[:8]), rolled[8:]], axis=0)
            out = out + (prev - out) * mu_ref[:, cols]
        ref[0, :, dst:dst + width] = out.astype(ref.dtype)

    n_lead = max(len(tiles) - n_heads, 0)
    for tile in tiles[:n_lead]:
        project(tile)

    ones_bd = ones_ref[...]
    inv_n = 1.0 / HEAD_DIM
    log2e = math.log2(math.e)

    def headnorm(z, g):
        ms = _headsum(z * z, ones_bd) * inv_n
        return (z * lax.rsqrt(ms + NORM_EPS)) * g

    q = headnorm(t[:, 0:d], qg_ref[...]) * (HEAD_DIM ** -0.5 * log2e)
    k = headnorm(t[:, d:2 * d], kg_ref[...])
    v = t[:, 2 * d:3 * d]
    neg_bound = -log2e * _score_bound(qg_ref[...], kg_ref[...])

    z = fl + bf_ref[...]
    log_f = jnp.minimum(z, 0.0) - jnp.log1p(jnp.exp(-jnp.abs(z)))
    row = lax.broadcasted_iota(jnp.int32, (tm, tm), 0)
    col = lax.broadcasted_iota(jnp.int32, (tm, tm), 1)
    cs = _cumsum_rows(jnp.where(row >= col, 1.0, 0.0).astype(BF16), log_f) + carry_ref[...]
    carry_ref[...] = cs[tm - 1:tm, :]
    c_hi, c_mid, c_lo = _split3(cs * log2e)

    lane = lax.broadcasted_iota(jnp.int32, (tm, LANES), 1)
    is_data = lane < HEAD_DIM
    in_range = lambda lo, hi: ((lane >= HEAD_DIM + lo) & (lane < HEAD_DIM + hi)).astype(F32)
    q_base = in_range(3, 7)
    k_base = jnp.where(lane == HEAD_DIM + 6, neg_bound, in_range(0, 3))
    one_col = in_range(0, 1)

    def place(pieces, base):
        out = base
        for off, piece in pieces:
            out = jnp.where(lane == off, piece, out)
        return out

    for h_idx in range(n_heads):
        if n_lead + h_idx < len(tiles):
            project(tiles[n_lead + h_idx])
        pair = slice((h_idx // 2) * LANES, (h_idx // 2 + 1) * LANES)
        shift = lambda z: z[:, pair] if h_idx % 2 == 0 else pltpu.roll(z[:, pair], HEAD_DIM, axis=1)
        ch = [jnp.broadcast_to(piece[:, h_idx:h_idx + 1], (tm, LANES)) for piece in (c_hi, c_mid, c_lo)]
        q_extra = place([(HEAD_DIM + i, ch[i]) for i in range(3)], q_base)
        k_extra = place([(HEAD_DIM + 3 + i, -ch[i]) for i in range(3)], k_base)
        qa_ref[0, h_idx] = jnp.where(is_data, shift(q), q_extra).astype(BF16)
        ka_ref[0, h_idx] = jnp.where(is_data, shift(k), k_extra).astype(BF16)
        v_t = jnp.where(is_data, shift(v), one_col).T.astype(BF16)
        for sub in range(tm // ATTN_KEY_SUB):
            va_ref[0, h_idx, sub] = v_t[:, sub * ATTN_KEY_SUB:(sub + 1) * ATTN_KEY_SUB]
    for tile in tiles[n_lead + n_heads:]:
        project(tile)


def _inproj_fox(x, g, w_cat, mu, q_g, k_g, b_f, ones_bd, rw_cols, gate_cols, tm):
    b, s, d_model = x.shape
    d = q_g.shape[-1]
    n_heads = d // HEAD_DIM
    const = lambda shape: pl.BlockSpec(shape, lambda i, j: (0,) * len(shape))
    rows = lambda width: pl.BlockSpec((1, tm, width), lambda i, j: (i, j, 0))
    head_spec = pl.BlockSpec((1, n_heads, tm, LANES), lambda i, j: (i, 0, j, 0))
    head_shape = jax.ShapeDtypeStruct((b, n_heads, s, LANES), BF16)
    return pl.pallas_call(
        functools.partial(_inproj_fox_body, n_heads=n_heads),
        grid=(b, s // tm),
        in_specs=[
            rows(d_model), const((1, d_model)),
            pl.BlockSpec(w_cat.shape, lambda i, j: (0, 0), pipeline_mode=pl.Buffered(1)),
            const((1, rw_cols)),
            const((1, d)), const((1, d)), const((1, LANES)), const((MXU_DIM, MXU_DIM)),
        ],
        out_specs=[rows(rw_cols), rows(gate_cols), head_spec, head_spec,
                   pl.BlockSpec((1, n_heads, tm // ATTN_KEY_SUB, LANES, ATTN_KEY_SUB),
                                lambda i, j: (i, 0, j, 0, 0))],
        out_shape=[jax.ShapeDtypeStruct((b, s, rw_cols), F32), jax.ShapeDtypeStruct((b, s, gate_cols), BF16),
                   head_shape, head_shape,
                   jax.ShapeDtypeStruct((b, n_heads, s // ATTN_KEY_SUB, LANES, ATTN_KEY_SUB), BF16)],
        scratch_shapes=[pltpu.VMEM((1, LANES), F32),
                        pltpu.VMEM((1, rw_cols), F32)],
        compiler_params=pltpu.CompilerParams(
            dimension_semantics=("arbitrary", "arbitrary"), vmem_limit_bytes=VMEM_LIMIT),
        name="inproj_fox",
    )(x, g, w_cat, mu, q_g, k_g, b_f, ones_bd)


def _rwkv_body(xs_ref, wup_ref, w0_ref, aup_ref, a0_ref, kk_ref, ka_ref, rk_ref, gng_ref, gnb_ref, ones_ref,
               y_ref, state_ref, m_ref, n_ref, rw_ref, yp_ref, bonus_ref, *, lora, n_chunks):
    @pl.when(pl.program_id(1) == 0)
    def _():
        state_ref[...] = jnp.zeros_like(state_ref)
        for ref in (m_ref, n_ref, rw_ref, yp_ref, bonus_ref):
            ref[...] = jnp.zeros_like(ref)

    L = CHUNK
    P = LANES
    d = y_ref.shape[-1]
    n_pairs = d // P
    ones_bd = ones_ref[...]
    chains = [(c, j) for c in range(n_chunks) for j in range(n_pairs)]
    row_l = lax.broadcasted_iota(jnp.int32, (L, L), 0)
    col_l = lax.broadcasted_iota(jnp.int32, (L, L), 1)
    tri = jnp.where(row_l >= col_l, 1.0, 0.0).astype(BF16)
    wup_x3 = _stack_x3_rhs(wup_ref[...])
    aup_x3 = _stack_x3_rhs(aup_ref[...])

    def state_pass(c):
        out = []
        for j in range(n_pairs):
            idx = c * n_pairs + j
            s0 = _bf(state_ref[j])
            out.append(_dot(rw_ref[idx], s0) + yp_ref[idx])
            state_ref[j] = _dot(m_ref[idx], s0) + n_ref[idx]
        return jnp.concatenate(out, axis=1)

    def prep(c):
        xs = xs_ref[0, c * L:(c + 1) * L, :]
        r = xs[:, 0:d]
        k = xs[:, d:2 * d]
        v = xs[:, 2 * d:3 * d]
        wl = xs[:, 3 * d:3 * d + lora]
        al = xs[:, 3 * d + lora:3 * d + 2 * lora]
        w_raw = w0_ref[...] + _dot_x3(jnp.tanh(wl), wup_x3)
        logw = (-math.exp(-0.5)) * _sigmoid(w_raw)
        a = _sigmoid(a0_ref[...] + _dot_x3(al, aup_x3))
        kkr = k * kk_ref[...]
        kk = kkr * lax.rsqrt(jnp.maximum(_headsum(kkr * kkr, ones_bd, pieces=1), 1e-24))
        k2 = k * (1.0 + (a - 1.0) * ka_ref[...])
        bonus = _headsum(r * k2 * rk_ref[...], ones_bd, pieces=1) * v
        cum = _cumsum_rows(tri, logw)
        cum_end = cum[L - 1:L, :]
        e_neg = jnp.exp(-cum)
        e_end = jnp.exp(cum_end - cum)
        kka = kk * a
        return dict(
            v=v, bonus=bonus,
            g_end=jnp.exp(cum_end),
            r_t=r * jnp.exp(cum),
            a_t=-kk * jnp.exp(cum - logw),
            b_t=kka * e_neg, k_t=k2 * e_neg,
            b_e=kka * e_end, k_e=k2 * e_end)

    ys, cur = [], []
    for c in range(n_chunks):
        ys.append(state_pass(c))
        cur.append(prep(c))

    y = jnp.concatenate(ys, axis=0)
    inv_n = 1.0 / HEAD_DIM
    mean = _headsum(y, ones_bd) * inv_n
    yc = y - mean
    var = _headsum(yc * yc, ones_bd) * inv_n
    y_ref[0] = ((yc * lax.rsqrt(var + GN_EPS)) * gng_ref[...] + gnb_ref[...] + bonus_ref[...]).astype(y_ref.dtype)
    bonus_ref[...] = jnp.concatenate([t["bonus"] for t in cur], axis=0)

    pair = lambda name, c, j: cur[c][name][:, j * P:(j + 1) * P]
    lane = lax.broadcasted_iota(jnp.int32, (L, P), 1)
    first = lane < HEAD_DIM
    stack = lambda z: jnp.concatenate([jnp.where(first, z, 0.0), jnp.where(first, 0.0, z)], axis=0)

    row = lax.broadcasted_iota(jnp.int32, (P, P), 0)
    col = lax.broadcasted_iota(jnp.int32, (P, P), 1)
    same = lambda s: (row >> s) == (col >> s)
    strict = same(LOG2_CHUNK) & (row > col)
    incl = same(LOG2_CHUNK) & (row >= col)
    eye = row == col

    amat = [_dot_nt(_bf(jnp.concatenate([stack(pair("a_t", c, j)), stack(pair("r_t", c, j))], axis=0)),
                    _bf(jnp.concatenate([pair("b_t", c, j)] * 2 + [pair("k_t", c, j)] * 2, axis=0)))
            for c, j in chains]
    n_st = [jnp.where(strict, m[:P, :P], 0.0) for m in amat]
    a_ak = [_bf(jnp.where(strict, m[:P, P:], 0.0)) for m in amat]
    a_r = [_bf(jnp.concatenate([jnp.where(incl, m[P:, :P], 0.0), jnp.where(incl, m[P:, P:], 0.0)], axis=1))
           for m in amat]

    n0_f = [jnp.where(same(LOG2_BASE), n, 0.0) for n in n_st]
    n0 = [_bf(n) for n in n0_f]
    n2 = [_bf(_dot(n, n)) for n in n0]
    n4 = [_bf(_dot(n, n)) for n in n2]
    t = [jnp.where(eye, 1.0, n) for n in n0_f]
    t = [x + _dot(_bf(x), m) for x, m in zip(t, n2)]
    t = [x + _dot(_bf(x), m) for x, m in zip(t, n4)]
    for s in range(LOG2_BASE, LOG2_CHUNK):
        off_mask = same(s + 1) & jnp.logical_not(same(s))
        n_off = [_bf(jnp.where(off_mask, n, 0.0)) for n in n_st]
        tb = [_bf(x) for x in t]
        half = [_bf(_dot(x, m)) for x, m in zip(tb, n_off)]
        t = [x + _dot(h, xb) for x, h, xb in zip(t, half, tb)]
    tb = [_bf(x) for x in t]

    v_st = [_bf(stack(pair("v", c, j))) for c, j in chains]
    x_st = [_dot(m, vs) for m, vs in zip(a_ak, v_st)]
    wu = [_dot(x, jnp.concatenate([_bf(stack(pair("a_t", c, j))), _bf(xs_)], axis=1))
          for x, xs_, (c, j) in zip(tb, x_st, chains)]
    zeros_b = jnp.zeros((P, P), BF16)
    rhs2 = [jnp.concatenate([_bf(w), jnp.concatenate([zeros_b, vs], axis=1)], axis=0)
            for w, vs in zip(wu, v_st)]
    top = [_dot(m, rh) for m, rh in zip(a_r, rhs2)]
    bot = [_dot_tn(_bf(jnp.concatenate([stack(pair("b_e", c, j)), stack(pair("k_e", c, j))], axis=0)), rh)
           for rh, (c, j) in zip(rhs2, chains)]

    fold = lambda z: z[:L] + z[L:]
    for idx, (c, j) in enumerate(chains):
        rw_ref[idx] = _bf(pair("r_t", c, j) + fold(top[idx][:, :P]))
        yp_ref[idx] = fold(top[idx][:, P:])
        m_ref[idx] = _bf(jnp.where(eye, pair("g_end", c, j), 0.0) + bot[idx][:, :P])
        n_ref[idx] = bot[idx][:, P:]


def _rwkv(xs_rw, w_up, w0, a_up, a0, k_k, k_a, r_k, gn_g, gn_b, ones_bd, n_chunks):
    b, s, cols = xs_rw.shape
    lora, d = w_up.shape
    rows = n_chunks * CHUNK
    n_tiles = s // rows
    n_chains = n_chunks * (d // LANES)
    row = lambda a: a.reshape(1, -1)
    const = lambda shape: pl.BlockSpec(shape, lambda i, j: (0,) * len(shape))
    return pl.pallas_call(
        functools.partial(_rwkv_body, lora=lora, n_chunks=n_chunks),
        grid=(b, n_tiles + 1),
        in_specs=[
            pl.BlockSpec((1, rows, cols), lambda i, j: (i, jnp.minimum(j, n_tiles - 1), 0)),
            const((lora, d)), const((1, d)), const((lora, d)), const((1, d)),
            const((1, d)), const((1, d)), const((1, d)), const((1, d)), const((1, d)), const((MXU_DIM, MXU_DIM)),
        ],
        out_specs=pl.BlockSpec((1, rows, d), lambda i, j: (i, jnp.maximum(j - 1, 0), 0)),
        out_shape=jax.ShapeDtypeStruct((b, s, d), BF16),
        scratch_shapes=[
            pltpu.VMEM((d // LANES, LANES, LANES), F32),
            pltpu.VMEM((n_chains, LANES, LANES), BF16),
            pltpu.VMEM((n_chains, LANES, LANES), F32),
            pltpu.VMEM((n_chains, CHUNK, LANES), BF16),
            pltpu.VMEM((n_chains, CHUNK, LANES), F32),
            pltpu.VMEM((rows, d), F32),
        ],
        compiler_params=pltpu.CompilerParams(
            dimension_semantics=("arbitrary", "arbitrary"), vmem_limit_bytes=VMEM_LIMIT),
        name="rwkv7",
    )(xs_rw, w_up, row(w0), a_up, row(a0), row(k_k), row(k_a), row(r_k), row(gn_g), row(gn_b), ones_bd)


def _attn_body(q_ref, k_ref, vt_ref, o_ref, *, heads_per_step, tq, online_max):
    qi = pl.program_id(2)
    tk = vt_ref.shape[-1]
    n_sub = tq // tk
    heads = range(heads_per_step)
    qs = [q_ref[0, hh] for hh in heads]

    items = [(u, hh) for u in range(n_sub) for hh in heads]
    zero = jnp.zeros((V_ROWS, tq), F32)

    def scores(jb, u, hh, diag):
        kb = k_ref[0, hh, pl.ds(pl.multiple_of((jb * n_sub + u) * tk, tk), tk), :]
        if not diag:
            return _dot_nt(kb, qs[hh])
        st = _dot_nt(kb, qs[hh][u * tk:, :])
        key = lax.broadcasted_iota(jnp.int32, st.shape, 0)
        qry = lax.broadcasted_iota(jnp.int32, st.shape, 1)
        return jnp.where(key <= qry, st, NEG_BIG)

    def values(sub_block, hh):
        return vt_ref[0, hh, sub_block, 0:V_ROWS, :]

    def add_cols(acc, part):
        off = acc.shape[1] - part.shape[1]
        return acc + part if off == 0 else jnp.concatenate([acc[:, :off], acc[:, off:] + part], axis=1)

    if online_max:
        def block(jb, carry, diag):
            new = list(carry)
            for u, hh in items:
                st = scores(jb, u, hh, diag)
                off = tq - st.shape[1]
                m, acc = new[hh]
                m_new = jnp.maximum(m[:, off:], jnp.max(st, axis=0, keepdims=True))
                m_new = m_new if off == 0 else jnp.concatenate([m[:, :off], m_new], axis=1)
                p = jnp.exp2(st - m_new[:, off:]).astype(BF16)
                new[hh] = (m_new, add_cols(acc * jnp.exp2(m - m_new), _dot(values(jb * n_sub + u, hh), p)))
            return tuple(new)

        init = tuple((jnp.full((1, tq), NEG_BIG, F32), zero) for _ in heads)
        carry = lax.fori_loop(0, qi, functools.partial(block, diag=False), init)
        accs = [c[1] for c in block(qi, carry, True)]
    else:
        def blocks(jbs, carry, diag):
            work = [(jb, u, hh) for jb in jbs for u, hh in items]
            tiles = [scores(jb, u, hh, diag) for jb, u, hh in work]
            probs = [jnp.exp2(st).astype(BF16) for st in tiles]
            new = list(carry)
            for (jb, u, hh), p in zip(work, probs):
                new[hh] = add_cols(new[hh], _dot(values(jb * n_sub + u, hh), p))
            return tuple(new)

        def span(jb0, n_blocks, carry):
            n_keys = n_blocks * tq
            start = pl.multiple_of(jb0 * tq, tq)
            tiles = [_dot_nt(k_ref[0, hh, pl.ds(start, n_keys), :], qs[hh]) for hh in heads]
            probs = [jnp.exp2(st).astype(BF16) for st in tiles]
            new = []
            for hh in heads:
                vt = jnp.concatenate([values(jb0 * n_sub + u, hh) for u in range(n_blocks * n_sub)], axis=1)
                new.append(carry[hh] + _dot(vt, probs[hh]))
            return tuple(new)

        per_iter = ATTN_BLOCKS_PER_ITER
        carry = lax.fori_loop(0, qi // per_iter, lambda i, c: span(i * per_iter, per_iter, c),
                              tuple(zero for _ in heads))
        first_left = (qi // per_iter) * per_iter
        carry = lax.fori_loop(first_left, qi, lambda jb, c: span(jb, 1, c), carry)
        accs = blocks([qi], carry, True)
    out_t = jnp.concatenate([acc[:HEAD_DIM] / acc[HEAD_DIM:HEAD_DIM + 1] for acc in accs], axis=0)
    o_ref[0] = out_t.astype(o_ref.dtype).T


def _attention(qa, ka, vt, tq, online_max):
    b, h, s, _ = qa.shape
    hps = ATTN_HEADS_PER_STEP if h % ATTN_HEADS_PER_STEP == 0 else LANES // HEAD_DIM
    return pl.pallas_call(
        functools.partial(_attn_body, heads_per_step=hps, tq=tq, online_max=online_max),
        grid=(b, h // hps, s // tq),
        in_specs=[
            pl.BlockSpec((1, hps, tq, LANES), lambda i, g, j: (i, g, j, 0)),
            pl.BlockSpec((1, hps, s, LANES), lambda i, g, j: (i, g, 0, 0)),
            pl.BlockSpec((1, hps) + vt.shape[2:], lambda i, g, j: (i, g, 0, 0, 0)),
        ],
        out_specs=pl.BlockSpec((1, tq, hps * HEAD_DIM), lambda i, g, j: (i, j, g)),
        out_shape=jax.ShapeDtypeStruct((b, s, h * HEAD_DIM), BF16),
        compiler_params=pltpu.CompilerParams(
            dimension_semantics=("parallel", "parallel", "arbitrary"), vmem_limit_bytes=VMEM_LIMIT),
        name="fox_attention_online_max" if online_max else "fox_attention",
    )(qa, ka, vt)


def _outproj_body(x_ref, yrw_ref, yfox_ref, gate_ref, w_ref, fg_ref, o_ref, *, final):
    g = gate_ref[...].astype(F32)
    sg = g * _sigmoid(g)
    d_rw = yrw_ref.shape[1]
    y_rw = (yrw_ref[...].astype(F32) * sg[:, :d_rw]).astype(BF16)
    y_fox = (yfox_ref[...].astype(F32) * sg[:, d_rw:]).astype(BF16)
    z = x_ref[...] + _dot(y_rw, w_ref[0:d_rw, :]) + _dot(y_fox, w_ref[d_rw:, :])
    if final:
        z = (z * lax.rsqrt(jnp.mean(z * z, axis=-1, keepdims=True) + NORM_EPS)) * fg_ref[...]
    o_ref[...] = z


def _outproj(x2, y_rw, y_fox, gate, w_out, final_g, final, tm):
    n, d = x2.shape
    d_rw, d_fox = y_rw.shape[1], y_fox.shape[1]
    return pl.pallas_call(
        functools.partial(_outproj_body, final=final),
        grid=(n // tm,),
        in_specs=[
            pl.BlockSpec((tm, d), lambda i: (i, 0)),
            pl.BlockSpec((tm, d_rw), lambda i: (i, 0)),
            pl.BlockSpec((tm, d_fox), lambda i: (i, 0)),
            pl.BlockSpec((tm, d_rw + d_fox), lambda i: (i, 0)),
            pl.BlockSpec(w_out.shape, lambda i: (0, 0)),
            pl.BlockSpec((1, d), lambda i: (0, 0)),
        ],
        out_specs=pl.BlockSpec((tm, d), lambda i: (i, 0)),
        out_shape=jax.ShapeDtypeStruct((n, d), F32),
        compiler_params=pltpu.CompilerParams(
            dimension_semantics=("parallel",), vmem_limit_bytes=VMEM_LIMIT),
        name="outproj",
    )(x2, y_rw, y_fox, gate, w_out, final_g)


def _pick_tile(n, target):
    t = min(n, target)
    while n % t:
        t //= 2
    return t


def kernel(x, norm_g, w_in, rw_mu, rw_w_up, rw_w0, rw_a_up, rw_a0, rw_k_k, rw_k_a, rw_r_k, rw_gn_g, rw_gn_b,
           fox_q_g, fox_k_g, fox_b_f, w_out, final_g):
    b, s, d_model = x.shape
    depth = norm_g.shape[0]
    d_rw = rw_w0.shape[-1]
    lora = rw_w_up.shape[1]
    d_mix = w_out.shape[1]
    d_fox = d_mix - d_rw
    h_fox = fox_b_f.shape[-1]
    rw_cols = 3 * d_rw + 2 * lora
    fox_cols = 3 * d_fox + h_fox
    assert d_rw == d_fox and s % CHUNK == 0 and h_fox <= LANES

    head = lax.broadcasted_iota(jnp.int32, (MXU_DIM, MXU_DIM), 0) // HEAD_DIM
    ones_bd = (head == head.T).astype(BF16)

    tm_out = _pick_tile(b * s, 1024)
    tm_in = _pick_tile(s, 512)
    tq = _pick_tile(s, 512)

    x2 = x.reshape(b * s, d_model)
    for l in range(depth):
        w = w_in[l]
        w_fl = jnp.pad(w[:, rw_cols + 3 * d_fox:rw_cols + fox_cols], ((0, 0), (0, LANES - h_fox)))
        w_cat = jnp.concatenate(
            [w[:, :rw_cols], w[:, rw_cols:rw_cols + 3 * d_fox], w[:, rw_cols + fox_cols:], w_fl], axis=1
        ).astype(BF16)
        tile_heads = lambda g: jnp.tile(g, d_fox // HEAD_DIM).reshape(1, -1)
        b_f = jnp.pad(fox_b_f[l], (0, LANES - h_fox)).reshape(1, -1)
        q_g, k_g = tile_heads(fox_q_g[l]), tile_heads(fox_k_g[l])
        xs_rw, gate, qa, ka, va = _inproj_fox(x2.reshape(b, s, d_model), norm_g[l].reshape(1, -1), w_cat,
                                              rw_mu[l].reshape(1, -1), q_g, k_g, b_f, ones_bd, rw_cols, d_mix,
                                              tm_in)

        y_rw = _rwkv(xs_rw, rw_w_up[l], rw_w0[l], rw_a_up[l], rw_a0[l],
                     rw_k_k[l], rw_k_a[l], rw_r_k[l].reshape(-1), rw_gn_g[l], rw_gn_b[l], ones_bd,
                     RWKV_CHUNKS_PER_STEP if s % (RWKV_CHUNKS_PER_STEP * CHUNK) == 0 else 1)

        y_fox = lax.cond(_score_bound(q_g, k_g)[0, 0] <= SAFE_SCORE_BOUND,
                         functools.partial(_attention, tq=tq, online_max=False),
                         functools.partial(_attention, tq=tq, online_max=True), qa, ka, va)

        x2 = _outproj(x2, y_rw.reshape(b * s, d_rw), y_fox.reshape(b * s, d_fox), gate.reshape(b * s, d_mix),
                      w_out[l].astype(BF16), final_g.reshape(1, -1), l == depth - 1, tm_out)
    return x2.reshape(b, s, d_model)
```

```python
import functools
import math

import jax
import jax.numpy as jnp
from jax import lax
from jax.experimental import pallas as pl
from jax.experimental.pallas import tpu as pltpu

F32 = jnp.float32
BF16 = jnp.bfloat16

HEAD_DIM = 64
NORM_EPS = 1e-6
GN_EPS = 64e-5
CHUNK = 64
LOG2_CHUNK = CHUNK.bit_length() - 1
LOG2_BASE = 3
RWKV_CHUNKS_PER_STEP = 4
LANES = 128
MXU_DIM = 256
NEG_BIG = -1e30
ATTN_HEADS_PER_STEP = 4
ATTN_KEY_SUB = 256
ATTN_BLOCKS_PER_ITER = 2
V_ROWS = 80
SAFE_SCORE_BOUND = 40.0
VMEM_LIMIT = 56 * 1024 * 1024


def _dot(a, b):
    return jnp.dot(a, b, preferred_element_type=F32)


def _dot_nt(a, b):
    return lax.dot_general(a, b, (((1,), (1,)), ((), ())), preferred_element_type=F32)


def _dot_tn(a, b):
    return lax.dot_general(a, b, (((0,), (0,)), ((), ())), preferred_element_type=F32)


def _sigmoid(x):
    return 1.0 / (1.0 + jnp.exp(-x))


def _bf(x):
    return x.astype(BF16)


def _split2(x):
    hi = x.astype(BF16)
    return hi, (x - hi.astype(F32)).astype(BF16)


def _split3(x):
    hi = x.astype(BF16)
    r1 = x - hi.astype(F32)
    mid = r1.astype(BF16)
    lo = (r1 - mid.astype(F32)).astype(BF16)
    return hi.astype(F32), mid.astype(F32), lo.astype(F32)


def _stack_x3_rhs(b):
    bh, bl = _split2(b)
    return jnp.concatenate([bh, bl, bh], axis=0)


def _dot_x3(a, b_stacked):
    ah, al = _split2(a)
    return _dot(jnp.concatenate([ah, ah, al], axis=1), b_stacked)


def _dot_pieces(x, ones_bf16, pieces):
    out = None
    for _ in range(pieces):
        hi = x.astype(BF16)
        term = _dot(hi, ones_bf16)
        out = term if out is None else out + term
        x = x - hi.astype(F32)
    return out


def _cumsum_rows(tri_bf16, x):
    out = None
    for _ in range(3):
        hi = x.astype(BF16)
        term = _dot(tri_bf16, hi)
        out = term if out is None else out + term
        x = x - hi.astype(F32)
    return out


def _headsum(x, ones_bd, pieces=2):
    width = x.shape[1]
    return jnp.concatenate([_dot_pieces(x[:, j:j + MXU_DIM], ones_bd, pieces) for j in range(0, width, MXU_DIM)],
                           axis=1)


def _score_bound(q_g, k_g):
    amax = lambda g: jnp.max(jnp.abs(g), axis=-1, keepdims=True)
    return (HEAD_DIM ** 0.5) * amax(q_g) * amax(k_g)


def _inproj_fox_body(x_ref, g_ref, w_ref, mu_ref, qg_ref, kg_ref, bf_ref, ones_ref,
                     rw_ref, gate_ref, qa_ref, ka_ref, va_ref, carry_ref, shift_ref, *, n_heads):
    @pl.when(pl.program_id(1) == 0)
    def _():
        carry_ref[...] = jnp.zeros_like(carry_ref)
        shift_ref[...] = jnp.zeros_like(shift_ref)

    x = x_ref[0]
    tm = x.shape[0]
    d = n_heads * HEAD_DIM
    rw_cols, gate_cols = rw_ref.shape[-1], gate_ref.shape[-1]
    off_qkv, off_gate, off_fl = rw_cols, rw_cols + 3 * d, rw_cols + 3 * d + gate_cols
    y = x * lax.rsqrt(jnp.mean(x * x, axis=-1, keepdims=True) + NORM_EPS)
    h = (y * g_ref[...]).astype(BF16)

    t = _dot(h, w_ref[:, off_qkv:off_qkv + 3 * d])
    fl = _dot(h, w_ref[:, off_fl:off_fl + LANES])

    tiles = [(rw_ref, c0, c0) for c0 in range(0, rw_cols, MXU_DIM)]
    tiles += [(gate_ref, c0, off_gate + c0) for c0 in range(0, gate_cols, MXU_DIM)]

    row8 = lax.broadcasted_iota(jnp.int32, (8, MXU_DIM), 0)

    def project(tile):
        ref, dst, src = tile
        width = min(MXU_DIM, ref.shape[-1] - dst)
        out = _dot(h, w_ref[:, src:src + width])
        if ref is rw_ref:
            cols = slice(dst, dst + width)
            before = shift_ref[:, cols]
            shift_ref[:, cols] = out[tm - 1:tm, :]
            rolled = pltpu.roll(out, 1, axis=0)
            prev = jnp.concatenate([jnp.where(row8[:, :width] == 0, before, rolled[:8]), rolled[8:]], axis=0)
            out = out + (prev - out) * mu_ref[:, cols]
        ref[0, :, dst:dst + width] = out.astype(ref.dtype)

    n_lead = max(len(tiles) - n_heads, 0)
    for tile in tiles[:n_lead]:
        project(tile)

    ones_bd = ones_ref[...]
    inv_n = 1.0 / HEAD_DIM
    log2e = math.log2(math.e)

    def headnorm(z, g):
        ms = _headsum(z * z, ones_bd) * inv_n
        return (z * lax.rsqrt(ms + NORM_EPS)) * g

    q = headnorm(t[:, 0:d], qg_ref[...]) * (HEAD_DIM ** -0.5 * log2e)
    k = headnorm(t[:, d:2 * d], kg_ref[...])
    v = t[:, 2 * d:3 * d]
    neg_bound = -log2e * _score_bound(qg_ref[...], kg_ref[...])

    z = fl + bf_ref[...]
    log_f = jnp.minimum(z, 0.0) - jnp.log1p(jnp.exp(-jnp.abs(z)))
    row = lax.broadcasted_iota(jnp.int32, (tm, tm), 0)
    col = lax.broadcasted_iota(jnp.int32, (tm, tm), 1)
    cs = _cumsum_rows(jnp.where(row >= col, 1.0, 0.0).astype(BF16), log_f) + carry_ref[...]
    carry_ref[...] = cs[tm - 1:tm, :]
    c_hi, c_mid, c_lo = _split3(cs * log2e)

    lane = lax.broadcasted_iota(jnp.int32, (tm, LANES), 1)
    is_data = lane < HEAD_DIM
    in_range = lambda lo, hi: ((lane >= HEAD_DIM + lo) & (lane < HEAD_DIM + hi)).astype(F32)
    q_base = in_range(3, 7)
    k_base = jnp.where(lane == HEAD_DIM + 6, neg_bound, in_range(0, 3))
    one_col = in_range(0, 1)

    def place(pieces, base):
        out = base
        for off, piece in pieces:
            out = jnp.where(lane == off, piece, out)
        return out

    for h_idx in range(n_heads):
        if n_lead + h_idx < len(tiles):
            project(tiles[n_lead + h_idx])
        pair = slice((h_idx // 2) * LANES, (h_idx // 2 + 1) * LANES)
        shift = lambda z: z[:, pair] if h_idx % 2 == 0 else pltpu.roll(z[:, pair], HEAD_DIM, axis=1)
        ch = [jnp.broadcast_to(piece[:, h_idx:h_idx + 1], (tm, LANES)) for piece in (c_hi, c_mid, c_lo)]
        q_extra = place([(HEAD_DIM + i, ch[i]) for i in range(3)], q_base)
        k_extra = place([(HEAD_DIM + 3 + i, -ch[i]) for i in range(3)], k_base)
        qa_ref[0, h_idx] = jnp.where(is_data, shift(q), q_extra).astype(BF16)
        ka_ref[0, h_idx] = jnp.where(is_data, shift(k), k_extra).astype(BF16)
        v_t = jnp.where(is_data, shift(v), one_col).T.astype(BF16)
        for sub in range(tm // ATTN_KEY_SUB):
            va_ref[0, h_idx, sub] = v_t[:, sub * ATTN_KEY_SUB:(sub + 1) * ATTN_KEY_SUB]
    for tile in tiles[n_lead + n_heads:]:
        project(tile)


def _inproj_fox(x, g, w_cat, mu, q_g, k_g, b_f, ones_bd, rw_cols, gate_cols, tm):
    b, s, d_model = x.shape
    d = q_g.shape[-1]
    n_heads = d // HEAD_DIM
    const = lambda shape: pl.BlockSpec(shape, lambda i, j: (0,) * len(shape))
    rows = lambda width: pl.BlockSpec((1, tm, width), lambda i, j: (i, j, 0))
    head_spec = pl.BlockSpec((1, n_heads, tm, LANES), lambda i, j: (i, 0, j, 0))
    head_shape = jax.ShapeDtypeStruct((b, n_heads, s, LANES), BF16)
    return pl.pallas_call(
        functools.partial(_inproj_fox_body, n_heads=n_heads),
        grid=(b, s // tm),
        in_specs=[
            rows(d_model), const((1, d_model)),
            pl.BlockSpec(w_cat.shape, lambda i, j: (0, 0), pipeline_mode=pl.Buffered(1)),
            const((1, rw_cols)),
            const((1, d)), const((1, d)), const((1, LANES)), const((MXU_DIM, MXU_DIM)),
        ],
        out_specs=[rows(rw_cols), rows(gate_cols), head_spec, head_spec,
                   pl.BlockSpec((1, n_heads, tm // ATTN_KEY_SUB, LANES, ATTN_KEY_SUB),
                                lambda i, j: (i, 0, j, 0, 0))],
        out_shape=[jax.ShapeDtypeStruct((b, s, rw_cols), F32), jax.ShapeDtypeStruct((b, s, gate_cols), BF16),
                   head_shape, head_shape,
                   jax.ShapeDtypeStruct((b, n_heads, s // ATTN_KEY_SUB, LANES, ATTN_KEY_SUB), BF16)],
        scratch_shapes=[pltpu.VMEM((1, LANES), F32),
                        pltpu.VMEM((1, rw_cols), F32)],
        compiler_params=pltpu.CompilerParams(
            dimension_semantics=("arbitrary", "arbitrary"), vmem_limit_bytes=VMEM_LIMIT),
        name="inproj_fox",
    )(x, g, w_cat, mu, q_g, k_g, b_f, ones_bd)


def _rwkv_body(xs_ref, wup_ref, w0_ref, aup_ref, a0_ref, kk_ref, ka_ref, rk_ref, gng_ref, gnb_ref, ones_ref,
               y_ref, state_ref, m_ref, n_ref, rw_ref, yp_ref, bonus_ref, *, lora, n_chunks):
    @pl.when(pl.program_id(1) == 0)
    def _():
        state_ref[...] = jnp.zeros_like(state_ref)
        for ref in (m_ref, n_ref, rw_ref, yp_ref, bonus_ref):
            ref[...] = jnp.zeros_like(ref)

    L = CHUNK
    P = LANES
    d = y_ref.shape[-1]
    n_pairs = d // P
    ones_bd = ones_ref[...]
    chains = [(c, j) for c in range(n_chunks) for j in range(n_pairs)]
    row_l = lax.broadcasted_iota(jnp.int32, (L, L), 0)
    col_l = lax.broadcasted_iota(jnp.int32, (L, L), 1)
    tri = jnp.where(row_l >= col_l, 1.0, 0.0).astype(BF16)
    wup_x3 = _stack_x3_rhs(wup_ref[...])
    aup_x3 = _stack_x3_rhs(aup_ref[...])

    def state_pass(c):
        out = []
        for j in range(n_pairs):
            idx = c * n_pairs + j
            s0 = _bf(state_ref[j])
            out.append(_dot(rw_ref[idx], s0) + yp_ref[idx])
            state_ref[j] = _dot(m_ref[idx], s0) + n_ref[idx]
        return jnp.concatenate(out, axis=1)

    def prep(c):
        xs = xs_ref[0, c * L:(c + 1) * L, :]
        r = xs[:, 0:d]
        k = xs[:, d:2 * d]
        v = xs[:, 2 * d:3 * d]
        wl = xs[:, 3 * d:3 * d + lora]
        al = xs[:, 3 * d + lora:3 * d + 2 * lora]
        w_raw = w0_ref[...] + _dot_x3(jnp.tanh(wl), wup_x3)
        logw = (-math.exp(-0.5)) * _sigmoid(w_raw)
        a = _sigmoid(a0_ref[...] + _dot_x3(al, aup_x3))
        kkr = k * kk_ref[...]
        kk = kkr * lax.rsqrt(jnp.maximum(_headsum(kkr * kkr, ones_bd, pieces=1), 1e-24))
        k2 = k * (1.0 + (a - 1.0) * ka_ref[...])
        bonus = _headsum(r * k2 * rk_ref[...], ones_bd, pieces=1) * v
        cum = _cumsum_rows(tri, logw)
        cum_end = cum[L - 1:L, :]
        e_neg = jnp.exp(-cum)
        e_end = jnp.exp(cum_end - cum)
        kka = kk * a
        return dict(
            v=v, bonus=bonus,
            g_end=jnp.exp(cum_end),
            r_t=r * jnp.exp(cum),
            a_t=-kk * jnp.exp(cum - logw),
            b_t=kka * e_neg, k_t=k2 * e_neg,
            b_e=kka * e_end, k_e=k2 * e_end)

    ys, cur = [], []
    for c in range(n_chunks):
        ys.append(state_pass(c))
        cur.append(prep(c))

    y = jnp.concatenate(ys, axis=0)
    inv_n = 1.0 / HEAD_DIM
    mean = _headsum(y, ones_bd) * inv_n
    yc = y - mean
    var = _headsum(yc * yc, ones_bd) * inv_n
    y_ref[0] = ((yc * lax.rsqrt(var + GN_EPS)) * gng_ref[...] + gnb_ref[...] + bonus_ref[...]).astype(y_ref.dtype)
    bonus_ref[...] = jnp.concatenate([t["bonus"] for t in cur], axis=0)

    pair = lambda name, c, j: cur[c][name][:, j * P:(j + 1) * P]
    lane = lax.broadcasted_iota(jnp.int32, (L, P), 1)
    first = lane < HEAD_DIM
    stack = lambda z: jnp.concatenate([jnp.where(first, z, 0.0), jnp.where(first, 0.0, z)], axis=0)

    row = lax.broadcasted_iota(jnp.int32, (P, P), 0)
    col = lax.broadcasted_iota(jnp.int32, (P, P), 1)
    same = lambda s: (row >> s) == (col >> s)
    strict = same(LOG2_CHUNK) & (row > col)
    incl = same(LOG2_CHUNK) & (row >= col)
    eye = row == col

    amat = [_dot_nt(_bf(jnp.concatenate([stack(pair("a_t", c, j)), stack(pair("r_t", c, j))], axis=0)),
                    _bf(jnp.concatenate([pair("b_t", c, j)] * 2 + [pair("k_t", c, j)] * 2, axis=0)))
            for c, j in chains]
    n_st = [jnp.where(strict, m[:P, :P], 0.0) for m in amat]
    a_ak = [_bf(jnp.where(strict, m[:P, P:], 0.0)) for m in amat]
    a_r = [_bf(jnp.concatenate([jnp.where(incl, m[P:, :P], 0.0), jnp.where(incl, m[P:, P:], 0.0)], axis=1))
           for m in amat]

    n0_f = [jnp.where(same(LOG2_BASE), n, 0.0) for n in n_st]
    n0 = [_bf(n) for n in n0_f]
    n2 = [_bf(_dot(n, n)) for n in n0]
    n4 = [_bf(_dot(n, n)) for n in n2]
    t = [jnp.where(eye, 1.0, n) for n in n0_f]
    t = [x + _dot(_bf(x), m) for x, m in zip(t, n2)]
    t = [x + _dot(_bf(x), m) for x, m in zip(t, n4)]
    for s in range(LOG2_BASE, LOG2_CHUNK):
        off_mask = same(s + 1) & jnp.logical_not(same(s))
        n_off = [_bf(jnp.where(off_mask, n, 0.0)) for n in n_st]
        tb = [_bf(x) for x in t]
        half = [_bf(_dot(x, m)) for x, m in zip(tb, n_off)]
        t = [x + _dot(h, xb) for x, h, xb in zip(t, half, tb)]
    tb = [_bf(x) for x in t]

    v_st = [_bf(stack(pair("v", c, j))) for c, j in chains]
    x_st = [_dot(m, vs) for m, vs in zip(a_ak, v_st)]
    wu = [_dot(x, jnp.concatenate([_bf(stack(pair("a_t", c, j))), _bf(xs_)], axis=1))
          for x, xs_, (c, j) in zip(tb, x_st, chains)]
    zeros_b = jnp.zeros((P, P), BF16)
    rhs2 = [jnp.concatenate([_bf(w), jnp.concatenate([zeros_b, vs], axis=1)], axis=0)
            for w, vs in zip(wu, v_st)]
    top = [_dot(m, rh) for m, rh in zip(a_r, rhs2)]
    bot = [_dot_tn(_bf(jnp.concatenate([stack(pair("b_e", c, j)), stack(pair("k_e", c, j))], axis=0)), rh)
           for rh, (c, j) in zip(rhs2, chains)]

    fold = lambda z: z[:L] + z[L:]
    for idx, (c, j) in enumerate(chains):
        rw_ref[idx] = _bf(pair("r_t", c, j) + fold(top[idx][:, :P]))
        yp_ref[idx] = fold(top[idx][:, P:])
        m_ref[idx] = _bf(jnp.where(eye, pair("g_end", c, j), 0.0) + bot[idx][:, :P])
        n_ref[idx] = bot[idx][:, P:]


def _rwkv(xs_rw, w_up, w0, a_up, a0, k_k, k_a, r_k, gn_g, gn_b, ones_bd, n_chunks):
    b, s, cols = xs_rw.shape
    lora, d = w_up.shape
    rows = n_chunks * CHUNK
    n_tiles = s // rows
    n_chains = n_chunks * (d // LANES)
    row = lambda a: a.reshape(1, -1)
    const = lambda shape: pl.BlockSpec(shape, lambda i, j: (0,) * len(shape))
    return pl.pallas_call(
        functools.partial(_rwkv_body, lora=lora, n_chunks=n_chunks),
        grid=(b, n_tiles + 1),
        in_specs=[
            pl.BlockSpec((1, rows, cols), lambda i, j: (i, jnp.minimum(j, n_tiles - 1), 0)),
            const((lora, d)), const((1, d)), const((lora, d)), const((1, d)),
            const((1, d)), const((1, d)), const((1, d)), const((1, d)), const((1, d)), const((MXU_DIM, MXU_DIM)),
        ],
        out_specs=pl.BlockSpec((1, rows, d), lambda i, j: (i, jnp.maximum(j - 1, 0), 0)),
        out_shape=jax.ShapeDtypeStruct((b, s, d), BF16),
        scratch_shapes=[
            pltpu.VMEM((d // LANES, LANES, LANES), F32),
            pltpu.VMEM((n_chains, LANES, LANES), BF16),
            pltpu.VMEM((n_chains, LANES, LANES), F32),
            pltpu.VMEM((n_chains, CHUNK, LANES), BF16),
            pltpu.VMEM((n_chains, CHUNK, LANES), F32),
            pltpu.VMEM((rows, d), F32),
        ],
        compiler_params=pltpu.CompilerParams(
            dimension_semantics=("arbitrary", "arbitrary"), vmem_limit_bytes=VMEM_LIMIT),
        name="rwkv7",
    )(xs_rw, w_up, row(w0), a_up, row(a0), row(k_k), row(k_a), row(r_k), row(gn_g), row(gn_b), ones_bd)


def _attn_body(q_ref, k_ref, vt_ref, o_ref, acc_ref, *, heads_per_step, tq, online_max):
    qi = pl.program_id(2)
    tk = vt_ref.shape[-1]
    n_sub = tq // tk
    heads = range(heads_per_step)
    qs = [q_ref[0, hh] for hh in heads]

    items = [(u, hh) for u in range(n_sub) for hh in heads]
    zero = jnp.zeros((V_ROWS, tq), F32)

    def scores(jb, u, hh, diag):
        kb = k_ref[0, hh, pl.ds(pl.multiple_of((jb * n_sub + u) * tk, tk), tk), :]
        if not diag:
            return _dot_nt(kb, qs[hh])
        st = _dot_nt(kb, qs[hh][u * tk:, :])
        key = lax.broadcasted_iota(jnp.int32, st.shape, 0)
        qry = lax.broadcasted_iota(jnp.int32, st.shape, 1)
        return jnp.where(key <= qry, st, NEG_BIG)

    def values(sub_block, hh):
        return vt_ref[0, hh, sub_block, 0:V_ROWS, :]

    def add_cols(acc, part):
        off = acc.shape[1] - part.shape[1]
        return acc + part if off == 0 else jnp.concatenate([acc[:, :off], acc[:, off:] + part], axis=1)

    if online_max:
        def block(jb, carry, diag):
            new = list(carry)
            for u, hh in items:
                st = scores(jb, u, hh, diag)
                off = tq - st.shape[1]
                m, acc = new[hh]
                m_new = jnp.maximum(m[:, off:], jnp.max(st, axis=0, keepdims=True))
                m_new = m_new if off == 0 else jnp.concatenate([m[:, :off], m_new], axis=1)
                p = jnp.exp2(st - m_new[:, off:]).astype(BF16)
                new[hh] = (m_new, add_cols(acc * jnp.exp2(m - m_new), _dot(values(jb * n_sub + u, hh), p)))
            return tuple(new)

        init = tuple((jnp.full((1, tq), NEG_BIG, F32), zero) for _ in heads)
        carry = lax.fori_loop(0, qi, functools.partial(block, diag=False), init)
        accs = [c[1] for c in block(qi, carry, True)]
    else:
        def blocks(jbs, carry, diag):
            work = [(jb, u, hh) for jb in jbs for u, hh in items]
            tiles = [scores(jb, u, hh, diag) for jb, u, hh in work]
            probs = [jnp.exp2(st).astype(BF16) for st in tiles]
            new = list(carry)
            for (jb, u, hh), p in zip(work, probs):
                new[hh] = add_cols(new[hh], _dot(values(jb * n_sub + u, hh), p))
            return tuple(new)

        def span(jb0, n_blocks, carry):
            n_keys = n_blocks * tq
            start = pl.multiple_of(jb0 * tq, tq)
            tiles = [_dot_nt(k_ref[0, hh, pl.ds(start, n_keys), :], qs[hh]) for hh in heads]
            probs = [jnp.exp2(st).astype(BF16) for st in tiles]
            for hh in heads:
                vt = jnp.concatenate([values(jb0 * n_sub + u, hh) for u in range(n_blocks * n_sub)], axis=1)
                acc_ref[hh] = acc_ref[hh] + _dot(vt, probs[hh])
            return carry

        acc_ref[...] = jnp.zeros_like(acc_ref)
        per_iter = ATTN_BLOCKS_PER_ITER
        lax.fori_loop(0, qi // per_iter, lambda i, c: span(i * per_iter, per_iter, c), 0)
        first_left = (qi // per_iter) * per_iter
        lax.fori_loop(first_left, qi, lambda jb, c: span(jb, 1, c), 0)
        accs = blocks([qi], tuple(acc_ref[hh] for hh in heads), True)
    out_t = jnp.concatenate([acc[:HEAD_DIM] / acc[HEAD_DIM:HEAD_DIM + 1] for acc in accs], axis=0)
    o_ref[0] = out_t.astype(o_ref.dtype).T


def _attention(qa, ka, vt, tq, online_max):
    b, h, s, _ = qa.shape
    hps = ATTN_HEADS_PER_STEP if h % ATTN_HEADS_PER_STEP == 0 else LANES // HEAD_DIM
    return pl.pallas_call(
        functools.partial(_attn_body, heads_per_step=hps, tq=tq, online_max=online_max),
        grid=(b, h // hps, s // tq),
        in_specs=[
            pl.BlockSpec((1, hps, tq, LANES), lambda i, g, j: (i, g, j, 0)),
            pl.BlockSpec((1, hps, s, LANES), lambda i, g, j: (i, g, 0, 0)),
            pl.BlockSpec((1, hps) + vt.shape[2:], lambda i, g, j: (i, g, 0, 0, 0)),
        ],
        out_specs=pl.BlockSpec((1, tq, hps * HEAD_DIM), lambda i, g, j: (i, j, g)),
        out_shape=jax.ShapeDtypeStruct((b, s, h * HEAD_DIM), BF16),
        scratch_shapes=[pltpu.VMEM((hps, V_ROWS, tq), F32)],
        compiler_params=pltpu.CompilerParams(
            dimension_semantics=("parallel", "parallel", "arbitrary"), vmem_limit_bytes=VMEM_LIMIT),
        name="fox_attention_online_max" if online_max else "fox_attention",
    )(qa, ka, vt)


def _outproj_body(x_ref, yrw_ref, yfox_ref, gate_ref, w_ref, fg_ref, o_ref, *, final):
    g = gate_ref[...].astype(F32)
    sg = g * _sigmoid(g)
    d_rw = yrw_ref.shape[1]
    y_rw = (yrw_ref[...].astype(F32) * sg[:, :d_rw]).astype(BF16)
    y_fox = (yfox_ref[...].astype(F32) * sg[:, d_rw:]).astype(BF16)
    z = x_ref[...] + _dot(y_rw, w_ref[0:d_rw, :]) + _dot(y_fox, w_ref[d_rw:, :])
    if final:
        z = (z * lax.rsqrt(jnp.mean(z * z, axis=-1, keepdims=True) + NORM_EPS)) * fg_ref[...]
    o_ref[...] = z


def _outproj(x2, y_rw, y_fox, gate, w_out, final_g, final, tm):
    n, d = x2.shape
    d_rw, d_fox = y_rw.shape[1], y_fox.shape[1]
    return pl.pallas_call(
        functools.partial(_outproj_body, final=final),
        grid=(n // tm,),
        in_specs=[
            pl.BlockSpec((tm, d), lambda i: (i, 0)),
            pl.BlockSpec((tm, d_rw), lambda i: (i, 0)),
            pl.BlockSpec((tm, d_fox), lambda i: (i, 0)),
            pl.BlockSpec((tm, d_rw + d_fox), lambda i: (i, 0)),
            pl.BlockSpec(w_out.shape, lambda i: (0, 0)),
            pl.BlockSpec((1, d), lambda i: (0, 0)),
        ],
        out_specs=pl.BlockSpec((tm, d), lambda i: (i, 0)),
        out_shape=jax.ShapeDtypeStruct((n, d), F32),
        compiler_params=pltpu.CompilerParams(
            dimension_semantics=("parallel",), vmem_limit_bytes=VMEM_LIMIT),
        name="outproj",
    )(x2, y_rw, y_fox, gate, w_out, final_g)


def _pick_tile(n, target):
    t = min(n, target)
    while n % t:
        t //= 2
    return t


def kernel(x, norm_g, w_in, rw_mu, rw_w_up, rw_w0, rw_a_up, rw_a0, rw_k_k, rw_k_a, rw_r_k, rw_gn_g, rw_gn_b,
           fox_q_g, fox_k_g, fox_b_f, w_out, final_g):
    b, s, d_model = x.shape
    depth = norm_g.shape[0]
    d_rw = rw_w0.shape[-1]
    lora = rw_w_up.shape[1]
    d_mix = w_out.shape[1]
    d_fox = d_mix - d_rw
    h_fox = fox_b_f.shape[-1]
    rw_cols = 3 * d_rw + 2 * lora
    fox_cols = 3 * d_fox + h_fox
    assert d_rw == d_fox and s % CHUNK == 0 and h_fox <= LANES

    head = lax.broadcasted_iota(jnp.int32, (MXU_DIM, MXU_DIM), 0) // HEAD_DIM
    ones_bd = (head == head.T).astype(BF16)

    tm_out = _pick_tile(b * s, 1024)
    tm_in = _pick_tile(s, 512)
    tq = _pick_tile(s, 512)

    x2 = x.reshape(b * s, d_model)
    for l in range(depth):
        w = w_in[l]
        w_fl = jnp.pad(w[:, rw_cols + 3 * d_fox:rw_cols + fox_cols], ((0, 0), (0, LANES - h_fox)))
        w_cat = jnp.concatenate(
            [w[:, :rw_cols], w[:, rw_cols:rw_cols + 3 * d_fox], w[:, rw_cols + fox_cols:], w_fl], axis=1
        ).astype(BF16)
        tile_heads = lambda g: jnp.tile(g, d_fox // HEAD_DIM).reshape(1, -1)
        b_f = jnp.pad(fox_b_f[l], (0, LANES - h_fox)).reshape(1, -1)
        q_g, k_g = tile_heads(fox_q_g[l]), tile_heads(fox_k_g[l])
        xs_rw, gate, qa, ka, va = _inproj_fox(x2.reshape(b, s, d_model), norm_g[l].reshape(1, -1), w_cat,
                                              rw_mu[l].reshape(1, -1), q_g, k_g, b_f, ones_bd, rw_cols, d_mix,
                                              tm_in)

        y_rw = _rwkv(xs_rw, rw_w_up[l], rw_w0[l], rw_a_up[l], rw_a0[l],
                     rw_k_k[l], rw_k_a[l], rw_r_k[l].reshape(-1), rw_gn_g[l], rw_gn_b[l], ones_bd,
                     RWKV_CHUNKS_PER_STEP if s % (RWKV_CHUNKS_PER_STEP * CHUNK) == 0 else 1)

        y_fox = lax.cond(_score_bound(q_g, k_g)[0, 0] <= SAFE_SCORE_BOUND,
                         functools.partial(_attention, tq=tq, online_max=False),
                         functools.partial(_attention, tq=tq, online_max=True), qa, ka, va)

        x2 = _outproj(x2, y_rw.reshape(b * s, d_rw), y_fox.reshape(b * s, d_fox), gate.reshape(b * s, d_mix),
                      w_out[l].astype(BF16), final_g.reshape(1, -1), l == depth - 1, tm_out)
    return x2.reshape(b, s, d_model)
```

```python
import functools
import math

import jax
import jax.numpy as jnp
from jax import lax
from jax.experimental import pallas as pl
from jax.experimental.pallas import tpu as pltpu

F32 = jnp.float32
BF16 = jnp.bfloat16

HEAD_DIM = 64
NORM_EPS = 1e-6
GN_EPS = 64e-5
CHUNK = 64
LOG2_CHUNK = CHUNK.bit_length() - 1
LOG2_BASE = 3
RWKV_CHUNKS_PER_STEP = 4
LANES = 128
MXU_DIM = 256
NEG_BIG = -1e30
ATTN_HEADS_PER_STEP = 4
ATTN_KEY_SUB = 256
ATTN_BLOCKS_PER_ITER = 2
V_ROWS = 80
SAFE_SCORE_BOUND = 40.0
VMEM_LIMIT = 56 * 1024 * 1024


def _dot(a, b):
    return jnp.dot(a, b, preferred_element_type=F32)


def _dot_nt(a, b):
    return lax.dot_general(a, b, (((1,), (1,)), ((), ())), preferred_element_type=F32)


def _dot_tn(a, b):
    return lax.dot_general(a, b, (((0,), (0,)), ((), ())), preferred_element_type=F32)


def _sigmoid(x):
    return 1.0 / (1.0 + jnp.exp(-x))


def _bf(x):
    return x.astype(BF16)


def _split2(x):
    hi = x.astype(BF16)
    return hi, (x - hi.astype(F32)).astype(BF16)


def _split3(x):
    hi = x.astype(BF16)
    r1 = x - hi.astype(F32)
    mid = r1.astype(BF16)
    lo = (r1 - mid.astype(F32)).astype(BF16)
    return hi.astype(F32), mid.astype(F32), lo.astype(F32)


def _stack_x3_rhs(b):
    bh, bl = _split2(b)
    return jnp.concatenate([bh, bl, bh], axis=0)


def _dot_x3(a, b_stacked):
    ah, al = _split2(a)
    return _dot(jnp.concatenate([ah, ah, al], axis=1), b_stacked)


def _dot_pieces(x, ones_bf16, pieces):
    out = None
    for _ in range(pieces):
        hi = x.astype(BF16)
        term = _dot(hi, ones_bf16)
        out = term if out is None else out + term
        x = x - hi.astype(F32)
    return out


def _cumsum_rows(tri_bf16, x):
    out = None
    for _ in range(3):
        hi = x.astype(BF16)
        term = _dot(tri_bf16, hi)
        out = term if out is None else out + term
        x = x - hi.astype(F32)
    return out


def _headsum(x, ones_bd, pieces=2):
    width = x.shape[1]
    return jnp.concatenate([_dot_pieces(x[:, j:j + MXU_DIM], ones_bd, pieces) for j in range(0, width, MXU_DIM)],
                           axis=1)


def _score_bound(q_g, k_g):
    amax = lambda g: jnp.max(jnp.abs(g), axis=-1, keepdims=True)
    return (HEAD_DIM ** 0.5) * amax(q_g) * amax(k_g)


def _inproj_fox_body(x_ref, g_ref, w_ref, mu_ref, qg_ref, kg_ref, bf_ref, ones_ref,
                     rw_ref, gate_ref, qa_ref, ka_ref, va_ref, carry_ref, shift_ref, *, n_heads):
    @pl.when(pl.program_id(1) == 0)
    def _():
        carry_ref[...] = jnp.zeros_like(carry_ref)
        shift_ref[...] = jnp.zeros_like(shift_ref)

    x = x_ref[0]
    tm = x.shape[0]
    d = n_heads * HEAD_DIM
    rw_cols, gate_cols = rw_ref.shape[-1], gate_ref.shape[-1]
    off_qkv, off_gate, off_fl = rw_cols, rw_cols + 3 * d, rw_cols + 3 * d + gate_cols
    y = x * lax.rsqrt(jnp.mean(x * x, axis=-1, keepdims=True) + NORM_EPS)
    h = (y * g_ref[...]).astype(BF16)

    t = _dot(h, w_ref[:, off_qkv:off_qkv + 3 * d])
    fl = _dot(h, w_ref[:, off_fl:off_fl + LANES])

    tiles = [(rw_ref, c0, c0) for c0 in range(0, rw_cols, MXU_DIM)]
    tiles += [(gate_ref, c0, off_gate + c0) for c0 in range(0, gate_cols, MXU_DIM)]

    row8 = lax.broadcasted_iota(jnp.int32, (8, MXU_DIM), 0)

    def project(tile):
        ref, dst, src = tile
        width = min(MXU_DIM, ref.shape[-1] - dst)
        out = _dot(h, w_ref[:, src:src + width])
        if ref is rw_ref:
            cols = slice(dst, dst + width)
            before = shift_ref[:, cols]
            shift_ref[:, cols] = out[tm - 1:tm, :]
            rolled = pltpu.roll(out, 1, axis=0)
            prev = jnp.concatenate([jnp.where(row8[:, :width] == 0, before, rolled[:8]), rolled[8:]], axis=0)
            out = out + (prev - out) * mu_ref[:, cols]
        ref[0, :, dst:dst + width] = out.astype(ref.dtype)

    n_lead = max(len(tiles) - n_heads, 0)
    for tile in tiles[:n_lead]:
        project(tile)

    ones_bd = ones_ref[...]
    inv_n = 1.0 / HEAD_DIM
    log2e = math.log2(math.e)

    def headnorm(z, g):
        ms = _headsum(z * z, ones_bd, pieces=1) * inv_n
        return (z * lax.rsqrt(ms + NORM_EPS)) * g

    q = headnorm(t[:, 0:d], qg_ref[...]) * (HEAD_DIM ** -0.5 * log2e)
    k = headnorm(t[:, d:2 * d], kg_ref[...])
    v = t[:, 2 * d:3 * d]
    neg_bound = -log2e * _score_bound(qg_ref[...], kg_ref[...])

    z = fl + bf_ref[...]
    log_f = jnp.minimum(z, 0.0) - jnp.log1p(jnp.exp(-jnp.abs(z)))
    blk = min(tm, MXU_DIM)
    row = lax.broadcasted_iota(jnp.int32, (blk, blk), 0)
    col = lax.broadcasted_iota(jnp.int32, (blk, blk), 1)
    tri = jnp.where(row >= col, 1.0, 0.0).astype(BF16)
    running, parts = carry_ref[...], []
    for r0 in range(0, tm, blk):
        parts.append(_cumsum_rows(tri, log_f[r0:r0 + blk]) + running)
        running = parts[-1][blk - 1:blk, :]
    cs = jnp.concatenate(parts, axis=0)
    carry_ref[...] = running
    c_hi, c_mid, c_lo = _split3(cs * log2e)

    n = n_heads
    lane = lax.broadcasted_iota(jnp.int32, (1, LANES), 1)
    is_data = lane < HEAD_DIM
    at = lambda off: lane == HEAD_DIM + off
    within = lambda lo, hi: (lane >= HEAD_DIM + lo) & (lane < HEAD_DIM + hi)
    pieces = [jnp.where(lane < n, c, 0.0) for c in (c_hi, c_mid, c_lo)]
    spread = lambda base: sum(pltpu.roll(c, HEAD_DIM + base + i * n, axis=1) for i, c in enumerate(pieces))
    q_extra = spread(0) + jnp.where(within(3 * n, 6 * n + 1), 1.0, 0.0)
    neg_c = -spread(3 * n)
    one_col = jnp.where(at(0), 1.0, 0.0)

    for h_idx in range(n_heads):
        if n_lead + h_idx < len(tiles):
            project(tiles[n_lead + h_idx])
        pair = slice((h_idx // 2) * LANES, (h_idx // 2 + 1) * LANES)
        shift = lambda z: z[:, pair] if h_idx % 2 == 0 else pltpu.roll(z[:, pair], HEAD_DIM, axis=1)
        k_const = jnp.where(at(6 * n), neg_bound,
                            jnp.where(at(h_idx) | at(n + h_idx) | at(2 * n + h_idx), 1.0, 0.0))
        own_c = at(3 * n + h_idx) | at(4 * n + h_idx) | at(5 * n + h_idx)
        k_extra = jnp.where(own_c, neg_c, k_const)
        qa_ref[0, h_idx] = jnp.where(is_data, shift(q), q_extra).astype(BF16)
        ka_ref[0, h_idx] = jnp.where(is_data, shift(k), k_extra).astype(BF16)
        v_t = jnp.where(is_data, shift(v), one_col).T.astype(BF16)
        for sub in range(tm // ATTN_KEY_SUB):
            va_ref[0, h_idx, sub] = v_t[:, sub * ATTN_KEY_SUB:(sub + 1) * ATTN_KEY_SUB]
    for tile in tiles[n_lead + n_heads:]:
        project(tile)


def _inproj_fox(x, g, w_cat, mu, q_g, k_g, b_f, ones_bd, rw_cols, gate_cols, tm):
    b, s, d_model = x.shape
    d = q_g.shape[-1]
    n_heads = d // HEAD_DIM
    const = lambda shape: pl.BlockSpec(shape, lambda i, j: (0,) * len(shape))
    rows = lambda width: pl.BlockSpec((1, tm, width), lambda i, j: (i, j, 0))
    head_spec = pl.BlockSpec((1, n_heads, tm, LANES), lambda i, j: (i, 0, j, 0))
    head_shape = jax.ShapeDtypeStruct((b, n_heads, s, LANES), BF16)
    return pl.pallas_call(
        functools.partial(_inproj_fox_body, n_heads=n_heads),
        grid=(b, s // tm),
        in_specs=[
            rows(d_model), const((1, d_model)),
            pl.BlockSpec(w_cat.shape, lambda i, j: (0, 0), pipeline_mode=pl.Buffered(1)),
            const((1, rw_cols)),
            const((1, d)), const((1, d)), const((1, LANES)), const((MXU_DIM, MXU_DIM)),
        ],
        out_specs=[rows(rw_cols), rows(gate_cols), head_spec, head_spec,
                   pl.BlockSpec((1, n_heads, tm // ATTN_KEY_SUB, LANES, ATTN_KEY_SUB),
                                lambda i, j: (i, 0, j, 0, 0))],
        out_shape=[jax.ShapeDtypeStruct((b, s, rw_cols), F32), jax.ShapeDtypeStruct((b, s, gate_cols), BF16),
                   head_shape, head_shape,
                   jax.ShapeDtypeStruct((b, n_heads, s // ATTN_KEY_SUB, LANES, ATTN_KEY_SUB), BF16)],
        scratch_shapes=[pltpu.VMEM((1, LANES), F32),
                        pltpu.VMEM((1, rw_cols), F32)],
        compiler_params=pltpu.CompilerParams(
            dimension_semantics=("arbitrary", "arbitrary"), vmem_limit_bytes=VMEM_LIMIT),
        name="inproj_fox",
    )(x, g, w_cat, mu, q_g, k_g, b_f, ones_bd)


def _rwkv_body(xs_ref, wup_ref, w0_ref, aup_ref, a0_ref, kk_ref, ka_ref, rk_ref, gng_ref, gnb_ref, ones_ref,
               y_ref, state_ref, m_ref, n_ref, rw_ref, yp_ref, bonus_ref, *, lora, n_chunks):
    @pl.when(pl.program_id(1) == 0)
    def _():
        state_ref[...] = jnp.zeros_like(state_ref)
        for ref in (m_ref, n_ref, rw_ref, yp_ref, bonus_ref):
            ref[...] = jnp.zeros_like(ref)

    L = CHUNK
    P = LANES
    d = y_ref.shape[-1]
    n_pairs = d // P
    ones_bd = ones_ref[...]
    chains = [(c, j) for c in range(n_chunks) for j in range(n_pairs)]
    row_l = lax.broadcasted_iota(jnp.int32, (L, L), 0)
    col_l = lax.broadcasted_iota(jnp.int32, (L, L), 1)
    tri = jnp.where(row_l >= col_l, 1.0, 0.0).astype(BF16)
    wup_x3 = _stack_x3_rhs(wup_ref[...])
    aup_x3 = _stack_x3_rhs(aup_ref[...])

    def state_pass(c):
        out = []
        for j in range(n_pairs):
            idx = c * n_pairs + j
            s0 = _bf(state_ref[j])
            out.append(_dot(rw_ref[idx], s0) + yp_ref[idx])
            state_ref[j] = _dot(m_ref[idx], s0) + n_ref[idx]
        return jnp.concatenate(out, axis=1)

    def prep(c):
        xs = xs_ref[0, c * L:(c + 1) * L, :]
        r = xs[:, 0:d]
        k = xs[:, d:2 * d]
        v = xs[:, 2 * d:3 * d]
        wl = xs[:, 3 * d:3 * d + lora]
        al = xs[:, 3 * d + lora:3 * d + 2 * lora]
        w_raw = w0_ref[...] + _dot_x3(jnp.tanh(wl), wup_x3)
        logw = (-math.exp(-0.5)) * _sigmoid(w_raw)
        a = _sigmoid(a0_ref[...] + _dot_x3(al, aup_x3))
        kkr = k * kk_ref[...]
        kk = kkr * lax.rsqrt(jnp.maximum(_headsum(kkr * kkr, ones_bd, pieces=1), 1e-24))
        k2 = k * (1.0 + (a - 1.0) * ka_ref[...])
        bonus = _headsum(r * k2 * rk_ref[...], ones_bd, pieces=1) * v
        cum = _cumsum_rows(tri, logw)
        cum_end = cum[L - 1:L, :]
        e_neg = jnp.exp(-cum)
        e_end = jnp.exp(cum_end - cum)
        kka = kk * a
        return dict(
            v=v, bonus=bonus,
            g_end=jnp.exp(cum_end),
            r_t=r * jnp.exp(cum),
            a_t=-kk * jnp.exp(cum - logw),
            b_t=kka * e_neg, k_t=k2 * e_neg,
            b_e=kka * e_end, k_e=k2 * e_end)

    ys, cur = [], []
    for c in range(n_chunks):
        ys.append(state_pass(c))
        cur.append(prep(c))

    y = jnp.concatenate(ys, axis=0)
    inv_n = 1.0 / HEAD_DIM
    mean = _headsum(y, ones_bd) * inv_n
    yc = y - mean
    var = _headsum(yc * yc, ones_bd) * inv_n
    y_ref[0] = ((yc * lax.rsqrt(var + GN_EPS)) * gng_ref[...] + gnb_ref[...] + bonus_ref[...]).astype(y_ref.dtype)
    bonus_ref[...] = jnp.concatenate([t["bonus"] for t in cur], axis=0)

    pair = lambda name, c, j: cur[c][name][:, j * P:(j + 1) * P]
    lane = lax.broadcasted_iota(jnp.int32, (L, P), 1)
    first = lane < HEAD_DIM
    stack = lambda z: jnp.concatenate([jnp.where(first, z, 0.0), jnp.where(first, 0.0, z)], axis=0)

    row = lax.broadcasted_iota(jnp.int32, (P, P), 0)
    col = lax.broadcasted_iota(jnp.int32, (P, P), 1)
    same = lambda s: (row >> s) == (col >> s)
    strict = same(LOG2_CHUNK) & (row > col)
    incl = same(LOG2_CHUNK) & (row >= col)
    eye = row == col

    amat = [_dot_nt(_bf(jnp.concatenate([stack(pair("a_t", c, j)), stack(pair("r_t", c, j))], axis=0)),
                    _bf(jnp.concatenate([pair("b_t", c, j)] * 2 + [pair("k_t", c, j)] * 2, axis=0)))
            for c, j in chains]
    n_st = [jnp.where(strict, m[:P, :P], 0.0) for m in amat]
    a_ak = [_bf(jnp.where(strict, m[:P, P:], 0.0)) for m in amat]
    a_r = [_bf(jnp.concatenate([jnp.where(incl, m[P:, :P], 0.0), jnp.where(incl, m[P:, P:], 0.0)], axis=1))
           for m in amat]

    n0_f = [jnp.where(same(LOG2_BASE), n, 0.0) for n in n_st]
    n0 = [_bf(n) for n in n0_f]
    n2 = [_bf(_dot(n, n)) for n in n0]
    n4 = [_bf(_dot(n, n)) for n in n2]
    t = [jnp.where(eye, 1.0, n) for n in n0_f]
    t = [x + _dot(_bf(x), m) for x, m in zip(t, n2)]
    t = [x + _dot(_bf(x), m) for x, m in zip(t, n4)]
    for s in range(LOG2_BASE, LOG2_CHUNK):
        off_mask = same(s + 1) & jnp.logical_not(same(s))
        n_off = [_bf(jnp.where(off_mask, n, 0.0)) for n in n_st]
        tb = [_bf(x) for x in t]
        half = [_bf(_dot(x, m)) for x, m in zip(tb, n_off)]
        t = [x + _dot(h, xb) for x, h, xb in zip(t, half, tb)]
    tb = [_bf(x) for x in t]

    v_st = [_bf(stack(pair("v", c, j))) for c, j in chains]
    x_st = [_dot(m, vs) for m, vs in zip(a_ak, v_st)]
    wu = [_dot(x, jnp.concatenate([_bf(stack(pair("a_t", c, j))), _bf(xs_)], axis=1))
          for x, xs_, (c, j) in zip(tb, x_st, chains)]
    zeros_b = jnp.zeros((P, P), BF16)
    rhs2 = [jnp.concatenate([_bf(w), jnp.concatenate([zeros_b, vs], axis=1)], axis=0)
            for w, vs in zip(wu, v_st)]
    top = [_dot(m, rh) for m, rh in zip(a_r, rhs2)]
    bot = [_dot_tn(_bf(jnp.concatenate([stack(pair("b_e", c, j)), stack(pair("k_e", c, j))], axis=0)), rh)
           for rh, (c, j) in zip(rhs2, chains)]

    fold = lambda z: z[:L] + z[L:]
    for idx, (c, j) in enumerate(chains):
        rw_ref[idx] = _bf(pair("r_t", c, j) + fold(top[idx][:, :P]))
        yp_ref[idx] = fold(top[idx][:, P:])
        m_ref[idx] = _bf(jnp.where(eye, pair("g_end", c, j), 0.0) + bot[idx][:, :P])
        n_ref[idx] = bot[idx][:, P:]


def _rwkv(xs_rw, w_up, w0, a_up, a0, k_k, k_a, r_k, gn_g, gn_b, ones_bd, n_chunks):
    b, s, cols = xs_rw.shape
    lora, d = w_up.shape
    rows = n_chunks * CHUNK
    n_tiles = s // rows
    n_chains = n_chunks * (d // LANES)
    row = lambda a: a.reshape(1, -1)
    const = lambda shape: pl.BlockSpec(shape, lambda i, j: (0,) * len(shape))
    return pl.pallas_call(
        functools.partial(_rwkv_body, lora=lora, n_chunks=n_chunks),
        grid=(b, n_tiles + 1),
        in_specs=[
            pl.BlockSpec((1, rows, cols), lambda i, j: (i, jnp.minimum(j, n_tiles - 1), 0)),
            const((lora, d)), const((1, d)), const((lora, d)), const((1, d)),
            const((1, d)), const((1, d)), const((1, d)), const((1, d)), const((1, d)), const((MXU_DIM, MXU_DIM)),
        ],
        out_specs=pl.BlockSpec((1, rows, d), lambda i, j: (i, jnp.maximum(j - 1, 0), 0)),
        out_shape=jax.ShapeDtypeStruct((b, s, d), BF16),
        scratch_shapes=[
            pltpu.VMEM((d // LANES, LANES, LANES), F32),
            pltpu.VMEM((n_chains, LANES, LANES), BF16),
            pltpu.VMEM((n_chains, LANES, LANES), F32),
            pltpu.VMEM((n_chains, CHUNK, LANES), BF16),
            pltpu.VMEM((n_chains, CHUNK, LANES), F32),
            pltpu.VMEM((rows, d), F32),
        ],
        compiler_params=pltpu.CompilerParams(
            dimension_semantics=("arbitrary", "arbitrary"), vmem_limit_bytes=VMEM_LIMIT),
        name="rwkv7",
    )(xs_rw, w_up, row(w0), a_up, row(a0), row(k_k), row(k_a), row(r_k), row(gn_g), row(gn_b), ones_bd)


def _attn_body(q_ref, k_ref, vt_ref, o_ref, acc_ref, *, heads_per_step, tq, online_max):
    qi = pl.program_id(2)
    tk = vt_ref.shape[-1]
    n_sub = tq // tk
    heads = range(heads_per_step)
    qs = [q_ref[0, hh] for hh in heads]

    items = [(u, hh) for u in range(n_sub) for hh in heads]
    zero = jnp.zeros((V_ROWS, tq), F32)

    def scores(jb, u, hh, diag):
        kb = k_ref[0, hh, pl.ds(pl.multiple_of((jb * n_sub + u) * tk, tk), tk), :]
        if not diag:
            return _dot_nt(kb, qs[hh])
        st = _dot_nt(kb, qs[hh][u * tk:, :])
        key = lax.broadcasted_iota(jnp.int32, st.shape, 0)
        qry = lax.broadcasted_iota(jnp.int32, st.shape, 1)
        return jnp.where(key <= qry, st, NEG_BIG)

    def values(sub_block, hh):
        return vt_ref[0, hh, sub_block, 0:V_ROWS, :]

    def add_cols(acc, part):
        off = acc.shape[1] - part.shape[1]
        return acc + part if off == 0 else jnp.concatenate([acc[:, :off], acc[:, off:] + part], axis=1)

    if online_max:
        def block(jb, carry, diag):
            new = list(carry)
            for u, hh in items:
                st = scores(jb, u, hh, diag)
                off = tq - st.shape[1]
                m, acc = new[hh]
                m_new = jnp.maximum(m[:, off:], jnp.max(st, axis=0, keepdims=True))
                m_new = m_new if off == 0 else jnp.concatenate([m[:, :off], m_new], axis=1)
                p = jnp.exp2(st - m_new[:, off:]).astype(BF16)
                new[hh] = (m_new, add_cols(acc * jnp.exp2(m - m_new), _dot(values(jb * n_sub + u, hh), p)))
            return tuple(new)

        init = tuple((jnp.full((1, tq), NEG_BIG, F32), zero) for _ in heads)
        carry = lax.fori_loop(0, qi, functools.partial(block, diag=False), init)
        accs = [c[1] for c in block(qi, carry, True)]
    else:
        def blocks(jbs, carry, diag):
            work = [(jb, u, hh) for jb in jbs for u, hh in items]
            tiles = [scores(jb, u, hh, diag) for jb, u, hh in work]
            probs = [jnp.exp2(st).astype(BF16) for st in tiles]
            new = list(carry)
            for (jb, u, hh), p in zip(work, probs):
                new[hh] = add_cols(new[hh], _dot(values(jb * n_sub + u, hh), p))
            return tuple(new)

        def span(jb0, n_blocks, carry):
            n_keys = n_blocks * tq
            start = pl.multiple_of(jb0 * tq, tq)
            tiles = [_dot_nt(k_ref[0, hh, pl.ds(start, n_keys), :], qs[hh]) for hh in heads]
            probs = [jnp.exp2(st).astype(BF16) for st in tiles]
            for hh in heads:
                vt = jnp.concatenate([values(jb0 * n_sub + u, hh) for u in range(n_blocks * n_sub)], axis=1)
                acc_ref[hh] = acc_ref[hh] + _dot(vt, probs[hh])
            return carry

        acc_ref[...] = jnp.zeros_like(acc_ref)
        per_iter = ATTN_BLOCKS_PER_ITER
        lax.fori_loop(0, qi // per_iter, lambda i, c: span(i * per_iter, per_iter, c), 0)
        first_left = (qi // per_iter) * per_iter
        lax.fori_loop(first_left, qi, lambda jb, c: span(jb, 1, c), 0)
        accs = blocks([qi], tuple(acc_ref[hh] for hh in heads), True)
    out_t = jnp.concatenate([acc[:HEAD_DIM] / acc[HEAD_DIM:HEAD_DIM + 1] for acc in accs], axis=0)
    o_ref[0] = out_t.astype(o_ref.dtype).T


def _attention(qa, ka, vt, tq, online_max):
    b, h, s, _ = qa.shape
    hps = ATTN_HEADS_PER_STEP if h % ATTN_HEADS_PER_STEP == 0 else LANES // HEAD_DIM
    return pl.pallas_call(
        functools.partial(_attn_body, heads_per_step=hps, tq=tq, online_max=online_max),
        grid=(b, h // hps, s // tq),
        in_specs=[
            pl.BlockSpec((1, hps, tq, LANES), lambda i, g, j: (i, g, j, 0)),
            pl.BlockSpec((1, hps, s, LANES), lambda i, g, j: (i, g, 0, 0)),
            pl.BlockSpec((1, hps) + vt.shape[2:], lambda i, g, j: (i, g, 0, 0, 0)),
        ],
        out_specs=pl.BlockSpec((1, tq, hps * HEAD_DIM), lambda i, g, j: (i, j, g)),
        out_shape=jax.ShapeDtypeStruct((b, s, h * HEAD_DIM), BF16),
        scratch_shapes=[pltpu.VMEM((hps, V_ROWS, tq), F32)],
        compiler_params=pltpu.CompilerParams(
            dimension_semantics=("parallel", "parallel", "arbitrary"), vmem_limit_bytes=VMEM_LIMIT),
        name="fox_attention_online_max" if online_max else "fox_attention",
    )(qa, ka, vt)


def _outproj_body(x_ref, yrw_ref, yfox_ref, gate_ref, w_ref, fg_ref, o_ref, *, final):
    g = gate_ref[...].astype(F32)
    sg = g * _sigmoid(g)
    d_rw = yrw_ref.shape[1]
    y_rw = (yrw_ref[...].astype(F32) * sg[:, :d_rw]).astype(BF16)
    y_fox = (yfox_ref[...].astype(F32) * sg[:, d_rw:]).astype(BF16)
    z = x_ref[...] + _dot(y_rw, w_ref[0:d_rw, :]) + _dot(y_fox, w_ref[d_rw:, :])
    if final:
        z = (z * lax.rsqrt(jnp.mean(z * z, axis=-1, keepdims=True) + NORM_EPS)) * fg_ref[...]
    o_ref[...] = z


def _outproj(x2, y_rw, y_fox, gate, w_out, final_g, final, tm):
    n, d = x2.shape
    d_rw, d_fox = y_rw.shape[1], y_fox.shape[1]
    return pl.pallas_call(
        functools.partial(_outproj_body, final=final),
        grid=(n // tm,),
        in_specs=[
            pl.BlockSpec((tm, d), lambda i: (i, 0)),
            pl.BlockSpec((tm, d_rw), lambda i: (i, 0)),
            pl.BlockSpec((tm, d_fox), lambda i: (i, 0)),
            pl.BlockSpec((tm, d_rw + d_fox), lambda i: (i, 0)),
            pl.BlockSpec(w_out.shape, lambda i: (0, 0)),
            pl.BlockSpec((1, d), lambda i: (0, 0)),
        ],
        out_specs=pl.BlockSpec((tm, d), lambda i: (i, 0)),
        out_shape=jax.ShapeDtypeStruct((n, d), F32),
        compiler_params=pltpu.CompilerParams(
            dimension_semantics=("parallel",), vmem_limit_bytes=VMEM_LIMIT),
        name="outproj",
    )(x2, y_rw, y_fox, gate, w_out, final_g)


def _pick_tile(n, target):
    t = min(n, target)
    while n % t:
        t //= 2
    return t


def kernel(x, norm_g, w_in, rw_mu, rw_w_up, rw_w0, rw_a_up, rw_a0, rw_k_k, rw_k_a, rw_r_k, rw_gn_g, rw_gn_b,
           fox_q_g, fox_k_g, fox_b_f, w_out, final_g):
    b, s, d_model = x.shape
    depth = norm_g.shape[0]
    d_rw = rw_w0.shape[-1]
    lora = rw_w_up.shape[1]
    d_mix = w_out.shape[1]
    d_fox = d_mix - d_rw
    h_fox = fox_b_f.shape[-1]
    rw_cols = 3 * d_rw + 2 * lora
    fox_cols = 3 * d_fox + h_fox
    assert d_rw == d_fox and s % CHUNK == 0 and h_fox <= LANES

    head = lax.broadcasted_iota(jnp.int32, (MXU_DIM, MXU_DIM), 0) // HEAD_DIM
    ones_bd = (head == head.T).astype(BF16)

    tm_out = _pick_tile(b * s, 1024)
    tm_in = _pick_tile(s, 512)
    tq = _pick_tile(s, 512)

    x2 = x.reshape(b * s, d_model)
    for l in range(depth):
        w = w_in[l]
        w_fl = jnp.pad(w[:, rw_cols + 3 * d_fox:rw_cols + fox_cols], ((0, 0), (0, LANES - h_fox)))
        w_cat = jnp.concatenate(
            [w[:, :rw_cols], w[:, rw_cols:rw_cols + 3 * d_fox], w[:, rw_cols + fox_cols:], w_fl], axis=1
        ).astype(BF16)
        tile_heads = lambda g: jnp.tile(g, d_fox // HEAD_DIM).reshape(1, -1)
        b_f = jnp.pad(fox_b_f[l], (0, LANES - h_fox)).reshape(1, -1)
        q_g, k_g = tile_heads(fox_q_g[l]), tile_heads(fox_k_g[l])
        xs_rw, gate, qa, ka, va = _inproj_fox(x2.reshape(b, s, d_model), norm_g[l].reshape(1, -1), w_cat,
                                              rw_mu[l].reshape(1, -1), q_g, k_g, b_f, ones_bd, rw_cols, d_mix,
                                              tm_in)

        y_rw = _rwkv(xs_rw, rw_w_up[l], rw_w0[l], rw_a_up[l], rw_a0[l],
                     rw_k_k[l], rw_k_a[l], rw_r_k[l].reshape(-1), rw_gn_g[l], rw_gn_b[l], ones_bd,
                     RWKV_CHUNKS_PER_STEP if s % (RWKV_CHUNKS_PER_STEP * CHUNK) == 0 else 1)

        y_fox = lax.cond(_score_bound(q_g, k_g)[0, 0] <= SAFE_SCORE_BOUND,
                         functools.partial(_attention, tq=tq, online_max=False),
                         functools.partial(_attention, tq=tq, online_max=True), qa, ka, va)

        x2 = _outproj(x2, y_rw.reshape(b * s, d_rw), y_fox.reshape(b * s, d_fox), gate.reshape(b * s, d_mix),
                      w_out[l].astype(BF16), final_g.reshape(1, -1), l == depth - 1, tm_out)
    return x2.reshape(b, s, d_model)
```

```python
import functools
import math

import jax
import jax.numpy as jnp
from jax import lax
from jax.experimental import pallas as pl
from jax.experimental.pallas import tpu as pltpu

F32 = jnp.float32
BF16 = jnp.bfloat16

HEAD_DIM = 64
NORM_EPS = 1e-6
GN_EPS = 64e-5
CHUNK = 64
LOG2_CHUNK = CHUNK.bit_length() - 1
LOG2_BASE = 3
RWKV_CHUNKS_PER_STEP = 4
LANES = 128
MXU_DIM = 256
NEG_BIG = -1e30
ATTN_HEADS_PER_STEP = 4
ATTN_KEY_SUB = 256
ATTN_BLOCKS_PER_ITER = 2
V_ROWS = 80
SAFE_SCORE_BOUND = 40.0
VMEM_LIMIT = 56 * 1024 * 1024


def _dot(a, b):
    return jnp.dot(a, b, preferred_element_type=F32)


def _dot_nt(a, b):
    return lax.dot_general(a, b, (((1,), (1,)), ((), ())), preferred_element_type=F32)


def _dot_tn(a, b):
    return lax.dot_general(a, b, (((0,), (0,)), ((), ())), preferred_element_type=F32)


def _sigmoid(x):
    return 1.0 / (1.0 + jnp.exp(-x))


def _bf(x):
    return x.astype(BF16)


def _split2(x):
    hi = x.astype(BF16)
    return hi, (x - hi.astype(F32)).astype(BF16)


def _split3(x):
    hi = x.astype(BF16)
    r1 = x - hi.astype(F32)
    mid = r1.astype(BF16)
    lo = (r1 - mid.astype(F32)).astype(BF16)
    return hi.astype(F32), mid.astype(F32), lo.astype(F32)


def _stack_x3_rhs(b):
    bh, bl = _split2(b)
    return jnp.concatenate([bh, bl, bh], axis=0)


def _dot_x3(a, b_stacked):
    ah, al = _split2(a)
    return _dot(jnp.concatenate([ah, ah, al], axis=1), b_stacked)


def _dot_pieces(x, ones_bf16, pieces):
    out = None
    for _ in range(pieces):
        hi = x.astype(BF16)
        term = _dot(hi, ones_bf16)
        out = term if out is None else out + term
        x = x - hi.astype(F32)
    return out


def _cumsum_rows(tri_bf16, x):
    out = None
    for _ in range(3):
        hi = x.astype(BF16)
        term = _dot(tri_bf16, hi)
        out = term if out is None else out + term
        x = x - hi.astype(F32)
    return out


def _headsum(x, ones_bd, pieces=2):
    width = x.shape[1]
    return jnp.concatenate([_dot_pieces(x[:, j:j + MXU_DIM], ones_bd, pieces) for j in range(0, width, MXU_DIM)],
                           axis=1)


def _score_bound(q_g, k_g):
    amax = lambda g: jnp.max(jnp.abs(g), axis=-1, keepdims=True)
    return (HEAD_DIM ** 0.5) * amax(q_g) * amax(k_g)


def _inproj_fox_body(x_ref, g_ref, w_ref, mu_ref, qg_ref, kg_ref, bf_ref, ones_ref,
                     rw_ref, gate_ref, qa_ref, ka_ref, va_ref, carry_ref, shift_ref, *, n_heads):
    @pl.when(pl.program_id(1) == 0)
    def _():
        carry_ref[...] = jnp.zeros_like(carry_ref)
        shift_ref[...] = jnp.zeros_like(shift_ref)

    x = x_ref[0]
    tm = x.shape[0]
    d = n_heads * HEAD_DIM
    rw_cols, gate_cols = rw_ref.shape[-1], gate_ref.shape[-1]
    off_fl, off_qkv, off_gate = rw_cols, rw_cols + LANES, rw_cols + LANES + 3 * d
    y = x * lax.rsqrt(jnp.mean(x * x, axis=-1, keepdims=True) + NORM_EPS)
    h = (y * g_ref[...]).astype(BF16)

    row8 = lax.broadcasted_iota(jnp.int32, (8, MXU_DIM), 0)

    def finish(ref, dst, out):
        width = out.shape[1]
        if ref is rw_ref:
            cols = slice(dst, dst + width)
            before = shift_ref[:, cols]
            shift_ref[:, cols] = out[tm - 1:tm, :]
            rolled = pltpu.roll(out, 1, axis=0)
            prev = jnp.concatenate([jnp.where(row8[:, :width] == 0, before, rolled[:8]), rolled[8:]], axis=0)
            out = out + (prev - out) * mu_ref[:, cols]
        ref[0, :, dst:dst + width] = out.astype(ref.dtype)

    def project(tile):
        ref, dst, src = tile
        width = min(MXU_DIM, ref.shape[-1] - dst)
        finish(ref, dst, _dot(h, w_ref[:, src:src + width]))

    t = _dot(h, w_ref[:, off_qkv:off_qkv + 3 * d])
    tail = rw_cols % MXU_DIM
    if tail and tail + LANES <= MXU_DIM:
        merged = _dot(h, w_ref[:, rw_cols - tail:rw_cols + LANES])
        finish(rw_ref, rw_cols - tail, merged[:, :tail])
        fl = merged[:, tail:]
    else:
        tail = 0
        fl = _dot(h, w_ref[:, off_fl:off_fl + LANES])

    tiles = [(rw_ref, c0, c0) for c0 in range(0, rw_cols - tail, MXU_DIM)]
    tiles += [(gate_ref, c0, off_gate + c0) for c0 in range(0, gate_cols, MXU_DIM)]

    n_lead = max(len(tiles) - n_heads, 0)
    for tile in tiles[:n_lead]:
        project(tile)

    ones_bd = ones_ref[...]
    inv_n = 1.0 / HEAD_DIM
    log2e = math.log2(math.e)

    def headnorm(z, g):
        ms = _headsum(z * z, ones_bd, pieces=1) * inv_n
        return (z * lax.rsqrt(ms + NORM_EPS)) * g

    q = headnorm(t[:, 0:d], qg_ref[...]) * (HEAD_DIM ** -0.5 * log2e)
    k = headnorm(t[:, d:2 * d], kg_ref[...])
    v = t[:, 2 * d:3 * d]
    neg_bound = -log2e * _score_bound(qg_ref[...], kg_ref[...])

    z = fl + bf_ref[...]
    log_f = jnp.minimum(z, 0.0) - jnp.log1p(jnp.exp(-jnp.abs(z)))
    blk = min(tm, MXU_DIM)
    row = lax.broadcasted_iota(jnp.int32, (blk, blk), 0)
    col = lax.broadcasted_iota(jnp.int32, (blk, blk), 1)
    tri = jnp.where(row >= col, 1.0, 0.0).astype(BF16)
    running, parts = carry_ref[...], []
    for r0 in range(0, tm, blk):
        parts.append(_cumsum_rows(tri, log_f[r0:r0 + blk]) + running)
        running = parts[-1][blk - 1:blk, :]
    cs = jnp.concatenate(parts, axis=0)
    carry_ref[...] = running
    c_hi, c_mid, c_lo = _split3(cs * log2e)

    n = n_heads
    lane = lax.broadcasted_iota(jnp.int32, (1, LANES), 1)
    is_data = lane < HEAD_DIM
    at = lambda off: lane == HEAD_DIM + off
    within = lambda lo, hi: (lane >= HEAD_DIM + lo) & (lane < HEAD_DIM + hi)
    pieces = [jnp.where(lane < n, c, 0.0) for c in (c_hi, c_mid, c_lo)]
    spread = lambda base: sum(pltpu.roll(c, HEAD_DIM + base + i * n, axis=1) for i, c in enumerate(pieces))
    q_extra = spread(0) + jnp.where(within(3 * n, 6 * n + 1), 1.0, 0.0)
    neg_c = -spread(3 * n)
    one_col = jnp.where(at(0), 1.0, 0.0)

    for h_idx in range(n_heads):
        if n_lead + h_idx < len(tiles):
            project(tiles[n_lead + h_idx])
        pair = slice((h_idx // 2) * LANES, (h_idx // 2 + 1) * LANES)
        shift = lambda z: z[:, pair] if h_idx % 2 == 0 else pltpu.roll(z[:, pair], HEAD_DIM, axis=1)
        k_const = jnp.where(at(6 * n), neg_bound,
                            jnp.where(at(h_idx) | at(n + h_idx) | at(2 * n + h_idx), 1.0, 0.0))
        own_c = at(3 * n + h_idx) | at(4 * n + h_idx) | at(5 * n + h_idx)
        k_extra = jnp.where(own_c, neg_c, k_const)
        qa_ref[0, h_idx] = jnp.where(is_data, shift(q), q_extra).astype(BF16)
        ka_ref[0, h_idx] = jnp.where(is_data, shift(k), k_extra).astype(BF16)
        v_t = jnp.where(is_data, shift(v), one_col).T.astype(BF16)
        for sub in range(tm // ATTN_KEY_SUB):
            va_ref[0, h_idx, sub] = v_t[:, sub * ATTN_KEY_SUB:(sub + 1) * ATTN_KEY_SUB]
    for tile in tiles[n_lead + n_heads:]:
        project(tile)


def _inproj_fox(x, g, w_cat, mu, q_g, k_g, b_f, ones_bd, rw_cols, gate_cols, tm):
    b, s, d_model = x.shape
    d = q_g.shape[-1]
    n_heads = d // HEAD_DIM
    const = lambda shape: pl.BlockSpec(shape, lambda i, j: (0,) * len(shape))
    rows = lambda width: pl.BlockSpec((1, tm, width), lambda i, j: (i, j, 0))
    head_spec = pl.BlockSpec((1, n_heads, tm, LANES), lambda i, j: (i, 0, j, 0))
    head_shape = jax.ShapeDtypeStruct((b, n_heads, s, LANES), BF16)
    return pl.pallas_call(
        functools.partial(_inproj_fox_body, n_heads=n_heads),
        grid=(b, s // tm),
        in_specs=[
            rows(d_model), const((1, d_model)),
            pl.BlockSpec(w_cat.shape, lambda i, j: (0, 0), pipeline_mode=pl.Buffered(1)),
            const((1, rw_cols)),
            const((1, d)), const((1, d)), const((1, LANES)), const((MXU_DIM, MXU_DIM)),
        ],
        out_specs=[rows(rw_cols), rows(gate_cols), head_spec, head_spec,
                   pl.BlockSpec((1, n_heads, tm // ATTN_KEY_SUB, LANES, ATTN_KEY_SUB),
                                lambda i, j: (i, 0, j, 0, 0))],
        out_shape=[jax.ShapeDtypeStruct((b, s, rw_cols), F32), jax.ShapeDtypeStruct((b, s, gate_cols), BF16),
                   head_shape, head_shape,
                   jax.ShapeDtypeStruct((b, n_heads, s // ATTN_KEY_SUB, LANES, ATTN_KEY_SUB), BF16)],
        scratch_shapes=[pltpu.VMEM((1, LANES), F32),
                        pltpu.VMEM((1, rw_cols), F32)],
        compiler_params=pltpu.CompilerParams(
            dimension_semantics=("arbitrary", "arbitrary"), vmem_limit_bytes=VMEM_LIMIT),
        name="inproj_fox",
    )(x, g, w_cat, mu, q_g, k_g, b_f, ones_bd)


def _rwkv_body(xs_ref, wup_ref, w0_ref, aup_ref, a0_ref, kk_ref, ka_ref, rk_ref, gng_ref, gnb_ref, ones_ref,
               y_ref, state_ref, m_ref, n_ref, rw_ref, yp_ref, bonus_ref, *, lora, n_chunks):
    @pl.when(pl.program_id(1) == 0)
    def _():
        state_ref[...] = jnp.zeros_like(state_ref)
        for ref in (m_ref, n_ref, rw_ref, yp_ref, bonus_ref):
            ref[...] = jnp.zeros_like(ref)

    L = CHUNK
    P = LANES
    d = y_ref.shape[-1]
    n_pairs = d // P
    ones_bd = ones_ref[...]
    chains = [(c, j) for c in range(n_chunks) for j in range(n_pairs)]
    row_l = lax.broadcasted_iota(jnp.int32, (L, L), 0)
    col_l = lax.broadcasted_iota(jnp.int32, (L, L), 1)
    tri = jnp.where(row_l >= col_l, 1.0, 0.0).astype(BF16)
    wup_x3 = _stack_x3_rhs(wup_ref[...])
    aup_x3 = _stack_x3_rhs(aup_ref[...])

    def state_pass(c):
        out = []
        for j in range(n_pairs):
            idx = c * n_pairs + j
            s0 = _bf(state_ref[j])
            out.append(_dot(rw_ref[idx], s0) + yp_ref[idx])
            state_ref[j] = _dot(m_ref[idx], s0) + n_ref[idx]
        return jnp.concatenate(out, axis=1)

    def prep(c):
        xs = xs_ref[0, c * L:(c + 1) * L, :]
        r = xs[:, 0:d]
        k = xs[:, d:2 * d]
        v = xs[:, 2 * d:3 * d]
        wl = xs[:, 3 * d:3 * d + lora]
        al = xs[:, 3 * d + lora:3 * d + 2 * lora]
        w_raw = w0_ref[...] + _dot_x3(jnp.tanh(wl), wup_x3)
        logw = (-math.exp(-0.5)) * _sigmoid(w_raw)
        a = _sigmoid(a0_ref[...] + _dot_x3(al, aup_x3))
        kkr = k * kk_ref[...]
        kk = kkr * lax.rsqrt(jnp.maximum(_headsum(kkr * kkr, ones_bd, pieces=1), 1e-24))
        k2 = k * (1.0 + (a - 1.0) * ka_ref[...])
        bonus = _headsum(r * k2 * rk_ref[...], ones_bd, pieces=1) * v
        cum = _cumsum_rows(tri, logw)
        cum_end = cum[L - 1:L, :]
        e_neg = jnp.exp(-cum)
        e_end = jnp.exp(cum_end - cum)
        kka = kk * a
        return dict(
            v=v, bonus=bonus,
            g_end=jnp.exp(cum_end),
            r_t=r * jnp.exp(cum),
            a_t=-kk * jnp.exp(cum - logw),
            b_t=kka * e_neg, k_t=k2 * e_neg,
            b_e=kka * e_end, k_e=k2 * e_end)

    ys, cur = [], []
    for c in range(n_chunks):
        ys.append(state_pass(c))
        cur.append(prep(c))

    y = jnp.concatenate(ys, axis=0)
    inv_n = 1.0 / HEAD_DIM
    mean = _headsum(y, ones_bd, pieces=1) * inv_n
    yc = y - mean
    var = _headsum(yc * yc, ones_bd, pieces=1) * inv_n
    y_ref[0] = ((yc * lax.rsqrt(var + GN_EPS)) * gng_ref[...] + gnb_ref[...] + bonus_ref[...]).astype(y_ref.dtype)
    bonus_ref[...] = jnp.concatenate([t["bonus"] for t in cur], axis=0)

    pair = lambda name, c, j: cur[c][name][:, j * P:(j + 1) * P]
    lane = lax.broadcasted_iota(jnp.int32, (L, P), 1)
    first = lane < HEAD_DIM
    stack = lambda z: jnp.concatenate([jnp.where(first, z, 0.0), jnp.where(first, 0.0, z)], axis=0)

    row = lax.broadcasted_iota(jnp.int32, (P, P), 0)
    col = lax.broadcasted_iota(jnp.int32, (P, P), 1)
    same = lambda s: (row >> s) == (col >> s)
    strict = same(LOG2_CHUNK) & (row > col)
    incl = same(LOG2_CHUNK) & (row >= col)
    eye = row == col

    amat = [_dot_nt(_bf(jnp.concatenate([stack(pair("a_t", c, j)), stack(pair("r_t", c, j))], axis=0)),
                    _bf(jnp.concatenate([pair("b_t", c, j)] * 2 + [pair("k_t", c, j)] * 2, axis=0)))
            for c, j in chains]
    n_st = [jnp.where(strict, m[:P, :P], 0.0) for m in amat]
    a_ak = [_bf(jnp.where(strict, m[:P, P:], 0.0)) for m in amat]
    a_r = [_bf(jnp.concatenate([jnp.where(incl, m[P:, :P], 0.0), jnp.where(incl, m[P:, P:], 0.0)], axis=1))
           for m in amat]

    n0_f = [jnp.where(same(LOG2_BASE), n, 0.0) for n in n_st]
    n0 = [_bf(n) for n in n0_f]
    n2 = [_bf(_dot(n, n)) for n in n0]
    n4 = [_bf(_dot(n, n)) for n in n2]
    t = [jnp.where(eye, 1.0, n) for n in n0_f]
    t = [x + _dot(_bf(x), m) for x, m in zip(t, n2)]
    t = [x + _dot(_bf(x), m) for x, m in zip(t, n4)]
    for s in range(LOG2_BASE, LOG2_CHUNK):
        off_mask = same(s + 1) & jnp.logical_not(same(s))
        n_off = [_bf(jnp.where(off_mask, n, 0.0)) for n in n_st]
        tb = [_bf(x) for x in t]
        half = [_bf(_dot(x, m)) for x, m in zip(tb, n_off)]
        t = [x + _dot(h, xb) for x, h, xb in zip(t, half, tb)]
    tb = [_bf(x) for x in t]

    v_st = [_bf(stack(pair("v", c, j))) for c, j in chains]
    x_st = [_dot(m, vs) for m, vs in zip(a_ak, v_st)]
    wu = [_dot(x, jnp.concatenate([_bf(stack(pair("a_t", c, j))), _bf(xs_)], axis=1))
          for x, xs_, (c, j) in zip(tb, x_st, chains)]
    zeros_b = jnp.zeros((P, P), BF16)
    rhs2 = [jnp.concatenate([_bf(w), jnp.concatenate([zeros_b, vs], axis=1)], axis=0)
            for w, vs in zip(wu, v_st)]
    top = [_dot(m, rh) for m, rh in zip(a_r, rhs2)]
    bot = [_dot_tn(_bf(jnp.concatenate([stack(pair("b_e", c, j)), stack(pair("k_e", c, j))], axis=0)), rh)
           for rh, (c, j) in zip(rhs2, chains)]

    fold = lambda z: z[:L] + z[L:]
    for idx, (c, j) in enumerate(chains):
        rw_ref[idx] = _bf(pair("r_t", c, j) + fold(top[idx][:, :P]))
        yp_ref[idx] = fold(top[idx][:, P:])
        m_ref[idx] = _bf(jnp.where(eye, pair("g_end", c, j), 0.0) + bot[idx][:, :P])
        n_ref[idx] = bot[idx][:, P:]


def _rwkv(xs_rw, w_up, w0, a_up, a0, k_k, k_a, r_k, gn_g, gn_b, ones_bd, n_chunks):
    b, s, cols = xs_rw.shape
    lora, d = w_up.shape
    rows = n_chunks * CHUNK
    n_tiles = s // rows
    n_chains = n_chunks * (d // LANES)
    row = lambda a: a.reshape(1, -1)
    const = lambda shape: pl.BlockSpec(shape, lambda i, j: (0,) * len(shape))
    return pl.pallas_call(
        functools.partial(_rwkv_body, lora=lora, n_chunks=n_chunks),
        grid=(b, n_tiles + 1),
        in_specs=[
            pl.BlockSpec((1, rows, cols), lambda i, j: (i, jnp.minimum(j, n_tiles - 1), 0)),
            const((lora, d)), const((1, d)), const((lora, d)), const((1, d)),
            const((1, d)), const((1, d)), const((1, d)), const((1, d)), const((1, d)), const((MXU_DIM, MXU_DIM)),
        ],
        out_specs=pl.BlockSpec((1, rows, d), lambda i, j: (i, jnp.maximum(j - 1, 0), 0)),
        out_shape=jax.ShapeDtypeStruct((b, s, d), BF16),
        scratch_shapes=[
            pltpu.VMEM((d // LANES, LANES, LANES), F32),
            pltpu.VMEM((n_chains, LANES, LANES), BF16),
            pltpu.VMEM((n_chains, LANES, LANES), F32),
            pltpu.VMEM((n_chains, CHUNK, LANES), BF16),
            pltpu.VMEM((n_chains, CHUNK, LANES), F32),
            pltpu.VMEM((rows, d), F32),
        ],
        compiler_params=pltpu.CompilerParams(
            dimension_semantics=("arbitrary", "arbitrary"), vmem_limit_bytes=VMEM_LIMIT),
        name="rwkv7",
    )(xs_rw, w_up, row(w0), a_up, row(a0), row(k_k), row(k_a), row(r_k), row(gn_g), row(gn_b), ones_bd)


def _attn_body(q_ref, k_ref, vt_ref, o_ref, acc_ref, *, heads_per_step, tq, online_max):
    qi = pl.program_id(2)
    tk = vt_ref.shape[-1]
    n_sub = tq // tk
    heads = range(heads_per_step)
    qs = [q_ref[0, hh] for hh in heads]

    items = [(u, hh) for u in range(n_sub) for hh in heads]
    zero = jnp.zeros((V_ROWS, tq), F32)

    def scores(jb, u, hh, diag):
        kb = k_ref[0, hh, pl.ds(pl.multiple_of((jb * n_sub + u) * tk, tk), tk), :]
        if not diag:
            return _dot_nt(kb, qs[hh])
        st = _dot_nt(kb, qs[hh][u * tk:, :])
        key = lax.broadcasted_iota(jnp.int32, st.shape, 0)
        qry = lax.broadcasted_iota(jnp.int32, st.shape, 1)
        return jnp.where(key <= qry, st, NEG_BIG)

    def values(sub_block, hh):
        return vt_ref[0, hh, sub_block, 0:V_ROWS, :]

    def add_cols(acc, part):
        off = acc.shape[1] - part.shape[1]
        return acc + part if off == 0 else jnp.concatenate([acc[:, :off], acc[:, off:] + part], axis=1)

    if online_max:
        def block(jb, carry, diag):
            new = list(carry)
            for u, hh in items:
                st = scores(jb, u, hh, diag)
                off = tq - st.shape[1]
                m, acc = new[hh]
                m_new = jnp.maximum(m[:, off:], jnp.max(st, axis=0, keepdims=True))
                m_new = m_new if off == 0 else jnp.concatenate([m[:, :off], m_new], axis=1)
                p = jnp.exp2(st - m_new[:, off:]).astype(BF16)
                new[hh] = (m_new, add_cols(acc * jnp.exp2(m - m_new), _dot(values(jb * n_sub + u, hh), p)))
            return tuple(new)

        init = tuple((jnp.full((1, tq), NEG_BIG, F32), zero) for _ in heads)
        carry = lax.fori_loop(0, qi, functools.partial(block, diag=False), init)
        accs = [c[1] for c in block(qi, carry, True)]
    else:
        def blocks(jbs, carry, diag):
            work = [(jb, u, hh) for jb in jbs for u, hh in items]
            tiles = [scores(jb, u, hh, diag) for jb, u, hh in work]
            probs = [jnp.exp2(st).astype(BF16) for st in tiles]
            new = list(carry)
            for (jb, u, hh), p in zip(work, probs):
                new[hh] = add_cols(new[hh], _dot(values(jb * n_sub + u, hh), p))
            return tuple(new)

        def span(jb0, n_blocks, carry):
            n_keys = n_blocks * tq
            start = pl.multiple_of(jb0 * tq, tq)
            tiles = [_dot_nt(k_ref[0, hh, pl.ds(start, n_keys), :], qs[hh]) for hh in heads]
            probs = [jnp.exp2(st).astype(BF16) for st in tiles]
            for hh in heads:
                vt = jnp.concatenate([values(jb0 * n_sub + u, hh) for u in range(n_blocks * n_sub)], axis=1)
                acc_ref[hh] = acc_ref[hh] + _dot(vt, probs[hh])
            return carry

        acc_ref[...] = jnp.zeros_like(acc_ref)
        per_iter = ATTN_BLOCKS_PER_ITER
        lax.fori_loop(0, qi // per_iter, lambda i, c: span(i * per_iter, per_iter, c), 0)
        first_left = (qi // per_iter) * per_iter
        lax.fori_loop(first_left, qi, lambda jb, c: span(jb, 1, c), 0)
        accs = blocks([qi], tuple(acc_ref[hh] for hh in heads), True)
    out_t = jnp.concatenate([acc[:HEAD_DIM] / acc[HEAD_DIM:HEAD_DIM + 1] for acc in accs], axis=0)
    o_ref[0] = out_t.astype(o_ref.dtype).T


def _attention(qa, ka, vt, tq, online_max):
    b, h, s, _ = qa.shape
    hps = ATTN_HEADS_PER_STEP if h % ATTN_HEADS_PER_STEP == 0 else LANES // HEAD_DIM
    return pl.pallas_call(
        functools.partial(_attn_body, heads_per_step=hps, tq=tq, online_max=online_max),
        grid=(b, h // hps, s // tq),
        in_specs=[
            pl.BlockSpec((1, hps, tq, LANES), lambda i, g, j: (i, g, j, 0)),
            pl.BlockSpec((1, hps, s, LANES), lambda i, g, j: (i, g, 0, 0)),
            pl.BlockSpec((1, hps) + vt.shape[2:], lambda i, g, j: (i, g, 0, 0, 0)),
        ],
        out_specs=pl.BlockSpec((1, tq, hps * HEAD_DIM), lambda i, g, j: (i, j, g)),
        out_shape=jax.ShapeDtypeStruct((b, s, h * HEAD_DIM), BF16),
        scratch_shapes=[pltpu.VMEM((hps, V_ROWS, tq), F32)],
        compiler_params=pltpu.CompilerParams(
            dimension_semantics=("parallel", "parallel", "arbitrary"), vmem_limit_bytes=VMEM_LIMIT),
        name="fox_attention_online_max" if online_max else "fox_attention",
    )(qa, ka, vt)


def _outproj_body(x_ref, yrw_ref, yfox_ref, gate_ref, w_ref, fg_ref, o_ref, *, final):
    g = gate_ref[...].astype(F32)
    sg = g * _sigmoid(g)
    d_rw = yrw_ref.shape[1]
    y_rw = (yrw_ref[...].astype(F32) * sg[:, :d_rw]).astype(BF16)
    y_fox = (yfox_ref[...].astype(F32) * sg[:, d_rw:]).astype(BF16)
    z = x_ref[...] + _dot(y_rw, w_ref[0:d_rw, :]) + _dot(y_fox, w_ref[d_rw:, :])
    if final:
        z = (z * lax.rsqrt(jnp.mean(z * z, axis=-1, keepdims=True) + NORM_EPS)) * fg_ref[...]
    o_ref[...] = z


def _outproj(x2, y_rw, y_fox, gate, w_out, final_g, final, tm):
    n, d = x2.shape
    d_rw, d_fox = y_rw.shape[1], y_fox.shape[1]
    return pl.pallas_call(
        functools.partial(_outproj_body, final=final),
        grid=(n // tm,),
        in_specs=[
            pl.BlockSpec((tm, d), lambda i: (i, 0)),
            pl.BlockSpec((tm, d_rw), lambda i: (i, 0)),
            pl.BlockSpec((tm, d_fox), lambda i: (i, 0)),
            pl.BlockSpec((tm, d_rw + d_fox), lambda i: (i, 0)),
            pl.BlockSpec(w_out.shape, lambda i: (0, 0)),
            pl.BlockSpec((1, d), lambda i: (0, 0)),
        ],
        out_specs=pl.BlockSpec((tm, d), lambda i: (i, 0)),
        out_shape=jax.ShapeDtypeStruct((n, d), F32),
        compiler_params=pltpu.CompilerParams(
            dimension_semantics=("parallel",), vmem_limit_bytes=VMEM_LIMIT),
        name="outproj",
    )(x2, y_rw, y_fox, gate, w_out, final_g)


def _pick_tile(n, target):
    t = min(n, target)
    while n % t:
        t //= 2
    return t


def kernel(x, norm_g, w_in, rw_mu, rw_w_up, rw_w0, rw_a_up, rw_a0, rw_k_k, rw_k_a, rw_r_k, rw_gn_g, rw_gn_b,
           fox_q_g, fox_k_g, fox_b_f, w_out, final_g):
    b, s, d_model = x.shape
    depth = norm_g.shape[0]
    d_rw = rw_w0.shape[-1]
    lora = rw_w_up.shape[1]
    d_mix = w_out.shape[1]
    d_fox = d_mix - d_rw
    h_fox = fox_b_f.shape[-1]
    rw_cols = 3 * d_rw + 2 * lora
    fox_cols = 3 * d_fox + h_fox
    assert d_rw == d_fox and s % CHUNK == 0 and h_fox <= LANES

    head = lax.broadcasted_iota(jnp.int32, (MXU_DIM, MXU_DIM), 0) // HEAD_DIM
    ones_bd = (head == head.T).astype(BF16)

    tm_out = _pick_tile(b * s, 1024)
    tm_in = _pick_tile(s, 512)
    tq = _pick_tile(s, 512)

    x2 = x.reshape(b * s, d_model)
    for l in range(depth):
        w = w_in[l]
        w_fl = jnp.pad(w[:, rw_cols + 3 * d_fox:rw_cols + fox_cols], ((0, 0), (0, LANES - h_fox)))
        w_cat = jnp.concatenate(
            [w[:, :rw_cols], w_fl, w[:, rw_cols:rw_cols + 3 * d_fox], w[:, rw_cols + fox_cols:]], axis=1
        ).astype(BF16)
        tile_heads = lambda g: jnp.tile(g, d_fox // HEAD_DIM).reshape(1, -1)
        b_f = jnp.pad(fox_b_f[l], (0, LANES - h_fox)).reshape(1, -1)
        q_g, k_g = tile_heads(fox_q_g[l]), tile_heads(fox_k_g[l])
        xs_rw, gate, qa, ka, va = _inproj_fox(x2.reshape(b, s, d_model), norm_g[l].reshape(1, -1), w_cat,
                                              rw_mu[l].reshape(1, -1), q_g, k_g, b_f, ones_bd, rw_cols, d_mix,
                                              tm_in)

        y_rw = _rwkv(xs_rw, rw_w_up[l], rw_w0[l], rw_a_up[l], rw_a0[l],
                     rw_k_k[l], rw_k_a[l], rw_r_k[l].reshape(-1), rw_gn_g[l], rw_gn_b[l], ones_bd,
                     RWKV_CHUNKS_PER_STEP if s % (RWKV_CHUNKS_PER_STEP * CHUNK) == 0 else 1)

        y_fox = lax.cond(_score_bound(q_g, k_g)[0, 0] <= SAFE_SCORE_BOUND,
                         functools.partial(_attention, tq=tq, online_max=False),
                         functools.partial(_attention, tq=tq, online_max=True), qa, ka, va)

        x2 = _outproj(x2, y_rw.reshape(b * s, d_rw), y_fox.reshape(b * s, d_fox), gate.reshape(b * s, d_mix),
                      w_out[l].astype(BF16), final_g.reshape(1, -1), l == depth - 1, tm_out)
    return x2.reshape(b, s, d_model)
```

```python
import functools
import math

import jax
import jax.numpy as jnp
from jax import lax
from jax.experimental import pallas as pl
from jax.experimental.pallas import tpu as pltpu

F32 = jnp.float32
BF16 = jnp.bfloat16

HEAD_DIM = 64
NORM_EPS = 1e-6
GN_EPS = 64e-5
CHUNK = 64
LOG2_CHUNK = CHUNK.bit_length() - 1
LOG2_BASE = 3
RWKV_CHUNKS_PER_STEP = 4
LANES = 128
MXU_DIM = 256
NEG_BIG = -1e30
ATTN_HEADS_PER_STEP = 4
ATTN_KEY_SUB = 256
ATTN_BLOCKS_PER_ITER = 2
V_ROWS = 80
SAFE_SCORE_BOUND = 40.0
VMEM_LIMIT = 56 * 1024 * 1024


def _dot(a, b):
    return jnp.dot(a, b, preferred_element_type=F32)


def _dot_nt(a, b):
    return lax.dot_general(a, b, (((1,), (1,)), ((), ())), preferred_element_type=F32)


def _dot_tn(a, b):
    return lax.dot_general(a, b, (((0,), (0,)), ((), ())), preferred_element_type=F32)


def _sigmoid(x):
    return 1.0 / (1.0 + jnp.exp(-x))


def _bf(x):
    return x.astype(BF16)


def _split3(x):
    hi = x.astype(BF16)
    r1 = x - hi.astype(F32)
    mid = r1.astype(BF16)
    lo = (r1 - mid.astype(F32)).astype(BF16)
    return hi.astype(F32), mid.astype(F32), lo.astype(F32)


def _dot_pieces(x, ones_bf16, pieces):
    out = None
    for _ in range(pieces):
        hi = x.astype(BF16)
        term = _dot(hi, ones_bf16)
        out = term if out is None else out + term
        x = x - hi.astype(F32)
    return out


def _cumsum_rows(tri_bf16, x, pieces=3):
    out = None
    for _ in range(pieces):
        hi = x.astype(BF16)
        term = _dot(tri_bf16, hi)
        out = term if out is None else out + term
        x = x - hi.astype(F32)
    return out


def _headsum(x, ones_bd, pieces=2):
    width = x.shape[1]
    return jnp.concatenate([_dot_pieces(x[:, j:j + MXU_DIM], ones_bd, pieces) for j in range(0, width, MXU_DIM)],
                           axis=1)


def _score_bound(q_g, k_g):
    amax = lambda g: jnp.max(jnp.abs(g), axis=-1, keepdims=True)
    return (HEAD_DIM ** 0.5) * amax(q_g) * amax(k_g)


def _inproj_fox_body(x_ref, g_ref, w_ref, mu_ref, qg_ref, kg_ref, bf_ref, ones_ref,
                     rw_ref, gate_ref, qa_ref, ka_ref, va_ref, carry_ref, shift_ref, *, n_heads):
    @pl.when(pl.program_id(1) == 0)
    def _():
        carry_ref[...] = jnp.zeros_like(carry_ref)
        shift_ref[...] = jnp.zeros_like(shift_ref)

    x = x_ref[0]
    tm = x.shape[0]
    d = n_heads * HEAD_DIM
    rw_cols, gate_cols = rw_ref.shape[-1], gate_ref.shape[-1]
    off_fl, off_qkv, off_gate = rw_cols, rw_cols + LANES, rw_cols + LANES + 3 * d
    y = x * lax.rsqrt(jnp.mean(x * x, axis=-1, keepdims=True) + NORM_EPS)
    h = (y * g_ref[...]).astype(BF16)

    row8 = lax.broadcasted_iota(jnp.int32, (8, MXU_DIM), 0)

    def finish(ref, dst, out):
        width = out.shape[1]
        if ref is rw_ref:
            cols = slice(dst, dst + width)
            before = shift_ref[:, cols]
            shift_ref[:, cols] = out[tm - 1:tm, :]
            rolled = pltpu.roll(out, 1, axis=0)
            prev = jnp.concatenate([jnp.where(row8[:, :width] == 0, before, rolled[:8]), rolled[8:]], axis=0)
            out = out + (prev - out) * mu_ref[:, cols]
        ref[0, :, dst:dst + width] = out.astype(ref.dtype)

    def project(tile):
        ref, dst, src = tile
        width = min(MXU_DIM, ref.shape[-1] - dst)
        finish(ref, dst, _dot(h, w_ref[:, src:src + width]))

    t = _dot(h, w_ref[:, off_qkv:off_qkv + 3 * d])
    tail = rw_cols % MXU_DIM
    if tail and tail + LANES <= MXU_DIM:
        merged = _dot(h, w_ref[:, rw_cols - tail:rw_cols + LANES])
        finish(rw_ref, rw_cols - tail, merged[:, :tail])
        fl = merged[:, tail:]
    else:
        tail = 0
        fl = _dot(h, w_ref[:, off_fl:off_fl + LANES])

    tiles = [(rw_ref, c0, c0) for c0 in range(0, rw_cols - tail, MXU_DIM)]
    tiles += [(gate_ref, c0, off_gate + c0) for c0 in range(0, gate_cols, MXU_DIM)]

    n_lead = max(len(tiles) - n_heads, 0)
    for tile in tiles[:n_lead]:
        project(tile)

    ones_bd = ones_ref[...]
    inv_n = 1.0 / HEAD_DIM
    log2e = math.log2(math.e)

    def headnorm(z, g):
        ms = _headsum(z * z, ones_bd, pieces=1) * inv_n
        return (z * lax.rsqrt(ms + NORM_EPS)) * g

    q = headnorm(t[:, 0:d], qg_ref[...]) * (HEAD_DIM ** -0.5 * log2e)
    k = headnorm(t[:, d:2 * d], kg_ref[...])
    v = t[:, 2 * d:3 * d]
    neg_bound = -log2e * _score_bound(qg_ref[...], kg_ref[...])

    z = fl + bf_ref[...]
    log_f = jnp.minimum(z, 0.0) - jnp.log1p(jnp.exp(-jnp.abs(z)))
    blk = min(tm, MXU_DIM)
    row = lax.broadcasted_iota(jnp.int32, (blk, blk), 0)
    col = lax.broadcasted_iota(jnp.int32, (blk, blk), 1)
    tri = jnp.where(row >= col, 1.0, 0.0).astype(BF16)
    running, parts = carry_ref[...], []
    for r0 in range(0, tm, blk):
        parts.append(_cumsum_rows(tri, log_f[r0:r0 + blk]) + running)
        running = parts[-1][blk - 1:blk, :]
    cs = jnp.concatenate(parts, axis=0)
    carry_ref[...] = running
    c_hi, c_mid, c_lo = _split3(cs * log2e)

    n = n_heads
    lane = lax.broadcasted_iota(jnp.int32, (1, LANES), 1)
    is_data = lane < HEAD_DIM
    at = lambda off: lane == HEAD_DIM + off
    within = lambda lo, hi: (lane >= HEAD_DIM + lo) & (lane < HEAD_DIM + hi)
    pieces = [jnp.where(lane < n, c, 0.0) for c in (c_hi, c_mid, c_lo)]
    spread = lambda base: sum(pltpu.roll(c, HEAD_DIM + base + i * n, axis=1) for i, c in enumerate(pieces))
    q_extra = spread(0) + jnp.where(within(3 * n, 6 * n + 1), 1.0, 0.0)
    neg_c = -spread(3 * n)
    one_col = jnp.where(at(0), 1.0, 0.0)

    for h_idx in range(n_heads):
        if n_lead + h_idx < len(tiles):
            project(tiles[n_lead + h_idx])
        pair = slice((h_idx // 2) * LANES, (h_idx // 2 + 1) * LANES)
        shift = lambda z: z[:, pair] if h_idx % 2 == 0 else pltpu.roll(z[:, pair], HEAD_DIM, axis=1)
        k_const = jnp.where(at(6 * n), neg_bound,
                            jnp.where(at(h_idx) | at(n + h_idx) | at(2 * n + h_idx), 1.0, 0.0))
        own_c = at(3 * n + h_idx) | at(4 * n + h_idx) | at(5 * n + h_idx)
        k_extra = jnp.where(own_c, neg_c, k_const)
        qa_ref[0, h_idx] = jnp.where(is_data, shift(q), q_extra).astype(BF16)
        ka_ref[0, h_idx] = jnp.where(is_data, shift(k), k_extra).astype(BF16)
        v_t = jnp.where(is_data, shift(v), one_col).T.astype(BF16)
        for sub in range(tm // ATTN_KEY_SUB):
            va_ref[0, h_idx, sub] = v_t[:, sub * ATTN_KEY_SUB:(sub + 1) * ATTN_KEY_SUB]
    for tile in tiles[n_lead + n_heads:]:
        project(tile)


def _inproj_fox(x, g, w_cat, mu, q_g, k_g, b_f, ones_bd, rw_cols, gate_cols, tm):
    b, s, d_model = x.shape
    d = q_g.shape[-1]
    n_heads = d // HEAD_DIM
    const = lambda shape: pl.BlockSpec(shape, lambda i, j: (0,) * len(shape))
    rows = lambda width: pl.BlockSpec((1, tm, width), lambda i, j: (i, j, 0))
    head_spec = pl.BlockSpec((1, n_heads, tm, LANES), lambda i, j: (i, 0, j, 0))
    head_shape = jax.ShapeDtypeStruct((b, n_heads, s, LANES), BF16)
    return pl.pallas_call(
        functools.partial(_inproj_fox_body, n_heads=n_heads),
        grid=(b, s // tm),
        in_specs=[
            rows(d_model), const((1, d_model)),
            pl.BlockSpec(w_cat.shape, lambda i, j: (0, 0), pipeline_mode=pl.Buffered(1)),
            const((1, rw_cols)),
            const((1, d)), const((1, d)), const((1, LANES)), const((MXU_DIM, MXU_DIM)),
        ],
        out_specs=[rows(rw_cols), rows(gate_cols), head_spec, head_spec,
                   pl.BlockSpec((1, n_heads, tm // ATTN_KEY_SUB, LANES, ATTN_KEY_SUB),
                                lambda i, j: (i, 0, j, 0, 0))],
        out_shape=[jax.ShapeDtypeStruct((b, s, rw_cols), F32), jax.ShapeDtypeStruct((b, s, gate_cols), BF16),
                   head_shape, head_shape,
                   jax.ShapeDtypeStruct((b, n_heads, s // ATTN_KEY_SUB, LANES, ATTN_KEY_SUB), BF16)],
        scratch_shapes=[pltpu.VMEM((1, LANES), F32),
                        pltpu.VMEM((1, rw_cols), F32)],
        compiler_params=pltpu.CompilerParams(
            dimension_semantics=("arbitrary", "arbitrary"), vmem_limit_bytes=VMEM_LIMIT),
        name="inproj_fox",
    )(x, g, w_cat, mu, q_g, k_g, b_f, ones_bd)


def _rwkv_body(xs_ref, wup_ref, w0_ref, aup_ref, a0_ref, kk_ref, ka_ref, rk_ref, gng_ref, gnb_ref, ones_ref,
               y_ref, state_ref, m_ref, n_ref, rw_ref, yp_ref, bonus_ref, *, lora, n_chunks):
    @pl.when(pl.program_id(1) == 0)
    def _():
        state_ref[...] = jnp.zeros_like(state_ref)
        for ref in (m_ref, n_ref, rw_ref, yp_ref, bonus_ref):
            ref[...] = jnp.zeros_like(ref)

    L = CHUNK
    P = LANES
    d = y_ref.shape[-1]
    n_pairs = d // P
    ones_bd = ones_ref[...]
    chains = [(c, j) for c in range(n_chunks) for j in range(n_pairs)]
    row_l = lax.broadcasted_iota(jnp.int32, (L, L), 0)
    col_l = lax.broadcasted_iota(jnp.int32, (L, L), 1)
    tri = jnp.where(row_l >= col_l, 1.0, 0.0).astype(BF16)
    w_up, a_up = _bf(wup_ref[...]), _bf(aup_ref[...])

    def state_pass(c):
        out = []
        for j in range(n_pairs):
            idx = c * n_pairs + j
            s0 = _bf(state_ref[j])
            out.append(_dot(rw_ref[idx], s0) + yp_ref[idx])
            state_ref[j] = _dot(m_ref[idx], s0) + n_ref[idx]
        return jnp.concatenate(out, axis=1)

    def prep(c):
        xs = xs_ref[0, c * L:(c + 1) * L, :]
        r = xs[:, 0:d]
        k = xs[:, d:2 * d]
        v = xs[:, 2 * d:3 * d]
        wl = xs[:, 3 * d:3 * d + lora]
        al = xs[:, 3 * d + lora:3 * d + 2 * lora]
        w_raw = w0_ref[...] + _dot(_bf(jnp.tanh(wl)), w_up)
        logw = (-math.exp(-0.5)) * _sigmoid(w_raw)
        a = _sigmoid(a0_ref[...] + _dot(_bf(al), a_up))
        kkr = k * kk_ref[...]
        kk = kkr * lax.rsqrt(jnp.maximum(_headsum(kkr * kkr, ones_bd, pieces=1), 1e-24))
        k2 = k * (1.0 + (a - 1.0) * ka_ref[...])
        bonus = _headsum(r * k2 * rk_ref[...], ones_bd, pieces=1) * v
        cum = _cumsum_rows(tri, logw, pieces=2)
        cum_end = cum[L - 1:L, :]
        e_neg = jnp.exp(-cum)
        e_end = jnp.exp(cum_end - cum)
        kka = kk * a
        return dict(
            v=v, bonus=bonus,
            g_end=jnp.exp(cum_end),
            r_t=r * jnp.exp(cum),
            a_t=-kk * jnp.exp(cum - logw),
            b_t=kka * e_neg, k_t=k2 * e_neg,
            b_e=kka * e_end, k_e=k2 * e_end)

    ys, cur = [], []
    for c in range(n_chunks):
        ys.append(state_pass(c))
        cur.append(prep(c))

    y = jnp.concatenate(ys, axis=0)
    inv_n = 1.0 / HEAD_DIM
    mean = _headsum(y, ones_bd, pieces=1) * inv_n
    yc = y - mean
    var = _headsum(yc * yc, ones_bd, pieces=1) * inv_n
    y_ref[0] = ((yc * lax.rsqrt(var + GN_EPS)) * gng_ref[...] + gnb_ref[...] + bonus_ref[...]).astype(y_ref.dtype)
    bonus_ref[...] = jnp.concatenate([t["bonus"] for t in cur], axis=0)

    pair = lambda name, c, j: cur[c][name][:, j * P:(j + 1) * P]
    lane = lax.broadcasted_iota(jnp.int32, (L, P), 1)
    first = lane < HEAD_DIM
    stack = lambda z: jnp.concatenate([jnp.where(first, z, 0.0), jnp.where(first, 0.0, z)], axis=0)

    row = lax.broadcasted_iota(jnp.int32, (P, P), 0)
    col = lax.broadcasted_iota(jnp.int32, (P, P), 1)
    same = lambda s: (row >> s) == (col >> s)
    strict = same(LOG2_CHUNK) & (row > col)
    incl = same(LOG2_CHUNK) & (row >= col)
    eye = row == col

    amat = [_dot_nt(_bf(jnp.concatenate([stack(pair("a_t", c, j)), stack(pair("r_t", c, j))], axis=0)),
                    _bf(jnp.concatenate([pair("b_t", c, j)] * 2 + [pair("k_t", c, j)] * 2, axis=0)))
            for c, j in chains]
    n_st = [jnp.where(strict, m[:P, :P], 0.0) for m in amat]
    a_ak = [_bf(jnp.where(strict, m[:P, P:], 0.0)) for m in amat]
    a_r = [_bf(jnp.concatenate([jnp.where(incl, m[P:, :P], 0.0), jnp.where(incl, m[P:, P:], 0.0)], axis=1))
           for m in amat]

    n0_f = [jnp.where(same(LOG2_BASE), n, 0.0) for n in n_st]
    n0 = [_bf(n) for n in n0_f]
    n2 = [_bf(_dot(n, n)) for n in n0]
    n4 = [_bf(_dot(n, n)) for n in n2]
    t = [jnp.where(eye, 1.0, n) for n in n0_f]
    t = [x + _dot(_bf(x), m) for x, m in zip(t, n2)]
    t = [x + _dot(_bf(x), m) for x, m in zip(t, n4)]
    for s in range(LOG2_BASE, LOG2_CHUNK):
        off_mask = same(s + 1) & jnp.logical_not(same(s))
        n_off = [_bf(jnp.where(off_mask, n, 0.0)) for n in n_st]
        tb = [_bf(x) for x in t]
        half = [_bf(_dot(x, m)) for x, m in zip(tb, n_off)]
        t = [x + _dot(h, xb) for x, h, xb in zip(t, half, tb)]
    tb = [_bf(x) for x in t]

    v_st = [_bf(stack(pair("v", c, j))) for c, j in chains]
    x_st = [_dot(m, vs) for m, vs in zip(a_ak, v_st)]
    wu = [_dot(x, jnp.concatenate([_bf(stack(pair("a_t", c, j))), _bf(xs_)], axis=1))
          for x, xs_, (c, j) in zip(tb, x_st, chains)]
    zeros_b = jnp.zeros((P, P), BF16)
    rhs2 = [jnp.concatenate([_bf(w), jnp.concatenate([zeros_b, vs], axis=1)], axis=0)
            for w, vs in zip(wu, v_st)]
    top = [_dot(m, rh) for m, rh in zip(a_r, rhs2)]
    bot = [_dot_tn(_bf(jnp.concatenate([stack(pair("b_e", c, j)), stack(pair("k_e", c, j))], axis=0)), rh)
           for rh, (c, j) in zip(rhs2, chains)]

    fold = lambda z: z[:L] + z[L:]
    for idx, (c, j) in enumerate(chains):
        rw_ref[idx] = _bf(pair("r_t", c, j) + fold(top[idx][:, :P]))
        yp_ref[idx] = fold(top[idx][:, P:])
        m_ref[idx] = _bf(jnp.where(eye, pair("g_end", c, j), 0.0) + bot[idx][:, :P])
        n_ref[idx] = bot[idx][:, P:]


def _rwkv(xs_rw, w_up, w0, a_up, a0, k_k, k_a, r_k, gn_g, gn_b, ones_bd, n_chunks):
    b, s, cols = xs_rw.shape
    lora, d = w_up.shape
    rows = n_chunks * CHUNK
    n_tiles = s // rows
    n_chains = n_chunks * (d // LANES)
    row = lambda a: a.reshape(1, -1)
    const = lambda shape: pl.BlockSpec(shape, lambda i, j: (0,) * len(shape))
    return pl.pallas_call(
        functools.partial(_rwkv_body, lora=lora, n_chunks=n_chunks),
        grid=(b, n_tiles + 1),
        in_specs=[
            pl.BlockSpec((1, rows, cols), lambda i, j: (i, jnp.minimum(j, n_tiles - 1), 0)),
            const((lora, d)), const((1, d)), const((lora, d)), const((1, d)),
            const((1, d)), const((1, d)), const((1, d)), const((1, d)), const((1, d)), const((MXU_DIM, MXU_DIM)),
        ],
        out_specs=pl.BlockSpec((1, rows, d), lambda i, j: (i, jnp.maximum(j - 1, 0), 0)),
        out_shape=jax.ShapeDtypeStruct((b, s, d), BF16),
        scratch_shapes=[
            pltpu.VMEM((d // LANES, LANES, LANES), F32),
            pltpu.VMEM((n_chains, LANES, LANES), BF16),
            pltpu.VMEM((n_chains, LANES, LANES), F32),
            pltpu.VMEM((n_chains, CHUNK, LANES), BF16),
            pltpu.VMEM((n_chains, CHUNK, LANES), F32),
            pltpu.VMEM((rows, d), F32),
        ],
        compiler_params=pltpu.CompilerParams(
            dimension_semantics=("arbitrary", "arbitrary"), vmem_limit_bytes=VMEM_LIMIT),
        name="rwkv7",
    )(xs_rw, w_up, row(w0), a_up, row(a0), row(k_k), row(k_a), row(r_k), row(gn_g), row(gn_b), ones_bd)


def _attn_body(q_ref, k_ref, vt_ref, o_ref, acc_ref, *, heads_per_step, tq, online_max):
    qi = pl.program_id(2)
    tk = vt_ref.shape[-1]
    n_sub = tq // tk
    heads = range(heads_per_step)
    qs = [q_ref[0, hh] for hh in heads]

    items = [(u, hh) for u in range(n_sub) for hh in heads]
    zero = jnp.zeros((V_ROWS, tq), F32)

    def scores(jb, u, hh, diag):
        kb = k_ref[0, hh, pl.ds(pl.multiple_of((jb * n_sub + u) * tk, tk), tk), :]
        if not diag:
            return _dot_nt(kb, qs[hh])
        st = _dot_nt(kb, qs[hh][u * tk:, :])
        key = lax.broadcasted_iota(jnp.int32, st.shape, 0)
        qry = lax.broadcasted_iota(jnp.int32, st.shape, 1)
        return jnp.where(key <= qry, st, NEG_BIG)

    def values(sub_block, hh):
        return vt_ref[0, hh, sub_block, 0:V_ROWS, :]

    def add_cols(acc, part):
        off = acc.shape[1] - part.shape[1]
        return acc + part if off == 0 else jnp.concatenate([acc[:, :off], acc[:, off:] + part], axis=1)

    if online_max:
        def block(jb, carry, diag):
            new = list(carry)
            for u, hh in items:
                st = scores(jb, u, hh, diag)
                off = tq - st.shape[1]
                m, acc = new[hh]
                m_new = jnp.maximum(m[:, off:], jnp.max(st, axis=0, keepdims=True))
                m_new = m_new if off == 0 else jnp.concatenate([m[:, :off], m_new], axis=1)
                p = jnp.exp2(st - m_new[:, off:]).astype(BF16)
                new[hh] = (m_new, add_cols(acc * jnp.exp2(m - m_new), _dot(values(jb * n_sub + u, hh), p)))
            return tuple(new)

        init = tuple((jnp.full((1, tq), NEG_BIG, F32), zero) for _ in heads)
        carry = lax.fori_loop(0, qi, functools.partial(block, diag=False), init)
        accs = [c[1] for c in block(qi, carry, True)]
    else:
        def blocks(jbs, carry, diag):
            work = [(jb, u, hh) for jb in jbs for u, hh in items]
            tiles = [scores(jb, u, hh, diag) for jb, u, hh in work]
            probs = [jnp.exp2(st).astype(BF16) for st in tiles]
            new = list(carry)
            for (jb, u, hh), p in zip(work, probs):
                new[hh] = add_cols(new[hh], _dot(values(jb * n_sub + u, hh), p))
            return tuple(new)

        def span(jb0, n_blocks, carry):
            n_keys = n_blocks * tq
            start = pl.multiple_of(jb0 * tq, tq)
            tiles = [_dot_nt(k_ref[0, hh, pl.ds(start, n_keys), :], qs[hh]) for hh in heads]
            probs = [jnp.exp2(st).astype(BF16) for st in tiles]
            for hh in heads:
                vt = jnp.concatenate([values(jb0 * n_sub + u, hh) for u in range(n_blocks * n_sub)], axis=1)
                acc_ref[hh] = acc_ref[hh] + _dot(vt, probs[hh])
            return carry

        acc_ref[...] = jnp.zeros_like(acc_ref)
        per_iter = ATTN_BLOCKS_PER_ITER
        lax.fori_loop(0, qi // per_iter, lambda i, c: span(i * per_iter, per_iter, c), 0)
        first_left = (qi // per_iter) * per_iter
        lax.fori_loop(first_left, qi, lambda jb, c: span(jb, 1, c), 0)
        accs = blocks([qi], tuple(acc_ref[hh] for hh in heads), True)
    out_t = jnp.concatenate([acc[:HEAD_DIM] / acc[HEAD_DIM:HEAD_DIM + 1] for acc in accs], axis=0)
    o_ref[0] = out_t.astype(o_ref.dtype).T


def _attention(qa, ka, vt, tq, online_max):
    b, h, s, _ = qa.shape
    hps = ATTN_HEADS_PER_STEP if h % ATTN_HEADS_PER_STEP == 0 else LANES // HEAD_DIM
    return pl.pallas_call(
        functools.partial(_attn_body, heads_per_step=hps, tq=tq, online_max=online_max),
        grid=(b, h // hps, s // tq),
        in_specs=[
            pl.BlockSpec((1, hps, tq, LANES), lambda i, g, j: (i, g, j, 0)),
            pl.BlockSpec((1, hps, s, LANES), lambda i, g, j: (i, g, 0, 0)),
            pl.BlockSpec((1, hps) + vt.shape[2:], lambda i, g, j: (i, g, 0, 0, 0)),
        ],
        out_specs=pl.BlockSpec((1, tq, hps * HEAD_DIM), lambda i, g, j: (i, j, g)),
        out_shape=jax.ShapeDtypeStruct((b, s, h * HEAD_DIM), BF16),
        scratch_shapes=[pltpu.VMEM((hps, V_ROWS, tq), F32)],
        compiler_params=pltpu.CompilerParams(
            dimension_semantics=("parallel", "parallel", "arbitrary"), vmem_limit_bytes=VMEM_LIMIT),
        name="fox_attention_online_max" if online_max else "fox_attention",
    )(qa, ka, vt)


def _outproj_body(x_ref, yrw_ref, yfox_ref, gate_ref, w_ref, fg_ref, o_ref, *, final):
    g = gate_ref[...].astype(F32)
    sg = g * _sigmoid(g)
    d_rw = yrw_ref.shape[1]
    y_rw = (yrw_ref[...].astype(F32) * sg[:, :d_rw]).astype(BF16)
    y_fox = (yfox_ref[...].astype(F32) * sg[:, d_rw:]).astype(BF16)
    z = x_ref[...] + _dot(y_rw, w_ref[0:d_rw, :]) + _dot(y_fox, w_ref[d_rw:, :])
    if final:
        z = (z * lax.rsqrt(jnp.mean(z * z, axis=-1, keepdims=True) + NORM_EPS)) * fg_ref[...]
    o_ref[...] = z


def _outproj(x2, y_rw, y_fox, gate, w_out, final_g, final, tm):
    n, d = x2.shape
    d_rw, d_fox = y_rw.shape[1], y_fox.shape[1]
    return pl.pallas_call(
        functools.partial(_outproj_body, final=final),
        grid=(n // tm,),
        in_specs=[
            pl.BlockSpec((tm, d), lambda i: (i, 0)),
            pl.BlockSpec((tm, d_rw), lambda i: (i, 0)),
            pl.BlockSpec((tm, d_fox), lambda i: (i, 0)),
            pl.BlockSpec((tm, d_rw + d_fox), lambda i: (i, 0)),
            pl.BlockSpec(w_out.shape, lambda i: (0, 0)),
            pl.BlockSpec((1, d), lambda i: (0, 0)),
        ],
        out_specs=pl.BlockSpec((tm, d), lambda i: (i, 0)),
        out_shape=jax.ShapeDtypeStruct((n, d), F32),
        compiler_params=pltpu.CompilerParams(
            dimension_semantics=("parallel",), vmem_limit_bytes=VMEM_LIMIT),
        name="outproj",
    )(x2, y_rw, y_fox, gate, w_out, final_g)


def _pick_tile(n, target):
    t = min(n, target)
    while n % t:
        t //= 2
    return t


def kernel(x, norm_g, w_in, rw_mu, rw_w_up, rw_w0, rw_a_up, rw_a0, rw_k_k, rw_k_a, rw_r_k, rw_gn_g, rw_gn_b,
           fox_q_g, fox_k_g, fox_b_f, w_out, final_g):
    b, s, d_model = x.shape
    depth = norm_g.shape[0]
    d_rw = rw_w0.shape[-1]
    lora = rw_w_up.shape[1]
    d_mix = w_out.shape[1]
    d_fox = d_mix - d_rw
    h_fox = fox_b_f.shape[-1]
    rw_cols = 3 * d_rw + 2 * lora
    fox_cols = 3 * d_fox + h_fox
    assert d_rw == d_fox and s % CHUNK == 0 and h_fox <= LANES

    head = lax.broadcasted_iota(jnp.int32, (MXU_DIM, MXU_DIM), 0) // HEAD_DIM
    ones_bd = (head == head.T).astype(BF16)

    tm_out = _pick_tile(b * s, 1024)
    tm_in = _pick_tile(s, 512)
    tq = _pick_tile(s, 512)

    x2 = x.reshape(b * s, d_model)
    for l in range(depth):
        w = w_in[l]
        w_fl = jnp.pad(w[:, rw_cols + 3 * d_fox:rw_cols + fox_cols], ((0, 0), (0, LANES - h_fox)))
        w_cat = jnp.concatenate(
            [w[:, :rw_cols], w_fl, w[:, rw_cols:rw_cols + 3 * d_fox], w[:, rw_cols + fox_cols:]], axis=1
        ).astype(BF16)
        tile_heads = lambda g: jnp.tile(g, d_fox // HEAD_DIM).reshape(1, -1)
        b_f = jnp.pad(fox_b_f[l], (0, LANES - h_fox)).reshape(1, -1)
        q_g, k_g = tile_heads(fox_q_g[l]), tile_heads(fox_k_g[l])
        xs_rw, gate, qa, ka, va = _inproj_fox(x2.reshape(b, s, d_model), norm_g[l].reshape(1, -1), w_cat,
                                              rw_mu[l].reshape(1, -1), q_g, k_g, b_f, ones_bd, rw_cols, d_mix,
                                              tm_in)

        y_rw = _rwkv(xs_rw, rw_w_up[l], rw_w0[l], rw_a_up[l], rw_a0[l],
                     rw_k_k[l], rw_k_a[l], rw_r_k[l].reshape(-1), rw_gn_g[l], rw_gn_b[l], ones_bd,
                     RWKV_CHUNKS_PER_STEP if s % (RWKV_CHUNKS_PER_STEP * CHUNK) == 0 else 1)

        y_fox = lax.cond(_score_bound(q_g, k_g)[0, 0] <= SAFE_SCORE_BOUND,
                         functools.partial(_attention, tq=tq, online_max=False),
                         functools.partial(_attention, tq=tq, online_max=True), qa, ka, va)

        x2 = _outproj(x2, y_rw.reshape(b * s, d_rw), y_fox.reshape(b * s, d_fox), gate.reshape(b * s, d_mix),
                      w_out[l].astype(BF16), final_g.reshape(1, -1), l == depth - 1, tm_out)
    return x2.reshape(b, s, d_model)
```
